```python
import math
import jax, jax.numpy as jnp
from jax import lax
import numpy as np

D_MODEL = 2048
BATCH = 4
SEQ = 2048
DEPTH = 1
DEC_BATCH = 128
DEC_SEQ = 8
PAST_LEN = 16384
PAGE_SIZE = 128

D_A = D_MODEL // 2
CHUNK = 128
SGU_GROUPS = 8
SGU_HEAD = D_A // SGU_GROUPS
D_B = D_MODEL // 2
CONV_WIDTH = 31
N_MEM = 256
N_MEM_HEADS = 4
D_C = D_MODEL // 2
MEM_HEAD_DIM = D_C // N_MEM_HEADS
N_BRANCH = 3
D_FF = 4 * D_MODEL
N_IN = 2 * D_A + 2 * D_B + D_C + N_BRANCH * D_MODEL
SPLITS = [D_A, 2 * D_A, 2 * D_A + D_B, 2 * D_A + 2 * D_B, 2 * D_A + 2 * D_B + D_C]
EPS = 1e-6

kernel_name = "gated_parallel_sgu_conformer_memattn_decoder_step"


def rms_norm(x, g):
    xf = x.astype(jnp.float32)
    y = xf * lax.rsqrt(jnp.mean(xf * xf, axis=-1, keepdims=True) + EPS)
    return (y * g.astype(jnp.float32)).astype(x.dtype)


def layer_norm(x, g, b):
    xf = x.astype(jnp.float32)
    mu = jnp.mean(xf, axis=-1, keepdims=True)
    xc = xf - mu
    y = xc * lax.rsqrt(jnp.mean(xc * xc, axis=-1, keepdims=True) + EPS)
    return (y * g.astype(jnp.float32) + b.astype(jnp.float32)).astype(x.dtype)


def mem_kv(mem, mem_norm_g, w_mem_kv):
    b = mem.shape[0]
    kv = rms_norm(mem, mem_norm_g) @ w_mem_kv
    k, v = jnp.split(kv, 2, axis=-1)
    return (k.reshape(b, N_MEM, N_MEM_HEADS, MEM_HEAD_DIM),
            v.reshape(b, N_MEM, N_MEM_HEADS, MEM_HEAD_DIM))


def causal_dwconv(c_ext, conv_w, conv_b):
    out = lax.conv_general_dilated(
        c_ext, conv_w[:, None, :].astype(c_ext.dtype), window_strides=(1,), padding='VALID',
        dimension_numbers=('NWC', 'WIO', 'NWC'), feature_group_count=D_B)
    return out + conv_b


def layer(x, mem_k, mem_v, conv_hist,
          attn_norm_g, w_in, b_gate, sgu_ln_g, sgu_ln_b, sgu_w, sgu_b, w_a_out,
          conv_w, conv_b, conv_ln_g, conv_ln_b, w_b_out, w_c_out, w_o,
          mlp_norm_g, w_up, w_down):
    bsz, s, _ = x.shape
    h = rms_norm(x, attn_norm_g)
    z = h @ w_in
    u, v, glu_a, glu_b, q, gates = jnp.split(z, SPLITS, axis=-1)

    u = jax.nn.gelu(u)
    v = layer_norm(jax.nn.gelu(v), sgu_ln_g, sgu_ln_b)
    n = min(s, CHUNK)
    vc = v.reshape(bsz, s // n, n, SGU_GROUPS, SGU_HEAD)
    w_s = sgu_w[:, :n, :n] * jnp.tril(jnp.ones((n, n), dtype=sgu_w.dtype))
    mixed = jnp.einsum('gts,bcsgh->bctgh', w_s, vc) + sgu_b[:, :n].T[None, None, :, :, None]
    y_a = (u * mixed.reshape(bsz, s, D_A)) @ w_a_out

    c = glu_a * jax.nn.sigmoid(glu_b)
    c_ext = jnp.concatenate([conv_hist.astype(c.dtype), c], axis=1)
    dc = causal_dwconv(c_ext, conv_w, conv_b)
    y_b = jax.nn.silu(layer_norm(dc, conv_ln_g, conv_ln_b)) @ w_b_out
    new_hist = c_ext[:, -(CONV_WIDTH - 1):]

    qh = q.reshape(bsz, s, N_MEM_HEADS, MEM_HEAD_DIM)
    scores = jnp.einsum('bshd,bmhd->bhsm', qh, mem_k).astype(jnp.float32) / math.sqrt(MEM_HEAD_DIM)
    probs = jax.nn.softmax(scores, axis=-1).astype(mem_v.dtype)
    o = jnp.einsum('bhsm,bmhd->bshd', probs, mem_v).reshape(bsz, s, D_C)
    y_c = o @ w_c_out

    g = jax.nn.sigmoid(gates + b_gate).reshape(bsz, s, N_BRANCH, D_MODEL)
    merged = g[:, :, 0] * y_a + g[:, :, 1] * y_b + g[:, :, 2] * y_c
    x = x + merged @ w_o

    h2 = rms_norm(x, mlp_norm_g)
    x = x + jnp.square(jax.nn.relu(h2 @ w_up)) @ w_down
    return x, new_hist, v


def setup_inputs(seed: int = 0) -> dict:
    key = jax.random.key(seed)
    ks = jax.random.split(key, 32)
    f32 = jnp.float32

    def nrm(k, shape, scale):
        return jax.random.normal(k, shape, f32) * scale

    return {
        "x_prompt": nrm(ks[0], (BATCH, SEQ, D_MODEL), 1.0),
        "x_sample": nrm(ks[1], (DEC_BATCH, DEC_SEQ, D_MODEL), 1.0),
        "mem_prompt": nrm(ks[2], (BATCH, N_MEM, D_MODEL), 1.0),
        "cache_mem_k": nrm(ks[3], (DEPTH, DEC_BATCH, N_MEM, N_MEM_HEADS, MEM_HEAD_DIM), 1.0),
        "cache_mem_v": nrm(ks[4], (DEPTH, DEC_BATCH, N_MEM, N_MEM_HEADS, MEM_HEAD_DIM), 1.0),
        "state_conv": nrm(ks[5], (DEPTH, DEC_BATCH, CONV_WIDTH - 1, D_B), 0.5),
        "attn_norm_g": 1.0 + nrm(ks[6], (DEPTH, D_MODEL), 0.02),
        "w_in": nrm(ks[7], (DEPTH, D_MODEL, N_IN), D_MODEL ** -0.5),
        "b_gate": nrm(ks[8], (DEPTH, N_BRANCH * D_MODEL), 0.02),
        "sgu_ln_g": 1.0 + nrm(ks[9], (DEPTH, D_A), 0.02),
        "sgu_ln_b": nrm(ks[10], (DEPTH, D_A), 0.02),
        "sgu_w": nrm(ks[11], (DEPTH, SGU_GROUPS, CHUNK, CHUNK), CHUNK ** -0.5),
        "sgu_b": 1.0 + nrm(ks[12], (DEPTH, SGU_GROUPS, CHUNK), 0.1),
        "w_a_out": nrm(ks[13], (DEPTH, D_A, D_MODEL), D_A ** -0.5),
        "conv_w": nrm(ks[14], (DEPTH, CONV_WIDTH, D_B), CONV_WIDTH ** -0.5),
        "conv_b": nrm(ks[15], (DEPTH, D_B), 0.02),
        "conv_ln_g": 1.0 + nrm(ks[16], (DEPTH, D_B), 0.02),
        "conv_ln_b": nrm(ks[17], (DEPTH, D_B), 0.02),
        "w_b_out": nrm(ks[18], (DEPTH, D_B, D_MODEL), D_B ** -0.5),
        "mem_norm_g": 1.0 + nrm(ks[19], (DEPTH, D_MODEL), 0.02),
        "w_mem_kv": nrm(ks[20], (DEPTH, D_MODEL, 2 * D_C), D_MODEL ** -0.5),
        "w_c_out": nrm(ks[21], (DEPTH, D_C, D_MODEL), D_C ** -0.5),
        "w_o": nrm(ks[22], (DEPTH, D_MODEL, D_MODEL), D_MODEL ** -0.5),
        "mlp_norm_g": 1.0 + nrm(ks[23], (DEPTH, D_MODEL), 0.02),
        "w_up": nrm(ks[24], (DEPTH, D_MODEL, D_FF), D_MODEL ** -0.5),
        "w_down": nrm(ks[25], (DEPTH, D_FF, D_MODEL), D_FF ** -0.5),
        "final_norm_g": 1.0 + nrm(ks[26], (D_MODEL,), 0.02),
    }


def reference(x_prompt, x_sample, mem_prompt, cache_mem_k, cache_mem_v, state_conv,
              attn_norm_g, w_in, b_gate, sgu_ln_g, sgu_ln_b, sgu_w, sgu_b, w_a_out,
              conv_w, conv_b, conv_ln_g, conv_ln_b, w_b_out, mem_norm_g, w_mem_kv,
              w_c_out, w_o, mlp_norm_g, w_up, w_down, final_norm_g):
    xp, xs = x_prompt, x_sample
    mk_p, mv_p, hist_p, hist_s, chunk_v_s = [], [], [], [], []
    for l in range(DEPTH):
        lw = (attn_norm_g[l], w_in[l], b_gate[l], sgu_ln_g[l], sgu_ln_b[l], sgu_w[l], sgu_b[l],
              w_a_out[l], conv_w[l], conv_b[l], conv_ln_g[l], conv_ln_b[l], w_b_out[l],
              w_c_out[l], w_o[l], mlp_norm_g[l], w_up[l], w_down[l])
        k_p, v_p = mem_kv(mem_prompt, mem_norm_g[l], w_mem_kv[l])
        zero_hist = jnp.zeros((xp.shape[0], CONV_WIDTH - 1, D_B), xp.dtype)
        xp, hp, _ = layer(xp, k_p, v_p, zero_hist, *lw)
        xs, hs, vs = layer(xs, cache_mem_k[l], cache_mem_v[l], state_conv[l], *lw)
        mk_p.append(k_p)
        mv_p.append(v_p)
        hist_p.append(hp)
        hist_s.append(hs)
        chunk_v_s.append(vs)
    y_prompt = rms_norm(xp, final_norm_g)
    y_sample = rms_norm(xs, final_norm_g)
    new_mem_k_prompt = jnp.stack(mk_p, axis=0)
    new_mem_v_prompt = jnp.stack(mv_p, axis=0)
    new_conv_state_prompt = jnp.stack(hist_p, axis=0)
    new_conv_state_sample = jnp.stack(hist_s, axis=0)
    new_chunk_v_sample = jnp.stack(chunk_v_s, axis=0)
    return (y_prompt, y_sample, new_mem_k_prompt, new_mem_v_prompt,
            new_conv_state_prompt, new_conv_state_sample, new_chunk_v_sample)
```

```python
import functools
import math

import jax
import jax.numpy as jnp
from jax import lax
from jax.experimental import pallas as pl
from jax.experimental.pallas import tpu as pltpu

F32 = jnp.float32
BF16 = jnp.bfloat16

D_MODEL = 2048
D_A = 1024
D_B = 1024
D_C = 1024
CHUNK = 128
SGU_GROUPS = 8
SGU_HEAD = D_A // SGU_GROUPS
CONV_WIDTH = 31
HIST = CONV_WIDTH - 1
N_MEM = 256
N_MEM_HEADS = 4
MEM_HEAD_DIM = D_C // N_MEM_HEADS
N_BRANCH = 3
D_FF = 4 * D_MODEL
N_IN = 2 * D_A + 2 * D_B + D_C + N_BRANCH * D_MODEL
EPS = 1e-6

COL_U, COL_V, COL_GLU_A, COL_GLU_B, COL_Q, COL_GATES = 0, 1, 2, 3, 4, 5
COL_W = 1024

V7X_VMEM_BYTES = 64 * 1024 * 1024
VMEM_LIMIT_BYTES = V7X_VMEM_BYTES - 8 * 1024 * 1024
SUBLANES = 8
HIST_PAD = 32


def _cparams(*semantics):
    return pltpu.CompilerParams(dimension_semantics=semantics, vmem_limit_bytes=VMEM_LIMIT_BYTES)


def _rms_norm(x, g):
    ms = jnp.mean(x * x, axis=-1, keepdims=True)
    return x * lax.rsqrt(ms + EPS) * g


def _layer_norm(x, g, b):
    mu = jnp.mean(x, axis=-1, keepdims=True)
    xc = x - mu
    var = jnp.mean(xc * xc, axis=-1, keepdims=True)
    return xc * lax.rsqrt(var + EPS) * g + b


NORM_ROWS = 128


def _norm_matmul_kernel(x_ref, g_ref, w_ref, o_ref, h_ref):
    @pl.when(pl.program_id(1) == 0)
    def _():
        def body(r, carry):
            rows = pl.ds(pl.multiple_of(r * NORM_ROWS, NORM_ROWS), NORM_ROWS)
            h_ref[rows, :] = _rms_norm(x_ref[rows, :], g_ref[...]).astype(BF16)
            return carry

        lax.fori_loop(0, x_ref.shape[0] // NORM_ROWS, body, 0)

    o_ref[...] = jnp.dot(h_ref[...], w_ref[...], preferred_element_type=F32)


def _norm_matmul(x, g, w, *, tm, tn):
    m, k = x.shape
    n = w.shape[1]
    return pl.pallas_call(
        _norm_matmul_kernel,
        out_shape=jax.ShapeDtypeStruct((m, n), F32),
        grid=(m // tm, n // tn),
        in_specs=[
            pl.BlockSpec((tm, k), lambda i, j: (i, 0)),
            pl.BlockSpec((1, k), lambda i, j: (0, 0)),
            pl.BlockSpec((k, tn), lambda i, j: (0, j)),
        ],
        out_specs=pl.BlockSpec((tm, tn), lambda i, j: (i, j)),
        scratch_shapes=[pltpu.VMEM((tm, k), BF16)],
        compiler_params=_cparams("arbitrary", "arbitrary"),
        name="norm_matmul",
    )(x, g.reshape(1, k), w)


def _sgu_kernel(u_ref, v_ref, g_ref, b_ref, w_ref, bias_ref, a_ref, *vn_refs, chunk):
    r = lax.broadcasted_iota(jnp.int32, (CHUNK, CHUNK), 0)
    c = lax.broadcasted_iota(jnp.int32, (CHUNK, CHUNK), 1)
    seg_bits = chunk.bit_length() - 1
    same_segment = (r >> seg_bits) == (c >> seg_bits)
    mask = jnp.logical_and(same_segment, (r & (chunk - 1)) >= (c & (chunk - 1)))
    w_s = [jnp.where(mask, w_ref[grp], 0.0).astype(BF16) for grp in range(SGU_GROUPS)]
    for blk in range(u_ref.shape[0] // CHUNK):
        rows = slice(blk * CHUNK, (blk + 1) * CHUNK)
        vn = _layer_norm(jax.nn.gelu(v_ref[rows, :]), g_ref[...], b_ref[...])
        if vn_refs:
            vn_refs[0][rows, :] = vn
        vn16 = vn.astype(BF16)
        for grp in range(SGU_GROUPS):
            cols = slice(grp * SGU_HEAD, (grp + 1) * SGU_HEAD)
            mixed = jnp.dot(w_s[grp], vn16[:, cols], preferred_element_type=F32) + bias_ref[:, cols]
            a_ref[rows, cols] = (jax.nn.gelu(u_ref[rows, cols]) * mixed).astype(BF16)


def _sgu(z, ln_g, ln_b, w_tiled, bias_rows, *, chunk, rows, want_vn):
    m = z.shape[0]
    out_shape = [jax.ShapeDtypeStruct((m, D_A), BF16)]
    out_specs = [pl.BlockSpec((rows, D_A), lambda i: (i, 0))]
    if want_vn:
        out_shape.append(jax.ShapeDtypeStruct((m, D_A), F32))
        out_specs.append(pl.BlockSpec((rows, D_A), lambda i: (i, 0)))
    return pl.pallas_call(
        functools.partial(_sgu_kernel, chunk=chunk),
        out_shape=out_shape,
        grid=(m // rows,),
        in_specs=[
            pl.BlockSpec((rows, COL_W), lambda i: (i, COL_U)),
            pl.BlockSpec((rows, COL_W), lambda i: (i, COL_V)),
            pl.BlockSpec((1, D_A), lambda i: (0, 0)),
            pl.BlockSpec((1, D_A), lambda i: (0, 0)),
            pl.BlockSpec((SGU_GROUPS, CHUNK, CHUNK), lambda i: (0, 0, 0)),
            pl.BlockSpec((CHUNK, D_A), lambda i: (0, 0)),
        ],
        out_specs=out_specs,
        compiler_params=_cparams("arbitrary"),
        name="sgu",
    )(z, z, ln_g.reshape(1, D_A), ln_b.reshape(1, D_A), w_tiled, bias_rows)


CONV_ROWS = 32
CONV_LANES = 256
LN_ROWS = 64
CONV_LEAD = HIST_PAD - HIST


def _conv_window(win, w_ref, lanes, n_rows):
    out = None
    for b in range(SUBLANES):
        rows_b = n_rows if b == 0 else n_rows + SUBLANES
        y = None
        for a in range(HIST_PAD // SUBLANES + 1):
            k = SUBLANES * a + b - CONV_LEAD
            if 0 <= k < CONV_WIDTH:
                term = w_ref[k : k + 1, lanes] * win[SUBLANES * a : SUBLANES * a + rows_b, :]
                y = term if y is None else y + term
        shifted = y[b : b + n_rows, :]
        out = shifted if out is None else out + shifted
    return out


def _ln_silu_rows(dc_ref, cb_ref, g_ref, b_ref, o_ref, rows):
    dc = dc_ref[rows, :] + cb_ref[...]
    o_ref[rows, :] = jax.nn.silu(_layer_norm(dc, g_ref[...], b_ref[...])).astype(BF16)


def _conv_prompt_kernel(ga_ref, gb_ref, w_ref, cb_ref, g_ref, b_ref, o_ref, hist_ref, ext_ref, dc_ref):
    t = ga_ref.shape[0]
    i = pl.program_id(1)

    @pl.when(i == 0)
    def _():
        ext_ref[0:HIST_PAD, :] = jnp.zeros((HIST_PAD, D_B), F32)

    @pl.when(i > 0)
    def _():
        ext_ref[0:HIST_PAD, :] = ext_ref[t : t + HIST_PAD, :]

    ext_ref[HIST_PAD : HIST_PAD + t, :] = ga_ref[...] * jax.nn.sigmoid(gb_ref[...])

    def conv_body(r, carry):
        r0 = pl.multiple_of(r * CONV_ROWS, CONV_ROWS)
        for lb in range(D_B // CONV_LANES):
            lanes = slice(lb * CONV_LANES, (lb + 1) * CONV_LANES)
            win = ext_ref[pl.ds(r0, CONV_ROWS + HIST_PAD), lanes]
            dc_ref[pl.ds(r0, CONV_ROWS), lanes] = _conv_window(win, w_ref, lanes, CONV_ROWS)
        return carry

    lax.fori_loop(0, t // CONV_ROWS, conv_body, 0)

    def ln_body(r, carry):
        rows = pl.ds(pl.multiple_of(r * LN_ROWS, LN_ROWS), LN_ROWS)
        _ln_silu_rows(dc_ref, cb_ref, g_ref, b_ref, o_ref, rows)
        return carry

    lax.fori_loop(0, t // LN_ROWS, ln_body, 0)
    hist_ref[0] = ext_ref[t + CONV_LEAD : t + HIST_PAD, :]


def _conv_prompt(z, batch, seq, conv_w, conv_b, ln_g, ln_b, *, t):
    steps = seq // t
    return pl.pallas_call(
        _conv_prompt_kernel,
        out_shape=[
            jax.ShapeDtypeStruct((batch * seq, D_B), BF16),
            jax.ShapeDtypeStruct((batch, HIST, D_B), F32),
        ],
        grid=(batch, steps),
        in_specs=[
            pl.BlockSpec((t, COL_W), lambda b, i: (b * steps + i, COL_GLU_A)),
            pl.BlockSpec((t, COL_W), lambda b, i: (b * steps + i, COL_GLU_B)),
            pl.BlockSpec((CONV_WIDTH, D_B), lambda b, i: (0, 0)),
            pl.BlockSpec((1, D_B), lambda b, i: (0, 0)),
            pl.BlockSpec((1, D_B), lambda b, i: (0, 0)),
            pl.BlockSpec((1, D_B), lambda b, i: (0, 0)),
        ],
        out_specs=[
            pl.BlockSpec((t, D_B), lambda b, i: (b * steps + i, 0)),
            pl.BlockSpec((1, HIST, D_B), lambda b, i: (b, 0, 0)),
        ],
        scratch_shapes=[pltpu.VMEM((t + HIST_PAD, D_B), F32), pltpu.VMEM((t, D_B), F32)],
        compiler_params=_cparams("arbitrary", "arbitrary"),
        name="conv_prompt",
    )(z, z, conv_w, conv_b.reshape(1, D_B), ln_g.reshape(1, D_B), ln_b.reshape(1, D_B))


def _conv_sample_kernel(ga_ref, gb_ref, hist_ref, w_ref, cb_ref, g_ref, b_ref, o_ref, nh_ref, ext_ref, dc_ref,
                        *, seq):
    n_seq = hist_ref.shape[0]
    c = ga_ref[...] * jax.nn.sigmoid(gb_ref[...])
    for s in range(n_seq):
        ext_ref[s, 0:SUBLANES, :] = jnp.zeros((SUBLANES, D_B), F32)
        ext_ref[s, CONV_LEAD:HIST_PAD, :] = hist_ref[s]
        ext_ref[s, HIST_PAD : HIST_PAD + seq, :] = c[s * seq : (s + 1) * seq, :]
    for s in range(n_seq):
        for lb in range(D_B // CONV_LANES):
            lanes = slice(lb * CONV_LANES, (lb + 1) * CONV_LANES)
            dc_ref[s * seq : (s + 1) * seq, lanes] = _conv_window(ext_ref[s, :, lanes], w_ref, lanes, seq)
        nh_ref[s] = ext_ref[s, seq + CONV_LEAD : seq + HIST_PAD, :]
    _ln_silu_rows(dc_ref, cb_ref, g_ref, b_ref, o_ref, slice(None))


def _conv_sample(z, hist, seq, conv_w, conv_b, ln_g, ln_b, *, n_seq):
    batch = hist.shape[0]
    return pl.pallas_call(
        functools.partial(_conv_sample_kernel, seq=seq),
        out_shape=[
            jax.ShapeDtypeStruct((batch * seq, D_B), BF16),
            jax.ShapeDtypeStruct((batch, HIST, D_B), F32),
        ],
        grid=(batch // n_seq,),
        in_specs=[
            pl.BlockSpec((n_seq * seq, COL_W), lambda i: (i, COL_GLU_A)),
            pl.BlockSpec((n_seq * seq, COL_W), lambda i: (i, COL_GLU_B)),
            pl.BlockSpec((n_seq, HIST, D_B), lambda i: (i, 0, 0)),
            pl.BlockSpec((CONV_WIDTH, D_B), lambda i: (0, 0)),
            pl.BlockSpec((1, D_B), lambda i: (0, 0)),
            pl.BlockSpec((1, D_B), lambda i: (0, 0)),
            pl.BlockSpec((1, D_B), lambda i: (0, 0)),
        ],
        out_specs=[
            pl.BlockSpec((n_seq * seq, D_B), lambda i: (i, 0)),
            pl.BlockSpec((n_seq, HIST, D_B), lambda i: (i, 0, 0)),
        ],
        scratch_shapes=[pltpu.VMEM((n_seq, HIST_PAD + seq, D_B), F32), pltpu.VMEM((n_seq * seq, D_B), F32)],
        compiler_params=_cparams("arbitrary"),
        name="conv_sample",
    )(z, z, hist, conv_w, conv_b.reshape(1, D_B), ln_g.reshape(1, D_B), ln_b.reshape(1, D_B))


def _attn_kernel(q_ref, k_ref, v_ref, o_ref, *, rows):
    scale = 1.0 / math.sqrt(MEM_HEAD_DIM)
    for s in range(k_ref.shape[0]):
        rs = slice(s * rows, (s + 1) * rows)
        q = (q_ref[rs, :] * scale).astype(BF16)
        for h in range(N_MEM_HEADS):
            cols = slice(h * MEM_HEAD_DIM, (h + 1) * MEM_HEAD_DIM)
            kh = k_ref[s, :, cols].astype(BF16)
            vh = v_ref[s, :, cols].astype(BF16)
            sc = lax.dot_general(q[:, cols], kh, (((1,), (1,)), ((), ())), preferred_element_type=F32)
            p = jnp.exp(sc - jnp.max(sc, axis=-1, keepdims=True))
            p = p / jnp.sum(p, axis=-1, keepdims=True)
            o_ref[rs, cols] = jnp.dot(p.astype(BF16), vh, preferred_element_type=F32).astype(BF16)


def _attn(z, k, v, k_col, v_col, *, rows, n_seq, steps_per_seq):
    m = z.shape[0]
    return pl.pallas_call(
        functools.partial(_attn_kernel, rows=rows),
        out_shape=jax.ShapeDtypeStruct((m, D_C), BF16),
        grid=(m // (rows * n_seq),),
        in_specs=[
            pl.BlockSpec((rows * n_seq, COL_W), lambda i: (i, COL_Q)),
            pl.BlockSpec((n_seq, N_MEM, D_C), lambda i: (i // steps_per_seq, 0, k_col)),
            pl.BlockSpec((n_seq, N_MEM, D_C), lambda i: (i // steps_per_seq, 0, v_col)),
        ],
        out_specs=pl.BlockSpec((rows * n_seq, D_C), lambda i: (i, 0)),
        compiler_params=_cparams("arbitrary"),
        name="attn",
    )(z, k, v)


def _merge_kernel(a_ref, b_ref, c_ref, *refs):
    gate_refs = refs[: 2 * N_BRANCH]
    bg_ref, x_ref, wa_ref, wb_ref, wc_ref, wo_ref, mg_ref, x1_ref, h2_ref = refs[2 * N_BRANCH :]
    halves = []
    for half in range(2):
        cols = slice(half * COL_W, (half + 1) * COL_W)
        merged = None
        for br, (in_ref, w_ref) in enumerate(((a_ref, wa_ref), (b_ref, wb_ref), (c_ref, wc_ref))):
            y = jnp.dot(in_ref[...], w_ref[:, cols], preferred_element_type=F32)
            gcols = slice((2 * br + half) * COL_W, (2 * br + half + 1) * COL_W)
            term = jax.nn.sigmoid(gate_refs[2 * br + half][...] + bg_ref[:, gcols]) * y
            merged = term if merged is None else merged + term
        halves.append(merged.astype(BF16))
    merged16 = jnp.concatenate(halves, axis=-1)
    x1 = x_ref[...] + jnp.dot(merged16, wo_ref[...], preferred_element_type=F32)
    x1_ref[...] = x1
    h2_ref[...] = _rms_norm(x1, mg_ref[...]).astype(BF16)


def _merge(a_in, b_in, c_in, z, x, b_gate, w_a, w_b, w_c, w_o, mlp_g, *, tm):
    m = x.shape[0]
    const = lambda i: (0, 0)
    resident = functools.partial(pl.BlockSpec, index_map=const, pipeline_mode=pl.Buffered(1))
    gate_specs = [
        pl.BlockSpec((tm, COL_W), functools.partial(lambda i, col: (i, col), col=COL_GATES + n))
        for n in range(2 * N_BRANCH)
    ]
    return pl.pallas_call(
        _merge_kernel,
        out_shape=[jax.ShapeDtypeStruct((m, D_MODEL), F32), jax.ShapeDtypeStruct((m, D_MODEL), BF16)],
        grid=(m // tm,),
        in_specs=[pl.BlockSpec((tm, D_A), lambda i: (i, 0))] * 3
        + gate_specs
        + [
            pl.BlockSpec((1, N_BRANCH * D_MODEL), const),
            pl.BlockSpec((tm, D_MODEL), lambda i: (i, 0)),
            resident((D_A, D_MODEL)),
            resident((D_B, D_MODEL)),
            resident((D_C, D_MODEL)),
            resident((D_MODEL, D_MODEL)),
            pl.BlockSpec((1, D_MODEL), const),
        ],
        out_specs=[pl.BlockSpec((tm, D_MODEL), lambda i: (i, 0))] * 2,
        compiler_params=_cparams("arbitrary"),
        name="merge",
    )(a_in, b_in, c_in, *([z] * (2 * N_BRANCH)), b_gate.reshape(1, -1), x, w_a, w_b, w_c, w_o,
      mlp_g.reshape(1, D_MODEL))


def _mlp_kernel(h2_ref, x1_ref, wu_ref, wd_ref, fg_ref, y_ref, acc_ref):
    f = pl.program_id(1)
    t = jnp.square(jnp.maximum(jnp.dot(h2_ref[...], wu_ref[...], preferred_element_type=F32), 0.0))
    part = jnp.dot(t.astype(BF16), wd_ref[...], preferred_element_type=F32)

    @pl.when(f == 0)
    def _():
        acc_ref[...] = part

    @pl.when(f > 0)
    def _():
        acc_ref[...] += part

    @pl.when(f == pl.num_programs(1) - 1)
    def _():
        y_ref[...] = _rms_norm(x1_ref[...] + acc_ref[...], fg_ref[...])


def _mlp(h2, x1, w_up, w_down, final_g, *, tm, tf):
    m = h2.shape[0]
    return pl.pallas_call(
        _mlp_kernel,
        out_shape=jax.ShapeDtypeStruct((m, D_MODEL), F32),
        grid=(m // tm, D_FF // tf),
        in_specs=[
            pl.BlockSpec((tm, D_MODEL), lambda i, f: (i, 0)),
            pl.BlockSpec((tm, D_MODEL), lambda i, f: (i, 0)),
            pl.BlockSpec((D_MODEL, tf), lambda i, f: (0, f)),
            pl.BlockSpec((tf, D_MODEL), lambda i, f: (f, 0)),
            pl.BlockSpec((1, D_MODEL), lambda i, f: (0, 0)),
        ],
        out_specs=pl.BlockSpec((tm, D_MODEL), lambda i, f: (i, 0)),
        scratch_shapes=[pltpu.VMEM((tm, D_MODEL), F32)],
        compiler_params=_cparams("arbitrary", "arbitrary"),
        name="mlp",
    )(h2, x1, w_up, w_down, final_g.reshape(1, D_MODEL))


def _sgu_operands(sgu_w, sgu_b, chunk):
    reps = CHUNK // chunk
    w_tiled = jnp.tile(sgu_w[:, :chunk, :chunk], (1, reps, reps))
    bias_rows = jnp.repeat(jnp.tile(sgu_b[:, :chunk].T, (reps, 1)), SGU_HEAD, axis=1)
    return w_tiled, bias_rows


def _layer(x, seq, k, v, k_col, v_col, hist, p, *, want_vn):
    m = x.shape[0]
    batch = m // seq
    chunk = min(seq, CHUNK)
    z = _norm_matmul(x, p["attn_norm_g"], p["w_in"], tm=1024, tn=1024)
    w_tiled, bias_rows = _sgu_operands(p["sgu_w"], p["sgu_b"], chunk)
    sgu_out = _sgu(z, p["sgu_ln_g"], p["sgu_ln_b"], w_tiled, bias_rows, chunk=chunk, rows=512, want_vn=want_vn)
    conv_args = (p["conv_w"], p["conv_b"], p["conv_ln_g"], p["conv_ln_b"])
    if hist is None:
        b_in, new_hist = _conv_prompt(z, batch, seq, *conv_args, t=256)
        c_in = _attn(z, k, v, k_col, v_col, rows=512, n_seq=1, steps_per_seq=seq // 512)
    else:
        b_in, new_hist = _conv_sample(z, hist, seq, *conv_args, n_seq=16)
        c_in = _attn(z, k, v, k_col, v_col, rows=seq, n_seq=4, steps_per_seq=1)
    x1, h2 = _merge(sgu_out[0], b_in, c_in, z, x, p["b_gate"], p["w_a_out"], p["w_b_out"], p["w_c_out"],
                    p["w_o"], p["mlp_norm_g"], tm=256)
    y = _mlp(h2, x1, p["w_up"], p["w_down"], p["final_norm_g"], tm=512, tf=1024)
    return y, new_hist, (sgu_out[1] if want_vn else None)


def kernel(x_prompt, x_sample, mem_prompt, cache_mem_k, cache_mem_v, state_conv, attn_norm_g, w_in, b_gate,
           sgu_ln_g, sgu_ln_b, sgu_w, sgu_b, w_a_out, conv_w, conv_b, conv_ln_g, conv_ln_b, w_b_out, mem_norm_g,
           w_mem_kv, w_c_out, w_o, mlp_norm_g, w_up, w_down, final_norm_g):
    assert attn_norm_g.shape[0] == 1, "single-layer trunk"
    bp, sp, _ = x_prompt.shape
    bs, ss, _ = x_sample.shape
    p = dict(
        attn_norm_g=attn_norm_g[0], w_in=w_in[0].astype(BF16), b_gate=b_gate[0], sgu_ln_g=sgu_ln_g[0],
        sgu_ln_b=sgu_ln_b[0], sgu_w=sgu_w[0], sgu_b=sgu_b[0], w_a_out=w_a_out[0].astype(BF16),
        conv_w=conv_w[0], conv_b=conv_b[0], conv_ln_g=conv_ln_g[0], conv_ln_b=conv_ln_b[0],
        w_b_out=w_b_out[0].astype(BF16), w_c_out=w_c_out[0].astype(BF16), w_o=w_o[0].astype(BF16),
        mlp_norm_g=mlp_norm_g[0], w_up=w_up[0].astype(BF16), w_down=w_down[0].astype(BF16),
        final_norm_g=final_norm_g,
    )
    kv = _norm_matmul(mem_prompt.reshape(bp * N_MEM, D_MODEL), mem_norm_g[0], w_mem_kv[0].astype(BF16),
                      tm=1024, tn=1024).reshape(bp, N_MEM, 2 * D_C)
    y_p, hist_p, _ = _layer(x_prompt.reshape(bp * sp, D_MODEL), sp, kv, kv, 0, 1, None, p, want_vn=False)
    k_s = cache_mem_k[0].reshape(bs, N_MEM, D_C)
    v_s = cache_mem_v[0].reshape(bs, N_MEM, D_C)
    y_s, hist_s, vn_s = _layer(x_sample.reshape(bs * ss, D_MODEL), ss, k_s, v_s, 0, 0, state_conv[0], p,
                               want_vn=True)
    head_shape = (1, bp, N_MEM, N_MEM_HEADS, MEM_HEAD_DIM)
    return (
        y_p.reshape(bp, sp, D_MODEL),
        y_s.reshape(bs, ss, D_MODEL),
        kv[:, :, :D_C].reshape(head_shape),
        kv[:, :, D_C:].reshape(head_shape),
        hist_p[None],
        hist_s[None],
        vn_s.reshape(1, bs, ss, D_A),
    )
```

```python
import functools
import math

import jax
import jax.numpy as jnp
from jax import lax
from jax.experimental import pallas as pl
from jax.experimental.pallas import tpu as pltpu

F32 = jnp.float32
BF16 = jnp.bfloat16

D_MODEL = 2048
D_A = 1024
D_B = 1024
D_C = 1024
CHUNK = 128
SGU_GROUPS = 8
SGU_HEAD = D_A // SGU_GROUPS
CONV_WIDTH = 31
HIST = CONV_WIDTH - 1
N_MEM = 256
N_MEM_HEADS = 4
MEM_HEAD_DIM = D_C // N_MEM_HEADS
N_BRANCH = 3
D_FF = 4 * D_MODEL
N_IN = 2 * D_A + 2 * D_B + D_C + N_BRANCH * D_MODEL
EPS = 1e-6

COL_U, COL_V, COL_GLU_A, COL_GLU_B, COL_Q, COL_GATES = 0, 1, 2, 3, 4, 5
COL_W = 1024

V7X_VMEM_BYTES = 64 * 1024 * 1024
VMEM_LIMIT_BYTES = V7X_VMEM_BYTES - 8 * 1024 * 1024
SUBLANES = 8
HIST_PAD = 32


def _cparams(*semantics):
    return pltpu.CompilerParams(dimension_semantics=semantics, vmem_limit_bytes=VMEM_LIMIT_BYTES)


def _rms_norm(x, g):
    ms = jnp.mean(x * x, axis=-1, keepdims=True)
    return x * lax.rsqrt(ms + EPS) * g


def _layer_norm(x, g, b):
    mu = jnp.mean(x, axis=-1, keepdims=True)
    xc = x - mu
    var = jnp.mean(xc * xc, axis=-1, keepdims=True)
    return xc * lax.rsqrt(var + EPS) * g + b


NORM_ROWS = 128


def _norm_matmul_kernel(x_ref, g_ref, w_ref, o_ref, h_ref):
    @pl.when(pl.program_id(1) == 0)
    def _():
        def body(r, carry):
            rows = pl.ds(pl.multiple_of(r * NORM_ROWS, NORM_ROWS), NORM_ROWS)
            h_ref[rows, :] = _rms_norm(x_ref[rows, :], g_ref[...]).astype(BF16)
            return carry

        lax.fori_loop(0, x_ref.shape[0] // NORM_ROWS, body, 0)

    o_ref[...] = jnp.dot(h_ref[...], w_ref[...], preferred_element_type=F32)


def _norm_matmul(x, g, w, *, tm, tn):
    m, k = x.shape
    n = w.shape[1]
    return pl.pallas_call(
        _norm_matmul_kernel,
        out_shape=jax.ShapeDtypeStruct((m, n), F32),
        grid=(m // tm, n // tn),
        in_specs=[
            pl.BlockSpec((tm, k), lambda i, j: (i, 0)),
            pl.BlockSpec((1, k), lambda i, j: (0, 0)),
            pl.BlockSpec((k, tn), lambda i, j: (0, j)),
        ],
        out_specs=pl.BlockSpec((tm, tn), lambda i, j: (i, j)),
        scratch_shapes=[pltpu.VMEM((tm, k), BF16)],
        compiler_params=_cparams("arbitrary", "arbitrary"),
        name="norm_matmul",
    )(x, g.reshape(1, k), w)


def _sgu_kernel(u_ref, v_ref, g_ref, b_ref, w_ref, bias_ref, a_ref, *vn_refs, chunk):
    r = lax.broadcasted_iota(jnp.int32, (CHUNK, CHUNK), 0)
    c = lax.broadcasted_iota(jnp.int32, (CHUNK, CHUNK), 1)
    seg_bits = chunk.bit_length() - 1
    same_segment = (r >> seg_bits) == (c >> seg_bits)
    mask = jnp.logical_and(same_segment, (r & (chunk - 1)) >= (c & (chunk - 1)))
    w_s = [jnp.where(mask, w_ref[grp], 0.0).astype(BF16) for grp in range(SGU_GROUPS)]
    for blk in range(u_ref.shape[0] // CHUNK):
        rows = slice(blk * CHUNK, (blk + 1) * CHUNK)
        vn = _layer_norm(jax.nn.gelu(v_ref[rows, :]), g_ref[...], b_ref[...])
        if vn_refs:
            vn_refs[0][rows, :] = vn
        vn16 = vn.astype(BF16)
        for grp in range(SGU_GROUPS):
            cols = slice(grp * SGU_HEAD, (grp + 1) * SGU_HEAD)
            mixed = jnp.dot(w_s[grp], vn16[:, cols], preferred_element_type=F32) + bias_ref[:, cols]
            a_ref[rows, cols] = (jax.nn.gelu(u_ref[rows, cols]) * mixed).astype(BF16)


def _sgu(z, ln_g, ln_b, w_tiled, bias_rows, *, chunk, rows, want_vn):
    m = z.shape[0]
    out_shape = [jax.ShapeDtypeStruct((m, D_A), BF16)]
    out_specs = [pl.BlockSpec((rows, D_A), lambda i: (i, 0))]
    if want_vn:
        out_shape.append(jax.ShapeDtypeStruct((m, D_A), F32))
        out_specs.append(pl.BlockSpec((rows, D_A), lambda i: (i, 0)))
    return pl.pallas_call(
        functools.partial(_sgu_kernel, chunk=chunk),
        out_shape=out_shape,
        grid=(m // rows,),
        in_specs=[
            pl.BlockSpec((rows, COL_W), lambda i: (i, COL_U)),
            pl.BlockSpec((rows, COL_W), lambda i: (i, COL_V)),
            pl.BlockSpec((1, D_A), lambda i: (0, 0)),
            pl.BlockSpec((1, D_A), lambda i: (0, 0)),
            pl.BlockSpec((SGU_GROUPS, CHUNK, CHUNK), lambda i: (0, 0, 0)),
            pl.BlockSpec((CHUNK, D_A), lambda i: (0, 0)),
        ],
        out_specs=out_specs,
        compiler_params=_cparams("arbitrary"),
        name="sgu",
    )(z, z, ln_g.reshape(1, D_A), ln_b.reshape(1, D_A), w_tiled, bias_rows)


CONV_ROWS = 32
CONV_LANES = 256
LN_ROWS = 64
CONV_LEAD = HIST_PAD - HIST


def _conv_window(win, w_ref, lanes, n_rows):
    out = None
    for b in range(SUBLANES):
        rows_b = n_rows if b == 0 else n_rows + SUBLANES
        y = None
        for a in range(HIST_PAD // SUBLANES + 1):
            k = SUBLANES * a + b - CONV_LEAD
            if 0 <= k < CONV_WIDTH:
                term = w_ref[k : k + 1, lanes] * win[SUBLANES * a : SUBLANES * a + rows_b, :]
                y = term if y is None else y + term
        shifted = y[b : b + n_rows, :]
        out = shifted if out is None else out + shifted
    return out


def _ln_silu_rows(dc_ref, cb_ref, g_ref, b_ref, o_ref, rows):
    dc = dc_ref[rows, :] + cb_ref[...]
    o_ref[rows, :] = jax.nn.silu(_layer_norm(dc, g_ref[...], b_ref[...])).astype(BF16)


def _conv_prompt_kernel(ga_ref, gb_ref, w_ref, cb_ref, g_ref, b_ref, o_ref, hist_ref, ext_ref, dc_ref):
    t = ga_ref.shape[0]
    i = pl.program_id(1)

    @pl.when(i == 0)
    def _():
        ext_ref[0:HIST_PAD, :] = jnp.zeros((HIST_PAD, D_B), F32)

    @pl.when(i > 0)
    def _():
        ext_ref[0:HIST_PAD, :] = ext_ref[t : t + HIST_PAD, :]

    ext_ref[HIST_PAD : HIST_PAD + t, :] = ga_ref[...] * jax.nn.sigmoid(gb_ref[...])

    def conv_body(r, carry):
        r0 = pl.multiple_of(r * CONV_ROWS, CONV_ROWS)
        for lb in range(D_B // CONV_LANES):
            lanes = slice(lb * CONV_LANES, (lb + 1) * CONV_LANES)
            win = ext_ref[pl.ds(r0, CONV_ROWS + HIST_PAD), lanes]
            dc_ref[pl.ds(r0, CONV_ROWS), lanes] = _conv_window(win, w_ref, lanes, CONV_ROWS)
        return carry

    lax.fori_loop(0, t // CONV_ROWS, conv_body, 0)

    def ln_body(r, carry):
        rows = pl.ds(pl.multiple_of(r * LN_ROWS, LN_ROWS), LN_ROWS)
        _ln_silu_rows(dc_ref, cb_ref, g_ref, b_ref, o_ref, rows)
        return carry

    lax.fori_loop(0, t // LN_ROWS, ln_body, 0)
    hist_ref[0] = ext_ref[t + CONV_LEAD : t + HIST_PAD, :]


def _conv_prompt(z, batch, seq, conv_w, conv_b, ln_g, ln_b, *, t):
    steps = seq // t
    return pl.pallas_call(
        _conv_prompt_kernel,
        out_shape=[
            jax.ShapeDtypeStruct((batch * seq, D_B), BF16),
            jax.ShapeDtypeStruct((batch, HIST, D_B), F32),
        ],
        grid=(batch, steps),
        in_specs=[
            pl.BlockSpec((t, COL_W), lambda b, i: (b * steps + i, COL_GLU_A)),
            pl.BlockSpec((t, COL_W), lambda b, i: (b * steps + i, COL_GLU_B)),
            pl.BlockSpec((CONV_WIDTH, D_B), lambda b, i: (0, 0)),
            pl.BlockSpec((1, D_B), lambda b, i: (0, 0)),
            pl.BlockSpec((1, D_B), lambda b, i: (0, 0)),
            pl.BlockSpec((1, D_B), lambda b, i: (0, 0)),
        ],
        out_specs=[
            pl.BlockSpec((t, D_B), lambda b, i: (b * steps + i, 0)),
            pl.BlockSpec((1, HIST, D_B), lambda b, i: (b, 0, 0)),
        ],
        scratch_shapes=[pltpu.VMEM((t + HIST_PAD, D_B), F32), pltpu.VMEM((t, D_B), F32)],
        compiler_params=_cparams("arbitrary", "arbitrary"),
        name="conv_prompt",
    )(z, z, conv_w, conv_b.reshape(1, D_B), ln_g.reshape(1, D_B), ln_b.reshape(1, D_B))


def _conv_sample_kernel(ga_ref, gb_ref, hist_ref, w_ref, cb_ref, g_ref, b_ref, o_ref, nh_ref, ext_ref, dc_ref,
                        *, seq):
    n_seq = hist_ref.shape[0]
    c = ga_ref[...] * jax.nn.sigmoid(gb_ref[...])
    for s in range(n_seq):
        ext_ref[s, 0:SUBLANES, :] = jnp.zeros((SUBLANES, D_B), F32)
        ext_ref[s, CONV_LEAD:HIST_PAD, :] = hist_ref[s]
        ext_ref[s, HIST_PAD : HIST_PAD + seq, :] = c[s * seq : (s + 1) * seq, :]
    for s in range(n_seq):
        for lb in range(D_B // CONV_LANES):
            lanes = slice(lb * CONV_LANES, (lb + 1) * CONV_LANES)
            dc_ref[s * seq : (s + 1) * seq, lanes] = _conv_window(ext_ref[s, :, lanes], w_ref, lanes, seq)
        nh_ref[s] = ext_ref[s, seq + CONV_LEAD : seq + HIST_PAD, :]
    _ln_silu_rows(dc_ref, cb_ref, g_ref, b_ref, o_ref, slice(None))


def _conv_sample(z, hist, seq, conv_w, conv_b, ln_g, ln_b, *, n_seq):
    batch = hist.shape[0]
    return pl.pallas_call(
        functools.partial(_conv_sample_kernel, seq=seq),
        out_shape=[
            jax.ShapeDtypeStruct((batch * seq, D_B), BF16),
            jax.ShapeDtypeStruct((batch, HIST, D_B), F32),
        ],
        grid=(batch // n_seq,),
        in_specs=[
            pl.BlockSpec((n_seq * seq, COL_W), lambda i: (i, COL_GLU_A)),
            pl.BlockSpec((n_seq * seq, COL_W), lambda i: (i, COL_GLU_B)),
            pl.BlockSpec((n_seq, HIST, D_B), lambda i: (i, 0, 0)),
            pl.BlockSpec((CONV_WIDTH, D_B), lambda i: (0, 0)),
            pl.BlockSpec((1, D_B), lambda i: (0, 0)),
            pl.BlockSpec((1, D_B), lambda i: (0, 0)),
            pl.BlockSpec((1, D_B), lambda i: (0, 0)),
        ],
        out_specs=[
            pl.BlockSpec((n_seq * seq, D_B), lambda i: (i, 0)),
            pl.BlockSpec((n_seq, HIST, D_B), lambda i: (i, 0, 0)),
        ],
        scratch_shapes=[pltpu.VMEM((n_seq, HIST_PAD + seq, D_B), F32), pltpu.VMEM((n_seq * seq, D_B), F32)],
        compiler_params=_cparams("arbitrary"),
        name="conv_sample",
    )(z, z, hist, conv_w, conv_b.reshape(1, D_B), ln_g.reshape(1, D_B), ln_b.reshape(1, D_B))


def _head_cols(h):
    return slice(h * MEM_HEAD_DIM, (h + 1) * MEM_HEAD_DIM)


LANES = 128
HEAD_LANE_TILES = MEM_HEAD_DIM // LANES
HEAD_ROW_PITCH = N_MEM_HEADS * HEAD_LANE_TILES


def _split_lane_tiles(kv):
    b = kv.shape[0]
    kv = kv.reshape(b, N_MEM, N_MEM_HEADS, HEAD_LANE_TILES, LANES)
    return kv.transpose(0, 1, 3, 2, 4).reshape(b, N_MEM * HEAD_ROW_PITCH, LANES)


def _head_of(kv_ref, s, h):
    if kv_ref.shape[-1] == LANES:
        tiles = [kv_ref[s, pl.ds(j * N_MEM_HEADS + h, N_MEM, stride=HEAD_ROW_PITCH), :]
                 for j in range(HEAD_LANE_TILES)]
        return jnp.concatenate(tiles, axis=-1).astype(BF16)
    return kv_ref[s, :, _head_cols(h)].astype(BF16)


def _attn_kernel(q_ref, k_ref, v_ref, o_ref, *, rows):
    scale = 1.0 / math.sqrt(MEM_HEAD_DIM)
    pairs = [(s, h) for s in range(k_ref.shape[0]) for h in range(N_MEM_HEADS)]
    scores = []
    for s, h in pairs:
        q = (q_ref[s * rows : (s + 1) * rows, _head_cols(h)] * scale).astype(BF16)
        scores.append(lax.dot_general(q, _head_of(k_ref, s, h), (((1,), (1,)), ((), ())),
                                      preferred_element_type=F32))
    sc = jnp.concatenate(scores, axis=0)
    p = jnp.exp(sc - jnp.max(sc, axis=-1, keepdims=True))
    p = p / jnp.sum(p, axis=-1, keepdims=True)
    for n, (s, h) in enumerate(pairs):
        ph = p[n * rows : (n + 1) * rows, :].astype(BF16)
        o = jnp.dot(ph, _head_of(v_ref, s, h), preferred_element_type=F32)
        o_ref[s * rows : (s + 1) * rows, _head_cols(h)] = o.astype(BF16)


def _attn(z, k, v, kv_specs, *, rows, n_seq):
    m = z.shape[0]
    return pl.pallas_call(
        functools.partial(_attn_kernel, rows=rows),
        out_shape=jax.ShapeDtypeStruct((m, D_C), BF16),
        grid=(m // (rows * n_seq),),
        in_specs=[pl.BlockSpec((rows * n_seq, COL_W), lambda i: (i, COL_Q))] + kv_specs,
        out_specs=pl.BlockSpec((rows * n_seq, D_C), lambda i: (i, 0)),
        compiler_params=_cparams("arbitrary"),
        name="attn",
    )(z, k, v)


def _merge_kernel(a_ref, b_ref, c_ref, *refs):
    gate_refs = refs[: 2 * N_BRANCH]
    bg_ref, x_ref, wa_ref, wb_ref, wc_ref, wo_ref, mg_ref, x1_ref, h2_ref = refs[2 * N_BRANCH :]
    halves = []
    for half in range(2):
        cols = slice(half * COL_W, (half + 1) * COL_W)
        merged = None
        for br, (in_ref, w_ref) in enumerate(((a_ref, wa_ref), (b_ref, wb_ref), (c_ref, wc_ref))):
            y = jnp.dot(in_ref[...], w_ref[:, cols], preferred_element_type=F32)
            gcols = slice((2 * br + half) * COL_W, (2 * br + half + 1) * COL_W)
            term = jax.nn.sigmoid(gate_refs[2 * br + half][...] + bg_ref[:, gcols]) * y
            merged = term if merged is None else merged + term
        halves.append(merged.astype(BF16))
    merged16 = jnp.concatenate(halves, axis=-1)
    x1 = x_ref[...] + jnp.dot(merged16, wo_ref[...], preferred_element_type=F32)
    x1_ref[...] = x1
    h2_ref[...] = _rms_norm(x1, mg_ref[...]).astype(BF16)


def _merge(a_in, b_in, c_in, z, x, b_gate, w_a, w_b, w_c, w_o, mlp_g, *, tm):
    m = x.shape[0]
    const = lambda i: (0, 0)
    resident = functools.partial(pl.BlockSpec, index_map=const, pipeline_mode=pl.Buffered(1))
    gate_specs = [
        pl.BlockSpec((tm, COL_W), functools.partial(lambda i, col: (i, col), col=COL_GATES + n))
        for n in range(2 * N_BRANCH)
    ]
    return pl.pallas_call(
        _merge_kernel,
        out_shape=[jax.ShapeDtypeStruct((m, D_MODEL), F32), jax.ShapeDtypeStruct((m, D_MODEL), BF16)],
        grid=(m // tm,),
        in_specs=[pl.BlockSpec((tm, D_A), lambda i: (i, 0))] * 3
        + gate_specs
        + [
            pl.BlockSpec((1, N_BRANCH * D_MODEL), const),
            pl.BlockSpec((tm, D_MODEL), lambda i: (i, 0)),
            resident((D_A, D_MODEL)),
            resident((D_B, D_MODEL)),
            resident((D_C, D_MODEL)),
            resident((D_MODEL, D_MODEL)),
            pl.BlockSpec((1, D_MODEL), const),
        ],
        out_specs=[pl.BlockSpec((tm, D_MODEL), lambda i: (i, 0))] * 2,
        compiler_params=_cparams("arbitrary"),
        name="merge",
    )(a_in, b_in, c_in, *([z] * (2 * N_BRANCH)), b_gate.reshape(1, -1), x, w_a, w_b, w_c, w_o,
      mlp_g.reshape(1, D_MODEL))


def _mlp_kernel(h2_ref, x1_ref, wu_ref, wd_ref, fg_ref, y_ref, acc_ref):
    f = pl.program_id(1)
    t = jnp.square(jnp.maximum(jnp.dot(h2_ref[...], wu_ref[...], preferred_element_type=F32), 0.0))
    part = jnp.dot(t.astype(BF16), wd_ref[...], preferred_element_type=F32)

    @pl.when(f == 0)
    def _():
        acc_ref[...] = part

    @pl.when(f > 0)
    def _():
        acc_ref[...] += part

    @pl.when(f == pl.num_programs(1) - 1)
    def _():
        y_ref[...] = _rms_norm(x1_ref[...] + acc_ref[...], fg_ref[...])


def _mlp(h2, x1, w_up, w_down, final_g, *, tm, tf):
    m = h2.shape[0]
    return pl.pallas_call(
        _mlp_kernel,
        out_shape=jax.ShapeDtypeStruct((m, D_MODEL), F32),
        grid=(m // tm, D_FF // tf),
        in_specs=[
            pl.BlockSpec((tm, D_MODEL), lambda i, f: (i, 0)),
            pl.BlockSpec((tm, D_MODEL), lambda i, f: (i, 0)),
            pl.BlockSpec((D_MODEL, tf), lambda i, f: (0, f)),
            pl.BlockSpec((tf, D_MODEL), lambda i, f: (f, 0)),
            pl.BlockSpec((1, D_MODEL), lambda i, f: (0, 0)),
        ],
        out_specs=pl.BlockSpec((tm, D_MODEL), lambda i, f: (i, 0)),
        scratch_shapes=[pltpu.VMEM((tm, D_MODEL), F32)],
        compiler_params=_cparams("arbitrary", "arbitrary"),
        name="mlp",
    )(h2, x1, w_up, w_down, final_g.reshape(1, D_MODEL))


def _sgu_operands(sgu_w, sgu_b, chunk):
    reps = CHUNK // chunk
    w_tiled = jnp.tile(sgu_w[:, :chunk, :chunk], (1, reps, reps))
    bias_rows = jnp.repeat(jnp.tile(sgu_b[:, :chunk].T, (reps, 1)), SGU_HEAD, axis=1)
    return w_tiled, bias_rows


PROMPT_ATTN_ROWS = 512
SAMPLE_ATTN_SEQS = 8


def _layer(x, seq, k, v, hist, p, *, want_vn):
    m = x.shape[0]
    batch = m // seq
    chunk = min(seq, CHUNK)
    z = _norm_matmul(x, p["attn_norm_g"], p["w_in"], tm=1024, tn=1024)
    w_tiled, bias_rows = _sgu_operands(p["sgu_w"], p["sgu_b"], chunk)
    sgu_out = _sgu(z, p["sgu_ln_g"], p["sgu_ln_b"], w_tiled, bias_rows, chunk=chunk, rows=512, want_vn=want_vn)
    conv_args = (p["conv_w"], p["conv_b"], p["conv_ln_g"], p["conv_ln_b"])
    if hist is None:
        b_in, new_hist = _conv_prompt(z, batch, seq, *conv_args, t=256)
        steps = seq // PROMPT_ATTN_ROWS
        kv_specs = [pl.BlockSpec((1, N_MEM, D_C), functools.partial(lambda i, col: (i // steps, 0, col), col=col))
                    for col in (0, 1)]
        c_in = _attn(z, k, v, kv_specs, rows=PROMPT_ATTN_ROWS, n_seq=1)
    else:
        b_in, new_hist = _conv_sample(z, hist, seq, *conv_args, n_seq=16)
        kv_specs = [pl.BlockSpec((SAMPLE_ATTN_SEQS, N_MEM * HEAD_ROW_PITCH, LANES), lambda i: (i, 0, 0))] * 2
        c_in = _attn(z, k, v, kv_specs, rows=seq, n_seq=SAMPLE_ATTN_SEQS)
    x1, h2 = _merge(sgu_out[0], b_in, c_in, z, x, p["b_gate"], p["w_a_out"], p["w_b_out"], p["w_c_out"],
                    p["w_o"], p["mlp_norm_g"], tm=256)
    y = _mlp(h2, x1, p["w_up"], p["w_down"], p["final_norm_g"], tm=512, tf=1024)
    return y, new_hist, (sgu_out[1] if want_vn else None)


def kernel(x_prompt, x_sample, mem_prompt, cache_mem_k, cache_mem_v, state_conv, attn_norm_g, w_in, b_gate,
           sgu_ln_g, sgu_ln_b, sgu_w, sgu_b, w_a_out, conv_w, conv_b, conv_ln_g, conv_ln_b, w_b_out, mem_norm_g,
           w_mem_kv, w_c_out, w_o, mlp_norm_g, w_up, w_down, final_norm_g):
    assert attn_norm_g.shape[0] == 1, "single-layer trunk"
    bp, sp, _ = x_prompt.shape
    bs, ss, _ = x_sample.shape
    p = dict(
        attn_norm_g=attn_norm_g[0], w_in=w_in[0].astype(BF16), b_gate=b_gate[0], sgu_ln_g=sgu_ln_g[0],
        sgu_ln_b=sgu_ln_b[0], sgu_w=sgu_w[0], sgu_b=sgu_b[0], w_a_out=w_a_out[0].astype(BF16),
        conv_w=conv_w[0], conv_b=conv_b[0], conv_ln_g=conv_ln_g[0], conv_ln_b=conv_ln_b[0],
        w_b_out=w_b_out[0].astype(BF16), w_c_out=w_c_out[0].astype(BF16), w_o=w_o[0].astype(BF16),
        mlp_norm_g=mlp_norm_g[0], w_up=w_up[0].astype(BF16), w_down=w_down[0].astype(BF16),
        final_norm_g=final_norm_g,
    )
    kv = _norm_matmul(mem_prompt.reshape(bp * N_MEM, D_MODEL), mem_norm_g[0], w_mem_kv[0].astype(BF16),
                      tm=1024, tn=1024).reshape(bp, N_MEM, 2 * D_C)
    y_p, hist_p, _ = _layer(x_prompt.reshape(bp * sp, D_MODEL), sp, kv, kv, None, p, want_vn=False)
    y_s, hist_s, vn_s = _layer(x_sample.reshape(bs * ss, D_MODEL), ss, _split_lane_tiles(cache_mem_k[0]),
                               _split_lane_tiles(cache_mem_v[0]), state_conv[0], p, want_vn=True)
    head_shape = (1, bp, N_MEM, N_MEM_HEADS, MEM_HEAD_DIM)
    return (
        y_p.reshape(bp, sp, D_MODEL),
        y_s.reshape(bs, ss, D_MODEL),
        kv[:, :, :D_C].reshape(head_shape),
        kv[:, :, D_C:].reshape(head_shape),
        hist_p[None],
        hist_s[None],
        vn_s.reshape(1, bs, ss, D_A),
    )
```

```python
import functools
import math

import jax
import jax.numpy as jnp
from jax import lax
from jax.experimental import pallas as pl
from jax.experimental.pallas import tpu as pltpu

F32 = jnp.float32
BF16 = jnp.bfloat16

D_MODEL = 2048
D_A = 1024
D_B = 1024
D_C = 1024
CHUNK = 128
SGU_GROUPS = 8
SGU_HEAD = D_A // SGU_GROUPS
CONV_WIDTH = 31
HIST = CONV_WIDTH - 1
N_MEM = 256
N_MEM_HEADS = 4
MEM_HEAD_DIM = D_C // N_MEM_HEADS
N_BRANCH = 3
D_FF = 4 * D_MODEL
N_IN = 2 * D_A + 2 * D_B + D_C + N_BRANCH * D_MODEL
EPS = 1e-6

COL_U, COL_V, COL_GLU_A, COL_GLU_B, COL_Q, COL_GATES = 0, 1, 2, 3, 4, 5
COL_W = 1024

V7X_VMEM_BYTES = 64 * 1024 * 1024
VMEM_LIMIT_BYTES = V7X_VMEM_BYTES - 8 * 1024 * 1024
SUBLANES = 8
HIST_PAD = 32


def _cparams(*semantics):
    return pltpu.CompilerParams(dimension_semantics=semantics, vmem_limit_bytes=VMEM_LIMIT_BYTES)


def _rms_norm(x, g):
    ms = jnp.mean(x * x, axis=-1, keepdims=True)
    return x * lax.rsqrt(ms + EPS) * g


def _layer_norm(x, g, b):
    mu = jnp.mean(x, axis=-1, keepdims=True)
    xc = x - mu
    var = jnp.mean(xc * xc, axis=-1, keepdims=True)
    return xc * lax.rsqrt(var + EPS) * g + b


NORM_ROWS = 128


def _norm_matmul_kernel(x_ref, g_ref, w_ref, o_ref, *rest):
    *w16_refs, h_ref = rest

    @pl.when(pl.program_id(1) == 0)
    def _():
        def body(r, carry):
            rows = pl.ds(pl.multiple_of(r * NORM_ROWS, NORM_ROWS), NORM_ROWS)
            h_ref[rows, :] = _rms_norm(x_ref[rows, :], g_ref[...]).astype(BF16)
            return carry

        lax.fori_loop(0, x_ref.shape[0] // NORM_ROWS, body, 0)

    w16 = w_ref[...].astype(BF16)
    if w16_refs:
        w16_refs[0][...] = w16
    o_ref[...] = jnp.dot(h_ref[...], w16, preferred_element_type=F32)


def _norm_matmul(x, g, w, *, tm, tn, emit_w16=False):
    m, k = x.shape
    n = w.shape[1]
    out_shape = [jax.ShapeDtypeStruct((m, n), F32)]
    out_specs = [pl.BlockSpec((tm, tn), lambda i, j: (i, j))]
    if emit_w16:
        assert m == tm, "every weight block must be visited exactly once"
        out_shape.append(jax.ShapeDtypeStruct((k, n), BF16))
        out_specs.append(pl.BlockSpec((k, tn), lambda i, j: (0, j)))
    outs = pl.pallas_call(
        _norm_matmul_kernel,
        out_shape=out_shape,
        grid=(m // tm, n // tn),
        in_specs=[
            pl.BlockSpec((tm, k), lambda i, j: (i, 0)),
            pl.BlockSpec((1, k), lambda i, j: (0, 0)),
            pl.BlockSpec((k, tn), lambda i, j: (0, j)),
        ],
        out_specs=out_specs,
        scratch_shapes=[pltpu.VMEM((tm, k), BF16)],
        compiler_params=_cparams("arbitrary", "arbitrary"),
        name="norm_matmul",
    )(x, g.reshape(1, k), w)
    return outs if emit_w16 else outs[0]


def _sgu_kernel(u_ref, v_ref, g_ref, b_ref, w_ref, bias_ref, a_ref, *vn_refs, chunk):
    r = lax.broadcasted_iota(jnp.int32, (CHUNK, CHUNK), 0)
    c = lax.broadcasted_iota(jnp.int32, (CHUNK, CHUNK), 1)
    seg_bits = chunk.bit_length() - 1
    same_segment = (r >> seg_bits) == (c >> seg_bits)
    mask = jnp.logical_and(same_segment, (r & (chunk - 1)) >= (c & (chunk - 1)))
    w_s = [jnp.where(mask, w_ref[grp], 0.0).astype(BF16) for grp in range(SGU_GROUPS)]
    for blk in range(u_ref.shape[0] // CHUNK):
        rows = slice(blk * CHUNK, (blk + 1) * CHUNK)
        vn = _layer_norm(jax.nn.gelu(v_ref[rows, :]), g_ref[...], b_ref[...])
        if vn_refs:
            vn_refs[0][rows, :] = vn
        vn16 = vn.astype(BF16)
        for grp in range(SGU_GROUPS):
            cols = slice(grp * SGU_HEAD, (grp + 1) * SGU_HEAD)
            mixed = jnp.dot(w_s[grp], vn16[:, cols], preferred_element_type=F32) + bias_ref[:, cols]
            a_ref[rows, cols] = (jax.nn.gelu(u_ref[rows, cols]) * mixed).astype(BF16)


def _sgu(z, ln_g, ln_b, w_tiled, bias_rows, *, chunk, rows, want_vn):
    m = z.shape[0]
    out_shape = [jax.ShapeDtypeStruct((m, D_A), BF16)]
    out_specs = [pl.BlockSpec((rows, D_A), lambda i: (i, 0))]
    if want_vn:
        out_shape.append(jax.ShapeDtypeStruct((m, D_A), F32))
        out_specs.append(pl.BlockSpec((rows, D_A), lambda i: (i, 0)))
    return pl.pallas_call(
        functools.partial(_sgu_kernel, chunk=chunk),
        out_shape=out_shape,
        grid=(m // rows,),
        in_specs=[
            pl.BlockSpec((rows, COL_W), lambda i: (i, COL_U)),
            pl.BlockSpec((rows, COL_W), lambda i: (i, COL_V)),
            pl.BlockSpec((1, D_A), lambda i: (0, 0)),
            pl.BlockSpec((1, D_A), lambda i: (0, 0)),
            pl.BlockSpec((SGU_GROUPS, CHUNK, CHUNK), lambda i: (0, 0, 0)),
            pl.BlockSpec((CHUNK, D_A), lambda i: (0, 0)),
        ],
        out_specs=out_specs,
        compiler_params=_cparams("arbitrary"),
        name="sgu",
    )(z, z, ln_g.reshape(1, D_A), ln_b.reshape(1, D_A), w_tiled, bias_rows)


CONV_ROWS = 32
CONV_LANES = 256
LN_ROWS = 64
CONV_LEAD = HIST_PAD - HIST


def _conv_window(win, w_ref, lanes, n_rows):
    out = None
    for b in range(SUBLANES):
        rows_b = n_rows if b == 0 else n_rows + SUBLANES
        y = None
        for a in range(HIST_PAD // SUBLANES + 1):
            k = SUBLANES * a + b - CONV_LEAD
            if 0 <= k < CONV_WIDTH:
                term = w_ref[k : k + 1, lanes] * win[SUBLANES * a : SUBLANES * a + rows_b, :]
                y = term if y is None else y + term
        shifted = y[b : b + n_rows, :]
        out = shifted if out is None else out + shifted
    return out


def _ln_silu_rows(dc_ref, cb_ref, g_ref, b_ref, o_ref, rows):
    dc = dc_ref[rows, :] + cb_ref[...]
    o_ref[rows, :] = jax.nn.silu(_layer_norm(dc, g_ref[...], b_ref[...])).astype(BF16)


def _conv_prompt_kernel(ga_ref, gb_ref, w_ref, cb_ref, g_ref, b_ref, o_ref, hist_ref, ext_ref, dc_ref):
    t = ga_ref.shape[0]
    i = pl.program_id(1)

    @pl.when(i == 0)
    def _():
        ext_ref[0:HIST_PAD, :] = jnp.zeros((HIST_PAD, D_B), F32)

    @pl.when(i > 0)
    def _():
        ext_ref[0:HIST_PAD, :] = ext_ref[t : t + HIST_PAD, :]

    ext_ref[HIST_PAD : HIST_PAD + t, :] = ga_ref[...] * jax.nn.sigmoid(gb_ref[...])

    def conv_body(r, carry):
        r0 = pl.multiple_of(r * CONV_ROWS, CONV_ROWS)
        for lb in range(D_B // CONV_LANES):
            lanes = slice(lb * CONV_LANES, (lb + 1) * CONV_LANES)
            win = ext_ref[pl.ds(r0, CONV_ROWS + HIST_PAD), lanes]
            dc_ref[pl.ds(r0, CONV_ROWS), lanes] = _conv_window(win, w_ref, lanes, CONV_ROWS)
        return carry

    lax.fori_loop(0, t // CONV_ROWS, conv_body, 0)

    def ln_body(r, carry):
        rows = pl.ds(pl.multiple_of(r * LN_ROWS, LN_ROWS), LN_ROWS)
        _ln_silu_rows(dc_ref, cb_ref, g_ref, b_ref, o_ref, rows)
        return carry

    lax.fori_loop(0, t // LN_ROWS, ln_body, 0)
    hist_ref[0] = ext_ref[t + CONV_LEAD : t + HIST_PAD, :]


def _conv_prompt(z, batch, seq, conv_w, conv_b, ln_g, ln_b, *, t):
    steps = seq // t
    return pl.pallas_call(
        _conv_prompt_kernel,
        out_shape=[
            jax.ShapeDtypeStruct((batch * seq, D_B), BF16),
            jax.ShapeDtypeStruct((batch, HIST, D_B), F32),
        ],
        grid=(batch, steps),
        in_specs=[
            pl.BlockSpec((t, COL_W), lambda b, i: (b * steps + i, COL_GLU_A)),
            pl.BlockSpec((t, COL_W), lambda b, i: (b * steps + i, COL_GLU_B)),
            pl.BlockSpec((CONV_WIDTH, D_B), lambda b, i: (0, 0)),
            pl.BlockSpec((1, D_B), lambda b, i: (0, 0)),
            pl.BlockSpec((1, D_B), lambda b, i: (0, 0)),
            pl.BlockSpec((1, D_B), lambda b, i: (0, 0)),
        ],
        out_specs=[
            pl.BlockSpec((t, D_B), lambda b, i: (b * steps + i, 0)),
            pl.BlockSpec((1, HIST, D_B), lambda b, i: (b, 0, 0)),
        ],
        scratch_shapes=[pltpu.VMEM((t + HIST_PAD, D_B), F32), pltpu.VMEM((t, D_B), F32)],
        compiler_params=_cparams("arbitrary", "arbitrary"),
        name="conv_prompt",
    )(z, z, conv_w, conv_b.reshape(1, D_B), ln_g.reshape(1, D_B), ln_b.reshape(1, D_B))


def _conv_sample_kernel(ga_ref, gb_ref, hist_ref, w_ref, cb_ref, g_ref, b_ref, o_ref, nh_ref, ext_ref, dc_ref,
                        *, seq):
    n_seq = hist_ref.shape[0]
    c = ga_ref[...] * jax.nn.sigmoid(gb_ref[...])
    for s in range(n_seq):
        ext_ref[s, 0:SUBLANES, :] = jnp.zeros((SUBLANES, D_B), F32)
        ext_ref[s, CONV_LEAD:HIST_PAD, :] = hist_ref[s]
        ext_ref[s, HIST_PAD : HIST_PAD + seq, :] = c[s * seq : (s + 1) * seq, :]
    for s in range(n_seq):
        for lb in range(D_B // CONV_LANES):
            lanes = slice(lb * CONV_LANES, (lb + 1) * CONV_LANES)
            dc_ref[s * seq : (s + 1) * seq, lanes] = _conv_window(ext_ref[s, :, lanes], w_ref, lanes, seq)
        nh_ref[s] = ext_ref[s, seq + CONV_LEAD : seq + HIST_PAD, :]
    _ln_silu_rows(dc_ref, cb_ref, g_ref, b_ref, o_ref, slice(None))


def _conv_sample(z, hist, seq, conv_w, conv_b, ln_g, ln_b, *, n_seq):
    batch = hist.shape[0]
    return pl.pallas_call(
        functools.partial(_conv_sample_kernel, seq=seq),
        out_shape=[
            jax.ShapeDtypeStruct((batch * seq, D_B), BF16),
            jax.ShapeDtypeStruct((batch, HIST, D_B), F32),
        ],
        grid=(batch // n_seq,),
        in_specs=[
            pl.BlockSpec((n_seq * seq, COL_W), lambda i: (i, COL_GLU_A)),
            pl.BlockSpec((n_seq * seq, COL_W), lambda i: (i, COL_GLU_B)),
            pl.BlockSpec((n_seq, HIST, D_B), lambda i: (i, 0, 0)),
            pl.BlockSpec((CONV_WIDTH, D_B), lambda i: (0, 0)),
            pl.BlockSpec((1, D_B), lambda i: (0, 0)),
            pl.BlockSpec((1, D_B), lambda i: (0, 0)),
            pl.BlockSpec((1, D_B), lambda i: (0, 0)),
        ],
        out_specs=[
            pl.BlockSpec((n_seq * seq, D_B), lambda i: (i, 0)),
            pl.BlockSpec((n_seq, HIST, D_B), lambda i: (i, 0, 0)),
        ],
        scratch_shapes=[pltpu.VMEM((n_seq, HIST_PAD + seq, D_B), F32), pltpu.VMEM((n_seq * seq, D_B), F32)],
        compiler_params=_cparams("arbitrary"),
        name="conv_sample",
    )(z, z, hist, conv_w, conv_b.reshape(1, D_B), ln_g.reshape(1, D_B), ln_b.reshape(1, D_B))


def _head_cols(h):
    return slice(h * MEM_HEAD_DIM, (h + 1) * MEM_HEAD_DIM)


LANES = 128
HEAD_LANE_TILES = MEM_HEAD_DIM // LANES
HEAD_ROW_PITCH = N_MEM_HEADS * HEAD_LANE_TILES


def _split_lane_tiles(kv):
    b = kv.shape[0]
    kv = kv.reshape(b, N_MEM, N_MEM_HEADS, HEAD_LANE_TILES, LANES)
    return kv.transpose(0, 1, 3, 2, 4).reshape(b, N_MEM * HEAD_ROW_PITCH, LANES)


def _head_of(kv_ref, s, h):
    if kv_ref.shape[-1] == LANES:
        tiles = [kv_ref[s, pl.ds(j * N_MEM_HEADS + h, N_MEM, stride=HEAD_ROW_PITCH), :]
                 for j in range(HEAD_LANE_TILES)]
        return jnp.concatenate(tiles, axis=-1).astype(BF16)
    return kv_ref[s, :, _head_cols(h)].astype(BF16)


def _attn_kernel(q_ref, k_ref, v_ref, o_ref, *, rows):
    scale = 1.0 / math.sqrt(MEM_HEAD_DIM)
    pairs = [(s, h) for s in range(k_ref.shape[0]) for h in range(N_MEM_HEADS)]
    scores = []
    for s, h in pairs:
        q = (q_ref[s * rows : (s + 1) * rows, _head_cols(h)] * scale).astype(BF16)
        scores.append(lax.dot_general(q, _head_of(k_ref, s, h), (((1,), (1,)), ((), ())),
                                      preferred_element_type=F32))
    sc = jnp.concatenate(scores, axis=0)
    p = jnp.exp(sc - jnp.max(sc, axis=-1, keepdims=True))
    p = p / jnp.sum(p, axis=-1, keepdims=True)
    for n, (s, h) in enumerate(pairs):
        ph = p[n * rows : (n + 1) * rows, :].astype(BF16)
        o = jnp.dot(ph, _head_of(v_ref, s, h), preferred_element_type=F32)
        o_ref[s * rows : (s + 1) * rows, _head_cols(h)] = o.astype(BF16)


def _attn(z, k, v, kv_specs, *, rows, n_seq):
    m = z.shape[0]
    return pl.pallas_call(
        functools.partial(_attn_kernel, rows=rows),
        out_shape=jax.ShapeDtypeStruct((m, D_C), BF16),
        grid=(m // (rows * n_seq),),
        in_specs=[pl.BlockSpec((rows * n_seq, COL_W), lambda i: (i, COL_Q))] + kv_specs,
        out_specs=pl.BlockSpec((rows * n_seq, D_C), lambda i: (i, 0)),
        compiler_params=_cparams("arbitrary"),
        name="attn",
    )(z, k, v)


def _merge_kernel(a_ref, b_ref, c_ref, *refs):
    gate_refs = refs[: 2 * N_BRANCH]
    bg_ref, x_ref, wa_ref, wb_ref, wc_ref, wo_ref, mg_ref, x1_ref, h2_ref = refs[2 * N_BRANCH :]
    halves = []
    for half in range(2):
        cols = slice(half * COL_W, (half + 1) * COL_W)
        merged = None
        for br, (in_ref, w_ref) in enumerate(((a_ref, wa_ref), (b_ref, wb_ref), (c_ref, wc_ref))):
            y = jnp.dot(in_ref[...], w_ref[:, cols], preferred_element_type=F32)
            gcols = slice((2 * br + half) * COL_W, (2 * br + half + 1) * COL_W)
            term = jax.nn.sigmoid(gate_refs[2 * br + half][...] + bg_ref[:, gcols]) * y
            merged = term if merged is None else merged + term
        halves.append(merged.astype(BF16))
    merged16 = jnp.concatenate(halves, axis=-1)
    x1 = x_ref[...] + jnp.dot(merged16, wo_ref[...], preferred_element_type=F32)
    x1_ref[...] = x1
    h2_ref[...] = _rms_norm(x1, mg_ref[...]).astype(BF16)


def _merge(a_in, b_in, c_in, z, x, b_gate, w_a, w_b, w_c, w_o, mlp_g, *, tm):
    m = x.shape[0]
    const = lambda i: (0, 0)
    resident = functools.partial(pl.BlockSpec, index_map=const, pipeline_mode=pl.Buffered(1))
    gate_specs = [
        pl.BlockSpec((tm, COL_W), functools.partial(lambda i, col: (i, col), col=COL_GATES + n))
        for n in range(2 * N_BRANCH)
    ]
    return pl.pallas_call(
        _merge_kernel,
        out_shape=[jax.ShapeDtypeStruct((m, D_MODEL), F32), jax.ShapeDtypeStruct((m, D_MODEL), BF16)],
        grid=(m // tm,),
        in_specs=[pl.BlockSpec((tm, D_A), lambda i: (i, 0))] * 3
        + gate_specs
        + [
            pl.BlockSpec((1, N_BRANCH * D_MODEL), const),
            pl.BlockSpec((tm, D_MODEL), lambda i: (i, 0)),
            resident((D_A, D_MODEL)),
            resident((D_B, D_MODEL)),
            resident((D_C, D_MODEL)),
            resident((D_MODEL, D_MODEL)),
            pl.BlockSpec((1, D_MODEL), const),
        ],
        out_specs=[pl.BlockSpec((tm, D_MODEL), lambda i: (i, 0))] * 2,
        compiler_params=_cparams("arbitrary"),
        name="merge",
    )(a_in, b_in, c_in, *([z] * (2 * N_BRANCH)), b_gate.reshape(1, -1), x, w_a, w_b, w_c, w_o,
      mlp_g.reshape(1, D_MODEL))


def _mlp_kernel(h2_ref, x1_ref, wu_ref, wd_ref, fg_ref, y_ref, acc_ref):
    f = pl.program_id(1)

    @pl.when(f == 0)
    def _():
        acc_ref[...] = x1_ref[...]

    t = jnp.square(jnp.maximum(jnp.dot(h2_ref[...], wu_ref[...], preferred_element_type=F32), 0.0))
    acc_ref[...] += jnp.dot(t.astype(BF16), wd_ref[...], preferred_element_type=F32)

    @pl.when(f == pl.num_programs(1) - 1)
    def _():
        y_ref[...] = _rms_norm(acc_ref[...], fg_ref[...])


def _mlp(h2, x1, w_up, w_down, final_g, *, tm, tf):
    m = h2.shape[0]
    return pl.pallas_call(
        _mlp_kernel,
        out_shape=jax.ShapeDtypeStruct((m, D_MODEL), F32),
        grid=(m // tm, D_FF // tf),
        in_specs=[
            pl.BlockSpec((tm, D_MODEL), lambda i, f: (i, 0)),
            pl.BlockSpec((tm, D_MODEL), lambda i, f: (i, 0)),
            pl.BlockSpec((D_MODEL, tf), lambda i, f: (0, f)),
            pl.BlockSpec((tf, D_MODEL), lambda i, f: (f, 0)),
            pl.BlockSpec((1, D_MODEL), lambda i, f: (0, 0)),
        ],
        out_specs=pl.BlockSpec((tm, D_MODEL), lambda i, f: (i, 0)),
        scratch_shapes=[pltpu.VMEM((tm, D_MODEL), F32)],
        compiler_params=_cparams("arbitrary", "arbitrary"),
        name="mlp",
    )(h2, x1, w_up, w_down, final_g.reshape(1, D_MODEL))


def _sgu_operands(sgu_w, sgu_b, chunk):
    reps = CHUNK // chunk
    w_tiled = jnp.tile(sgu_w[:, :chunk, :chunk], (1, reps, reps))
    bias_rows = jnp.repeat(jnp.tile(sgu_b[:, :chunk].T, (reps, 1)), SGU_HEAD, axis=1)
    return w_tiled, bias_rows


PROMPT_ATTN_ROWS = 512
SAMPLE_ATTN_SEQS = 8


def _layer(x, seq, k, v, hist, p, *, want_vn):
    m = x.shape[0]
    batch = m // seq
    chunk = min(seq, CHUNK)
    if p["w_in"].dtype == F32:
        z, w_in16 = _norm_matmul(x, p["attn_norm_g"], p["w_in"], tm=1024, tn=512, emit_w16=True)
        p = {**p, "w_in": w_in16}
    else:
        z = _norm_matmul(x, p["attn_norm_g"], p["w_in"], tm=1024, tn=1024)
    w_tiled, bias_rows = _sgu_operands(p["sgu_w"], p["sgu_b"], chunk)
    sgu_out = _sgu(z, p["sgu_ln_g"], p["sgu_ln_b"], w_tiled, bias_rows, chunk=chunk, rows=512, want_vn=want_vn)
    conv_args = (p["conv_w"], p["conv_b"], p["conv_ln_g"], p["conv_ln_b"])
    if hist is None:
        b_in, new_hist = _conv_prompt(z, batch, seq, *conv_args, t=256)
        steps = seq // PROMPT_ATTN_ROWS
        kv_specs = [pl.BlockSpec((1, N_MEM, D_C), functools.partial(lambda i, col: (i // steps, 0, col), col=col))
                    for col in (0, 1)]
        c_in = _attn(z, k, v, kv_specs, rows=PROMPT_ATTN_ROWS, n_seq=1)
    else:
        b_in, new_hist = _conv_sample(z, hist, seq, *conv_args, n_seq=16)
        kv_specs = [pl.BlockSpec((SAMPLE_ATTN_SEQS, N_MEM * HEAD_ROW_PITCH, LANES), lambda i: (i, 0, 0))] * 2
        c_in = _attn(z, k, v, kv_specs, rows=seq, n_seq=SAMPLE_ATTN_SEQS)
    x1, h2 = _merge(sgu_out[0], b_in, c_in, z, x, p["b_gate"], p["w_a_out"], p["w_b_out"], p["w_c_out"],
                    p["w_o"], p["mlp_norm_g"], tm=256)
    y = _mlp(h2, x1, p["w_up"], p["w_down"], p["final_norm_g"], tm=512, tf=1024)
    return y, new_hist, (sgu_out[1] if want_vn else None), p


def kernel(x_prompt, x_sample, mem_prompt, cache_mem_k, cache_mem_v, state_conv, attn_norm_g, w_in, b_gate,
           sgu_ln_g, sgu_ln_b, sgu_w, sgu_b, w_a_out, conv_w, conv_b, conv_ln_g, conv_ln_b, w_b_out, mem_norm_g,
           w_mem_kv, w_c_out, w_o, mlp_norm_g, w_up, w_down, final_norm_g):
    assert attn_norm_g.shape[0] == 1, "single-layer trunk"
    bp, sp, _ = x_prompt.shape
    bs, ss, _ = x_sample.shape
    p = dict(
        attn_norm_g=attn_norm_g[0], w_in=w_in[0], b_gate=b_gate[0], sgu_ln_g=sgu_ln_g[0],
        sgu_ln_b=sgu_ln_b[0], sgu_w=sgu_w[0], sgu_b=sgu_b[0], w_a_out=w_a_out[0].astype(BF16),
        conv_w=conv_w[0], conv_b=conv_b[0], conv_ln_g=conv_ln_g[0], conv_ln_b=conv_ln_b[0],
        w_b_out=w_b_out[0].astype(BF16), w_c_out=w_c_out[0].astype(BF16), w_o=w_o[0].astype(BF16),
        mlp_norm_g=mlp_norm_g[0], w_up=w_up[0].astype(BF16), w_down=w_down[0].astype(BF16),
        final_norm_g=final_norm_g,
    )
    kv = _norm_matmul(mem_prompt.reshape(bp * N_MEM, D_MODEL), mem_norm_g[0], w_mem_kv[0],
                      tm=1024, tn=512).reshape(bp, N_MEM, 2 * D_C)
    y_s, hist_s, vn_s, p = _layer(x_sample.reshape(bs * ss, D_MODEL), ss, _split_lane_tiles(cache_mem_k[0]),
                                  _split_lane_tiles(cache_mem_v[0]), state_conv[0], p, want_vn=True)
    y_p, hist_p, _, _ = _layer(x_prompt.reshape(bp * sp, D_MODEL), sp, kv, kv, None, p, want_vn=False)
    head_shape = (1, bp, N_MEM, N_MEM_HEADS, MEM_HEAD_DIM)
    return (
        y_p.reshape(bp, sp, D_MODEL),
        y_s.reshape(bs, ss, D_MODEL),
        kv[:, :, :D_C].reshape(head_shape),
        kv[:, :, D_C:].reshape(head_shape),
        hist_p[None],
        hist_s[None],
        vn_s.reshape(1, bs, ss, D_A),
    )
```

```python
import functools
import math
from typing import NamedTuple

import jax
import jax.numpy as jnp
from jax import lax
from jax.experimental import pallas as pl
from jax.experimental.pallas import tpu as pltpu

F32 = jnp.float32
BF16 = jnp.bfloat16

D_MODEL = 2048
D_A = 1024
D_B = 1024
D_C = 1024
CHUNK = 128
SGU_GROUPS = 8
SGU_HEAD = D_A // SGU_GROUPS
CONV_WIDTH = 31
HIST = CONV_WIDTH - 1
N_MEM = 256
N_MEM_HEADS = 4
MEM_HEAD_DIM = D_C // N_MEM_HEADS
N_BRANCH = 3
D_FF = 4 * D_MODEL
N_IN = 2 * D_A + 2 * D_B + D_C + N_BRANCH * D_MODEL
EPS = 1e-6

COL_U, COL_V, COL_GLU_A, COL_GLU_B, COL_Q, COL_GATES = 0, 1, 2, 3, 4, 5
COL_W = 1024


class ZCols(NamedTuple):
    u: int
    v: int
    q: int
    gates: int


Z_COLS_ALL = ZCols(COL_U, COL_V, COL_Q, COL_GATES)
Z_COLS_NO_GLU = ZCols(COL_U, COL_V, COL_Q - 2, COL_GATES - 2)

V7X_VMEM_BYTES = 64 * 1024 * 1024
VMEM_LIMIT_BYTES = V7X_VMEM_BYTES - 8 * 1024 * 1024
SUBLANES = 8
LANES = 128
HIST_PAD = 32


def _cparams(*semantics, flags=None):
    return pltpu.CompilerParams(dimension_semantics=semantics, vmem_limit_bytes=VMEM_LIMIT_BYTES, flags=flags)


def _not_before(value, anchor):
    tile = anchor[-SUBLANES:, -LANES:]
    zero = pltpu.bitcast((pltpu.bitcast(tile, jnp.uint32) >> 16) >> 16, F32)
    reps = (value.shape[0] // SUBLANES, value.shape[1] // LANES)
    return value + jnp.tile(zero, reps)


def _rms_norm(x, g):
    ms = jnp.mean(x * x, axis=-1, keepdims=True)
    return x * lax.rsqrt(ms + EPS) * g


def _layer_norm(x, g, b):
    mu = jnp.mean(x, axis=-1, keepdims=True)
    xc = x - mu
    var = jnp.mean(xc * xc, axis=-1, keepdims=True)
    return xc * lax.rsqrt(var + EPS) * g + b


NORM_ROWS = 128


def _norm_matmul_kernel(x_ref, g_ref, w_ref, o_ref, *rest):
    *w16_refs, h_ref = rest

    @pl.when(pl.program_id(1) == 0)
    def _():
        def body(r, carry):
            rows = pl.ds(pl.multiple_of(r * NORM_ROWS, NORM_ROWS), NORM_ROWS)
            h_ref[rows, :] = _rms_norm(x_ref[rows, :], g_ref[...]).astype(BF16)
            return carry

        lax.fori_loop(0, x_ref.shape[0] // NORM_ROWS, body, 0)

    w16 = w_ref[...].astype(BF16)
    if w16_refs:
        w16_refs[0][...] = w16
    o_ref[...] = jnp.dot(h_ref[...], w16, preferred_element_type=F32)


def _norm_matmul(x, g, w, *, tm, tn, emit_w16=False):
    m, k = x.shape
    n = w.shape[1]
    out_shape = [jax.ShapeDtypeStruct((m, n), F32)]
    out_specs = [pl.BlockSpec((tm, tn), lambda i, j: (i, j))]
    if emit_w16:
        assert m == tm, "every weight block must be visited exactly once"
        out_shape.append(jax.ShapeDtypeStruct((k, n), BF16))
        out_specs.append(pl.BlockSpec((k, tn), lambda i, j: (0, j)))
    outs = pl.pallas_call(
        _norm_matmul_kernel,
        out_shape=out_shape,
        grid=(m // tm, n // tn),
        in_specs=[
            pl.BlockSpec((tm, k), lambda i, j: (i, 0)),
            pl.BlockSpec((1, k), lambda i, j: (0, 0)),
            pl.BlockSpec((k, tn), lambda i, j: (0, j)),
        ],
        out_specs=out_specs,
        scratch_shapes=[pltpu.VMEM((tm, k), BF16)],
        compiler_params=_cparams("arbitrary", "arbitrary"),
        name="norm_matmul",
    )(x, g.reshape(1, k), w)
    return outs if emit_w16 else outs[0]


def _sgu_kernel(u_ref, v_ref, g_ref, b_ref, w_ref, bias_ref, a_ref, *vn_refs, chunk):
    r = lax.broadcasted_iota(jnp.int32, (CHUNK, CHUNK), 0)
    c = lax.broadcasted_iota(jnp.int32, (CHUNK, CHUNK), 1)
    seg_bits = chunk.bit_length() - 1
    same_segment = (r >> seg_bits) == (c >> seg_bits)
    mask = jnp.logical_and(same_segment, (r & (chunk - 1)) >= (c & (chunk - 1)))
    w_s = [jnp.where(mask, w_ref[grp], 0.0).astype(BF16) for grp in range(SGU_GROUPS)]
    for blk in range(u_ref.shape[0] // CHUNK):
        rows = slice(blk * CHUNK, (blk + 1) * CHUNK)
        vn = _layer_norm(jax.nn.gelu(v_ref[rows, :]), g_ref[...], b_ref[...])
        if vn_refs:
            vn_refs[0][rows, :] = vn
        vn16 = vn.astype(BF16)
        for grp in range(SGU_GROUPS):
            cols = slice(grp * SGU_HEAD, (grp + 1) * SGU_HEAD)
            mixed = jnp.dot(w_s[grp], vn16[:, cols], preferred_element_type=F32) + bias_ref[:, cols]
            a_ref[rows, cols] = (jax.nn.gelu(u_ref[rows, cols]) * mixed).astype(BF16)


def _sgu(z, cols, ln_g, ln_b, w_tiled, bias_rows, *, chunk, rows, want_vn):
    m = z.shape[0]
    out_shape = [jax.ShapeDtypeStruct((m, D_A), BF16)]
    out_specs = [pl.BlockSpec((rows, D_A), lambda i: (i, 0))]
    if want_vn:
        out_shape.append(jax.ShapeDtypeStruct((m, D_A), F32))
        out_specs.append(pl.BlockSpec((rows, D_A), lambda i: (i, 0)))
    return pl.pallas_call(
        functools.partial(_sgu_kernel, chunk=chunk),
        out_shape=out_shape,
        grid=(m // rows,),
        in_specs=[
            pl.BlockSpec((rows, COL_W), lambda i: (i, cols.u)),
            pl.BlockSpec((rows, COL_W), lambda i: (i, cols.v)),
            pl.BlockSpec((1, D_A), lambda i: (0, 0)),
            pl.BlockSpec((1, D_A), lambda i: (0, 0)),
            pl.BlockSpec((SGU_GROUPS, CHUNK, CHUNK), lambda i: (0, 0, 0)),
            pl.BlockSpec((CHUNK, D_A), lambda i: (0, 0)),
        ],
        out_specs=out_specs,
        compiler_params=_cparams("arbitrary"),
        name="sgu",
    )(z, z, ln_g.reshape(1, D_A), ln_b.reshape(1, D_A), w_tiled, bias_rows)


CONV_ROWS = 32
CONV_LANES = 128
PROJ_LANES = 256
CONV_LEAD = HIST_PAD - HIST


def _conv_window(win, w_ref, lanes, n_rows):
    out = None
    for b in range(SUBLANES):
        rows_b = n_rows if b == 0 else n_rows + SUBLANES
        y = None
        for a in range(HIST_PAD // SUBLANES + 1):
            k = SUBLANES * a + b - CONV_LEAD
            if 0 <= k < CONV_WIDTH:
                term = w_ref[k : k + 1, lanes] * win[SUBLANES * a : SUBLANES * a + rows_b, :]
                y = term if y is None else y + term
        shifted = y[b : b + n_rows, :]
        out = shifted if out is None else out + shifted
    return out


def _ln_silu_rows(dc_ref, cb_ref, g_ref, b_ref, o_ref, rows):
    dc = dc_ref[rows, :] + cb_ref[...]
    o_ref[rows, :] = jax.nn.silu(_layer_norm(dc, g_ref[...], b_ref[...])).astype(BF16)


CONV_STEP_ROWS = 128
N_GLU_STEPS = 2
N_COLS = N_IN // COL_W


def _glu_first(j):
    return jnp.where(j < N_GLU_STEPS, j + COL_GLU_A, jnp.where(j < COL_GLU_A + N_GLU_STEPS, j - N_GLU_STEPS, j))


def _inproj_conv_kernel(x_ref, g_ref, w_ref, cw_ref, cb_ref, lg_ref, lb_ref, z_ref, bin_ref, hist_ref,
                        h_ref, ext_ref, *, tiles_per_seq):
    i = pl.program_id(0)
    j = pl.program_id(1)
    t = x_ref.shape[0]

    def proj():
        return jnp.dot(h_ref[...], w_ref[...], preferred_element_type=F32)

    @pl.when(j == 0)
    def _():
        def body(r, carry):
            rows = pl.ds(pl.multiple_of(r * NORM_ROWS, NORM_ROWS), NORM_ROWS)
            h_ref[rows, :] = _rms_norm(x_ref[rows, :], g_ref[...]).astype(BF16)
            return carry

        lax.fori_loop(0, t // NORM_ROWS, body, 0)

        @pl.when(i % tiles_per_seq == 0)
        def _():
            ext_ref[0:HIST_PAD, :] = jnp.zeros((HIST_PAD, D_B), F32)

        @pl.when(i % tiles_per_seq != 0)
        def _():
            ext_ref[0:HIST_PAD, :] = ext_ref[t : t + HIST_PAD, :]

        ext_ref[HIST_PAD : HIST_PAD + t, :] = proj()

    @pl.when(j == 1)
    def _():
        ext_ref[HIST_PAD : HIST_PAD + t, :] = ext_ref[HIST_PAD : HIST_PAD + t, :] * jax.nn.sigmoid(proj())
        hist_ref[0] = ext_ref[t + CONV_LEAD : t + HIST_PAD, :]

    @pl.when(j >= N_GLU_STEPS)
    def _():
        chunk = jnp.minimum(j - N_GLU_STEPS, t // CONV_STEP_ROWS - 1)
        r0 = pl.multiple_of(chunk * CONV_STEP_ROWS, CONV_STEP_ROWS)
        n_row_blocks = CONV_STEP_ROWS // CONV_ROWS
        n_lane_blocks = D_B // PROJ_LANES
        rows_per_block = t // n_row_blocks
        conv_done, proj_done = [], []
        for rc in range(n_row_blocks):
            rows = slice(rc * rows_per_block, (rc + 1) * rows_per_block)
            parts = []
            for lb in range(n_lane_blocks):
                lanes = slice(lb * PROJ_LANES, (lb + 1) * PROJ_LANES)
                wins = []
                for cl in range(lb * PROJ_LANES, (lb + 1) * PROJ_LANES, CONV_LANES):
                    clanes = slice(cl, cl + CONV_LANES)
                    win = ext_ref[pl.ds(r0 + rc * CONV_ROWS, CONV_ROWS + HIST_PAD), clanes]
                    if proj_done:
                        win = _not_before(win, proj_done[-1])
                    wins.append(_conv_window(win, cw_ref, clanes, CONV_ROWS))
                piece = jnp.concatenate(wins, axis=-1)
                zp = jnp.dot(h_ref[rows, :], w_ref[:, lanes], preferred_element_type=F32)
                if conv_done:
                    zp = _not_before(zp, conv_done[-1])
                z_ref[rows, lanes] = zp
                conv_done.append(piece)
                proj_done.append(zp)
                parts.append(piece)
            dc = jnp.concatenate(parts, axis=-1) + cb_ref[...]
            y = jax.nn.silu(_layer_norm(dc, lg_ref[...], lb_ref[...]))
            bin_ref[pl.ds(r0 + rc * CONV_ROWS, CONV_ROWS), :] = y.astype(BF16)


def _inproj_conv(x, seq, g, w16, conv_w, conv_b, ln_g, ln_b, *, tm):
    m, k = x.shape
    tiles_per_seq = seq // tm
    n_conv_steps = N_COLS - N_GLU_STEPS
    assert tm // CONV_STEP_ROWS <= n_conv_steps, "not enough grid steps to convolve the whole tile"
    const = lambda i, j: (0, 0)
    return pl.pallas_call(
        functools.partial(_inproj_conv_kernel, tiles_per_seq=tiles_per_seq),
        out_shape=[
            jax.ShapeDtypeStruct((m, n_conv_steps * COL_W), F32),
            jax.ShapeDtypeStruct((m, D_B), BF16),
            jax.ShapeDtypeStruct((m // seq, HIST, D_B), F32),
        ],
        grid=(m // tm, N_COLS),
        in_specs=[
            pl.BlockSpec((tm, k), lambda i, j: (i, 0)),
            pl.BlockSpec((1, k), const),
            pl.BlockSpec((k, COL_W), lambda i, j: (0, _glu_first(j))),
            pl.BlockSpec((CONV_WIDTH, D_B), const),
            pl.BlockSpec((1, D_B), const),
            pl.BlockSpec((1, D_B), const),
            pl.BlockSpec((1, D_B), const),
        ],
        out_specs=[
            pl.BlockSpec((tm, COL_W), lambda i, j: (i, jnp.maximum(j - N_GLU_STEPS, 0))),
            pl.BlockSpec((tm, D_B), lambda i, j: (i, 0)),
            pl.BlockSpec((1, HIST, D_B), lambda i, j: (i // tiles_per_seq, 0, 0)),
        ],
        scratch_shapes=[pltpu.VMEM((tm, k), BF16), pltpu.VMEM((tm + HIST_PAD, D_B), F32)],
        compiler_params=_cparams("arbitrary", "arbitrary", ),
        name="inproj_conv",
    )(x, g.reshape(1, k), w16, conv_w, conv_b.reshape(1, D_B), ln_g.reshape(1, D_B), ln_b.reshape(1, D_B))


def _conv_sample_kernel(ga_ref, gb_ref, hist_ref, w_ref, cb_ref, g_ref, b_ref, o_ref, nh_ref, ext_ref, dc_ref,
                        *, seq):
    n_seq = hist_ref.shape[0]
    c = ga_ref[...] * jax.nn.sigmoid(gb_ref[...])
    for s in range(n_seq):
        ext_ref[s, 0:SUBLANES, :] = jnp.zeros((SUBLANES, D_B), F32)
        ext_ref[s, CONV_LEAD:HIST_PAD, :] = hist_ref[s]
        ext_ref[s, HIST_PAD : HIST_PAD + seq, :] = c[s * seq : (s + 1) * seq, :]
    for s in range(n_seq):
        for lb in range(D_B // CONV_LANES):
            lanes = slice(lb * CONV_LANES, (lb + 1) * CONV_LANES)
            dc_ref[s * seq : (s + 1) * seq, lanes] = _conv_window(ext_ref[s, :, lanes], w_ref, lanes, seq)
        nh_ref[s] = ext_ref[s, seq + CONV_LEAD : seq + HIST_PAD, :]
    _ln_silu_rows(dc_ref, cb_ref, g_ref, b_ref, o_ref, slice(None))


def _conv_sample(z, hist, seq, conv_w, conv_b, ln_g, ln_b, *, n_seq):
    batch = hist.shape[0]
    return pl.pallas_call(
        functools.partial(_conv_sample_kernel, seq=seq),
        out_shape=[
            jax.ShapeDtypeStruct((batch * seq, D_B), BF16),
            jax.ShapeDtypeStruct((batch, HIST, D_B), F32),
        ],
        grid=(batch // n_seq,),
        in_specs=[
            pl.BlockSpec((n_seq * seq, COL_W), lambda i: (i, COL_GLU_A)),
            pl.BlockSpec((n_seq * seq, COL_W), lambda i: (i, COL_GLU_B)),
            pl.BlockSpec((n_seq, HIST, D_B), lambda i: (i, 0, 0)),
            pl.BlockSpec((CONV_WIDTH, D_B), lambda i: (0, 0)),
            pl.BlockSpec((1, D_B), lambda i: (0, 0)),
            pl.BlockSpec((1, D_B), lambda i: (0, 0)),
            pl.BlockSpec((1, D_B), lambda i: (0, 0)),
        ],
        out_specs=[
            pl.BlockSpec((n_seq * seq, D_B), lambda i: (i, 0)),
            pl.BlockSpec((n_seq, HIST, D_B), lambda i: (i, 0, 0)),
        ],
        scratch_shapes=[pltpu.VMEM((n_seq, HIST_PAD + seq, D_B), F32), pltpu.VMEM((n_seq * seq, D_B), F32)],
        compiler_params=_cparams("arbitrary"),
        name="conv_sample",
    )(z, z, hist, conv_w, conv_b.reshape(1, D_B), ln_g.reshape(1, D_B), ln_b.reshape(1, D_B))


def _head_cols(h):
    return slice(h * MEM_HEAD_DIM, (h + 1) * MEM_HEAD_DIM)


HEAD_LANE_TILES = MEM_HEAD_DIM // LANES
HEAD_ROW_PITCH = N_MEM_HEADS * HEAD_LANE_TILES


def _split_lane_tiles(kv):
    b = kv.shape[0]
    kv = kv.reshape(b, N_MEM, N_MEM_HEADS, HEAD_LANE_TILES, LANES)
    return kv.transpose(0, 1, 3, 2, 4).reshape(b, N_MEM * HEAD_ROW_PITCH, LANES)


def _head_of(kv_ref, s, h):
    if kv_ref.shape[-1] == LANES:
        tiles = [kv_ref[s, pl.ds(j * N_MEM_HEADS + h, N_MEM, stride=HEAD_ROW_PITCH), :]
                 for j in range(HEAD_LANE_TILES)]
        return jnp.concatenate(tiles, axis=-1).astype(BF16)
    return kv_ref[s, :, _head_cols(h)].astype(BF16)


def _attn_kernel(q_ref, k_ref, v_ref, o_ref, *, rows):
    scale = 1.0 / math.sqrt(MEM_HEAD_DIM)
    pairs = [(s, h) for s in range(k_ref.shape[0]) for h in range(N_MEM_HEADS)]
    scores = []
    for s, h in pairs:
        q = (q_ref[s * rows : (s + 1) * rows, _head_cols(h)] * scale).astype(BF16)
        scores.append(lax.dot_general(q, _head_of(k_ref, s, h), (((1,), (1,)), ((), ())),
                                      preferred_element_type=F32))
    sc = jnp.concatenate(scores, axis=0)
    p = jnp.exp(sc - jnp.max(sc, axis=-1, keepdims=True))
    p = p / jnp.sum(p, axis=-1, keepdims=True)
    for n, (s, h) in enumerate(pairs):
        ph = p[n * rows : (n + 1) * rows, :].astype(BF16)
        o = jnp.dot(ph, _head_of(v_ref, s, h), preferred_element_type=F32)
        o_ref[s * rows : (s + 1) * rows, _head_cols(h)] = o.astype(BF16)


def _attn(z, cols, k, v, kv_specs, *, rows, n_seq):
    m = z.shape[0]
    return pl.pallas_call(
        functools.partial(_attn_kernel, rows=rows),
        out_shape=jax.ShapeDtypeStruct((m, D_C), BF16),
        grid=(m // (rows * n_seq),),
        in_specs=[pl.BlockSpec((rows * n_seq, COL_W), lambda i: (i, cols.q))] + kv_specs,
        out_specs=pl.BlockSpec((rows * n_seq, D_C), lambda i: (i, 0)),
        compiler_params=_cparams("arbitrary"),
        name="attn",
    )(z, k, v)


def _merge_kernel(a_ref, b_ref, c_ref, *refs):
    gate_refs = refs[: 2 * N_BRANCH]
    bg_ref, x_ref, wa_ref, wb_ref, wc_ref, wo_ref, mg_ref, x1_ref, h2_ref = refs[2 * N_BRANCH :]
    halves = []
    for half in range(2):
        cols = slice(half * COL_W, (half + 1) * COL_W)
        merged = None
        for br, (in_ref, w_ref) in enumerate(((a_ref, wa_ref), (b_ref, wb_ref), (c_ref, wc_ref))):
            y = jnp.dot(in_ref[...], w_ref[:, cols], preferred_element_type=F32)
            gcols = slice((2 * br + half) * COL_W, (2 * br + half + 1) * COL_W)
            term = jax.nn.sigmoid(gate_refs[2 * br + half][...] + bg_ref[:, gcols]) * y
            merged = term if merged is None else merged + term
        halves.append(merged.astype(BF16))
    merged16 = jnp.concatenate(halves, axis=-1)
    x1 = x_ref[...] + jnp.dot(merged16, wo_ref[...], preferred_element_type=F32)
    x1_ref[...] = x1
    h2_ref[...] = _rms_norm(x1, mg_ref[...]).astype(BF16)


def _merge(a_in, b_in, c_in, z, cols, x, b_gate, w_a, w_b, w_c, w_o, mlp_g, *, tm):
    m = x.shape[0]
    const = lambda i: (0, 0)
    resident = functools.partial(pl.BlockSpec, index_map=const, pipeline_mode=pl.Buffered(1))
    gate_specs = [
        pl.BlockSpec((tm, COL_W), functools.partial(lambda i, col: (i, col), col=cols.gates + n))
        for n in range(2 * N_BRANCH)
    ]
    return pl.pallas_call(
        _merge_kernel,
        out_shape=[jax.ShapeDtypeStruct((m, D_MODEL), F32), jax.ShapeDtypeStruct((m, D_MODEL), BF16)],
        grid=(m // tm,),
        in_specs=[pl.BlockSpec((tm, D_A), lambda i: (i, 0))] * 3
        + gate_specs
        + [
            pl.BlockSpec((1, N_BRANCH * D_MODEL), const),
            pl.BlockSpec((tm, D_MODEL), lambda i: (i, 0)),
            resident((D_A, D_MODEL)),
            resident((D_B, D_MODEL)),
            resident((D_C, D_MODEL)),
            resident((D_MODEL, D_MODEL)),
            pl.BlockSpec((1, D_MODEL), const),
        ],
        out_specs=[pl.BlockSpec((tm, D_MODEL), lambda i: (i, 0))] * 2,
        compiler_params=_cparams("arbitrary"),
        name="merge",
    )(a_in, b_in, c_in, *([z] * (2 * N_BRANCH)), b_gate.reshape(1, -1), x, w_a, w_b, w_c, w_o,
      mlp_g.reshape(1, D_MODEL))


def _mlp_kernel(h2_ref, x1_ref, wu_ref, wd_ref, fg_ref, y_ref, acc_ref):
    f = pl.program_id(1)

    @pl.when(f == 0)
    def _():
        acc_ref[...] = x1_ref[...]

    t = jnp.square(jnp.maximum(jnp.dot(h2_ref[...], wu_ref[...], preferred_element_type=F32), 0.0))
    acc_ref[...] += jnp.dot(t.astype(BF16), wd_ref[...], preferred_element_type=F32)

    @pl.when(f == pl.num_programs(1) - 1)
    def _():
        y_ref[...] = _rms_norm(acc_ref[...], fg_ref[...])


def _mlp(h2, x1, w_up, w_down, final_g, *, tm, tf):
    m = h2.shape[0]
    return pl.pallas_call(
        _mlp_kernel,
        out_shape=jax.ShapeDtypeStruct((m, D_MODEL), F32),
        grid=(m // tm, D_FF // tf),
        in_specs=[
            pl.BlockSpec((tm, D_MODEL), lambda i, f: (i, 0)),
            pl.BlockSpec((tm, D_MODEL), lambda i, f: (i, 0)),
            pl.BlockSpec((D_MODEL, tf), lambda i, f: (0, f)),
            pl.BlockSpec((tf, D_MODEL), lambda i, f: (f, 0)),
            pl.BlockSpec((1, D_MODEL), lambda i, f: (0, 0)),
        ],
        out_specs=pl.BlockSpec((tm, D_MODEL), lambda i, f: (i, 0)),
        scratch_shapes=[pltpu.VMEM((tm, D_MODEL), F32)],
        compiler_params=_cparams("arbitrary", "arbitrary"),
        name="mlp",
    )(h2, x1, w_up, w_down, final_g.reshape(1, D_MODEL))


def _sgu_operands(sgu_w, sgu_b, chunk):
    reps = CHUNK // chunk
    w_tiled = jnp.tile(sgu_w[:, :chunk, :chunk], (1, reps, reps))
    bias_rows = jnp.repeat(jnp.tile(sgu_b[:, :chunk].T, (reps, 1)), SGU_HEAD, axis=1)
    return w_tiled, bias_rows


PROMPT_ATTN_ROWS = 512
SAMPLE_ATTN_SEQS = 8


def _layer(x, seq, k, v, hist, p, *, want_vn):
    chunk = min(seq, CHUNK)
    conv_args = (p["conv_w"], p["conv_b"], p["conv_ln_g"], p["conv_ln_b"])
    if hist is None:
        assert p["w_in"].dtype == BF16, "the group with a single token tile goes first and rounds w_in"
        z, b_in, new_hist = _inproj_conv(x, seq, p["attn_norm_g"], p["w_in"], *conv_args, tm=1024)
        cols = Z_COLS_NO_GLU
    else:
        z, w_in16 = _norm_matmul(x, p["attn_norm_g"], p["w_in"], tm=1024, tn=512, emit_w16=True)
        p = {**p, "w_in": w_in16}
        cols = Z_COLS_ALL
        b_in, new_hist = _conv_sample(z, hist, seq, *conv_args, n_seq=16)
    w_tiled, bias_rows = _sgu_operands(p["sgu_w"], p["sgu_b"], chunk)
    sgu_out = _sgu(z, cols, p["sgu_ln_g"], p["sgu_ln_b"], w_tiled, bias_rows, chunk=chunk, rows=512,
                   want_vn=want_vn)
    if hist is None:
        steps = seq // PROMPT_ATTN_ROWS
        kv_specs = [pl.BlockSpec((1, N_MEM, D_C), functools.partial(lambda i, col: (i // steps, 0, col), col=col))
                    for col in (0, 1)]
        c_in = _attn(z, cols, k, v, kv_specs, rows=PROMPT_ATTN_ROWS, n_seq=1)
    else:
        kv_specs = [pl.BlockSpec((SAMPLE_ATTN_SEQS, N_MEM * HEAD_ROW_PITCH, LANES), lambda i: (i, 0, 0))] * 2
        c_in = _attn(z, cols, k, v, kv_specs, rows=seq, n_seq=SAMPLE_ATTN_SEQS)
    x1, h2 = _merge(sgu_out[0], b_in, c_in, z, cols, x, p["b_gate"], p["w_a_out"], p["w_b_out"], p["w_c_out"],
                    p["w_o"], p["mlp_norm_g"], tm=256)
    y = _mlp(h2, x1, p["w_up"], p["w_down"], p["final_norm_g"], tm=512, tf=1024)
    return y, new_hist, (sgu_out[1] if want_vn else None), p


def kernel(x_prompt, x_sample, mem_prompt, cache_mem_k, cache_mem_v, state_conv, attn_norm_g, w_in, b_gate,
           sgu_ln_g, sgu_ln_b, sgu_w, sgu_b, w_a_out, conv_w, conv_b, conv_ln_g, conv_ln_b, w_b_out, mem_norm_g,
           w_mem_kv, w_c_out, w_o, mlp_norm_g, w_up, w_down, final_norm_g):
    assert attn_norm_g.shape[0] == 1, "single-layer trunk"
    bp, sp, _ = x_prompt.shape
    bs, ss, _ = x_sample.shape
    p = dict(
        attn_norm_g=attn_norm_g[0], w_in=w_in[0], b_gate=b_gate[0], sgu_ln_g=sgu_ln_g[0],
        sgu_ln_b=sgu_ln_b[0], sgu_w=sgu_w[0], sgu_b=sgu_b[0], w_a_out=w_a_out[0].astype(BF16),
        conv_w=conv_w[0], conv_b=conv_b[0], conv_ln_g=conv_ln_g[0], conv_ln_b=conv_ln_b[0],
        w_b_out=w_b_out[0].astype(BF16), w_c_out=w_c_out[0].astype(BF16), w_o=w_o[0].astype(BF16),
        mlp_norm_g=mlp_norm_g[0], w_up=w_up[0].astype(BF16), w_down=w_down[0].astype(BF16),
        final_norm_g=final_norm_g,
    )
    kv = _norm_matmul(mem_prompt.reshape(bp * N_MEM, D_MODEL), mem_norm_g[0], w_mem_kv[0],
                      tm=1024, tn=512).reshape(bp, N_MEM, 2 * D_C)
    y_s, hist_s, vn_s, p = _layer(x_sample.reshape(bs * ss, D_MODEL), ss, _split_lane_tiles(cache_mem_k[0]),
                                  _split_lane_tiles(cache_mem_v[0]), state_conv[0], p, want_vn=True)
    y_p, hist_p, _, _ = _layer(x_prompt.reshape(bp * sp, D_MODEL), sp, kv, kv, None, p, want_vn=False)
    head_shape = (1, bp, N_MEM, N_MEM_HEADS, MEM_HEAD_DIM)
    return (
        y_p.reshape(bp, sp, D_MODEL),
        y_s.reshape(bs, ss, D_MODEL),
        kv[:, :, :D_C].reshape(head_shape),
        kv[:, :, D_C:].reshape(head_shape),
        hist_p[None],
        hist_s[None],
        vn_s.reshape(1, bs, ss, D_A),
    )
```

```python
import functools
import math
from typing import NamedTuple

import jax
import jax.numpy as jnp
from jax import lax
from jax.experimental import pallas as pl
from jax.experimental.pallas import tpu as pltpu

F32 = jnp.float32
BF16 = jnp.bfloat16

D_MODEL = 2048
D_A = 1024
D_B = 1024
D_C = 1024
CHUNK = 128
SGU_GROUPS = 8
SGU_HEAD = D_A // SGU_GROUPS
CONV_WIDTH = 31
HIST = CONV_WIDTH - 1
N_MEM = 256
N_MEM_HEADS = 4
MEM_HEAD_DIM = D_C // N_MEM_HEADS
N_BRANCH = 3
D_FF = 4 * D_MODEL
N_IN = 2 * D_A + 2 * D_B + D_C + N_BRANCH * D_MODEL
EPS = 1e-6

COL_U, COL_V, COL_GLU_A, COL_GLU_B, COL_Q, COL_GATES = 0, 1, 2, 3, 4, 5
COL_W = 1024


class ZCols(NamedTuple):
    u: int
    v: int
    q: int
    gates: int


Z_COLS_ALL = ZCols(COL_U, COL_V, COL_Q, COL_GATES)
Z_COLS_NO_GLU = ZCols(COL_U, COL_V, COL_Q - 2, COL_GATES - 2)

V7X_VMEM_BYTES = 64 * 1024 * 1024
VMEM_LIMIT_BYTES = V7X_VMEM_BYTES - 8 * 1024 * 1024
SUBLANES = 8
LANES = 128
HIST_PAD = 32


def _cparams(*semantics, flags=None):
    return pltpu.CompilerParams(dimension_semantics=semantics, vmem_limit_bytes=VMEM_LIMIT_BYTES, flags=flags)


def _not_before(value, anchor):
    tile = anchor[-SUBLANES:, -LANES:]
    zero = pltpu.bitcast((pltpu.bitcast(tile, jnp.uint32) >> 16) >> 16, F32)
    reps = (value.shape[0] // SUBLANES, value.shape[1] // LANES)
    return value + jnp.tile(zero, reps)


def _rms_norm(x, g):
    ms = jnp.mean(x * x, axis=-1, keepdims=True)
    return x * lax.rsqrt(ms + EPS) * g


def _layer_norm(x, g, b):
    mu = jnp.mean(x, axis=-1, keepdims=True)
    xc = x - mu
    var = jnp.mean(xc * xc, axis=-1, keepdims=True)
    return xc * lax.rsqrt(var + EPS) * g + b


NORM_ROWS = 128


def _norm_matmul_kernel(x_ref, g_ref, w_ref, o_ref, *rest):
    *w16_refs, h_ref = rest

    @pl.when(pl.program_id(1) == 0)
    def _():
        def body(r, carry):
            rows = pl.ds(pl.multiple_of(r * NORM_ROWS, NORM_ROWS), NORM_ROWS)
            h_ref[rows, :] = _rms_norm(x_ref[rows, :], g_ref[...]).astype(BF16)
            return carry

        lax.fori_loop(0, x_ref.shape[0] // NORM_ROWS, body, 0)

    w16 = w_ref[...].astype(BF16)
    if w16_refs:
        w16_refs[0][...] = w16
    o_ref[...] = jnp.dot(h_ref[...], w16, preferred_element_type=F32)


def _norm_matmul(x, g, w, *, tm, tn, emit_w16=False):
    m, k = x.shape
    n = w.shape[1]
    out_shape = [jax.ShapeDtypeStruct((m, n), F32)]
    out_specs = [pl.BlockSpec((tm, tn), lambda i, j: (i, j))]
    if emit_w16:
        assert m == tm, "every weight block must be visited exactly once"
        out_shape.append(jax.ShapeDtypeStruct((k, n), BF16))
        out_specs.append(pl.BlockSpec((k, tn), lambda i, j: (0, j)))
    outs = pl.pallas_call(
        _norm_matmul_kernel,
        out_shape=out_shape,
        grid=(m // tm, n // tn),
        in_specs=[
            pl.BlockSpec((tm, k), lambda i, j: (i, 0)),
            pl.BlockSpec((1, k), lambda i, j: (0, 0)),
            pl.BlockSpec((k, tn), lambda i, j: (0, j)),
        ],
        out_specs=out_specs,
        scratch_shapes=[pltpu.VMEM((tm, k), BF16)],
        compiler_params=_cparams("arbitrary", "arbitrary"),
        name="norm_matmul",
    )(x, g.reshape(1, k), w)
    return outs if emit_w16 else outs[0]


def _sgu_kernel(u_ref, v_ref, g_ref, b_ref, w_ref, bias_ref, a_ref, *vn_refs, chunk):
    r = lax.broadcasted_iota(jnp.int32, (CHUNK, CHUNK), 0)
    c = lax.broadcasted_iota(jnp.int32, (CHUNK, CHUNK), 1)
    seg_bits = chunk.bit_length() - 1
    same_segment = (r >> seg_bits) == (c >> seg_bits)
    mask = jnp.logical_and(same_segment, (r & (chunk - 1)) >= (c & (chunk - 1)))
    w_s = [jnp.where(mask, w_ref[grp], 0.0).astype(BF16) for grp in range(SGU_GROUPS)]
    for blk in range(u_ref.shape[0] // CHUNK):
        rows = slice(blk * CHUNK, (blk + 1) * CHUNK)
        vn = _layer_norm(jax.nn.gelu(v_ref[rows, :]), g_ref[...], b_ref[...])
        if vn_refs:
            vn_refs[0][rows, :] = vn
        vn16 = vn.astype(BF16)
        for grp in range(SGU_GROUPS):
            cols = slice(grp * SGU_HEAD, (grp + 1) * SGU_HEAD)
            mixed = jnp.dot(w_s[grp], vn16[:, cols], preferred_element_type=F32) + bias_ref[:, cols]
            a_ref[rows, cols] = (jax.nn.gelu(u_ref[rows, cols]) * mixed).astype(BF16)


def _sgu(z, cols, ln_g, ln_b, w_tiled, bias_rows, *, chunk, rows, want_vn):
    m = z.shape[0]
    out_shape = [jax.ShapeDtypeStruct((m, D_A), BF16)]
    out_specs = [pl.BlockSpec((rows, D_A), lambda i: (i, 0))]
    if want_vn:
        out_shape.append(jax.ShapeDtypeStruct((m, D_A), F32))
        out_specs.append(pl.BlockSpec((rows, D_A), lambda i: (i, 0)))
    return pl.pallas_call(
        functools.partial(_sgu_kernel, chunk=chunk),
        out_shape=out_shape,
        grid=(m // rows,),
        in_specs=[
            pl.BlockSpec((rows, COL_W), lambda i: (i, cols.u)),
            pl.BlockSpec((rows, COL_W), lambda i: (i, cols.v)),
            pl.BlockSpec((1, D_A), lambda i: (0, 0)),
            pl.BlockSpec((1, D_A), lambda i: (0, 0)),
            pl.BlockSpec((SGU_GROUPS, CHUNK, CHUNK), lambda i: (0, 0, 0)),
            pl.BlockSpec((CHUNK, D_A), lambda i: (0, 0)),
        ],
        out_specs=out_specs,
        compiler_params=_cparams("arbitrary"),
        name="sgu",
    )(z, z, ln_g.reshape(1, D_A), ln_b.reshape(1, D_A), w_tiled, bias_rows)


CONV_ROWS = 32
CONV_LANES = 128
PROJ_LANES = 256
PROJ_ROWS = 256
CONV_LEAD = HIST_PAD - HIST


def _conv_window(win, w_ref, lanes, n_rows):
    out = None
    for b in range(SUBLANES):
        rows_b = n_rows if b == 0 else n_rows + SUBLANES
        y = None
        for a in range(HIST_PAD // SUBLANES + 1):
            k = SUBLANES * a + b - CONV_LEAD
            if 0 <= k < CONV_WIDTH:
                term = w_ref[k : k + 1, lanes] * win[SUBLANES * a : SUBLANES * a + rows_b, :]
                y = term if y is None else y + term
        shifted = y[b : b + n_rows, :]
        out = shifted if out is None else out + shifted
    return out


def _ln_silu_rows(dc_ref, cb_ref, g_ref, b_ref, o_ref, rows):
    dc = dc_ref[rows, :] + cb_ref[...]
    o_ref[rows, :] = jax.nn.silu(_layer_norm(dc, g_ref[...], b_ref[...])).astype(BF16)


CONV_STEP_ROWS = 128
N_GLU_STEPS = 2
N_COLS = N_IN // COL_W


def _glu_first(j):
    return jnp.where(j < N_GLU_STEPS, j + COL_GLU_A, jnp.where(j < COL_GLU_A + N_GLU_STEPS, j - N_GLU_STEPS, j))


def _inproj_conv_kernel(x_ref, g_ref, w_ref, cw_ref, cb_ref, lg_ref, lb_ref, z_ref, bin_ref, hist_ref,
                        h_ref, ext_ref, *, tiles_per_seq):
    i = pl.program_id(0)
    j = pl.program_id(1)
    t = x_ref.shape[0]

    def proj():
        return jnp.dot(h_ref[...], w_ref[...], preferred_element_type=F32)

    @pl.when(j == 0)
    def _():
        @pl.when(i % tiles_per_seq == 0)
        def _():
            ext_ref[0:HIST_PAD, :] = jnp.zeros((HIST_PAD, D_B), F32)

        @pl.when(i % tiles_per_seq != 0)
        def _():
            ext_ref[0:HIST_PAD, :] = ext_ref[t : t + HIST_PAD, :]

        for rb in range(t // PROJ_ROWS):
            for r in range(rb * PROJ_ROWS, (rb + 1) * PROJ_ROWS, NORM_ROWS):
                h_ref[r : r + NORM_ROWS, :] = _rms_norm(x_ref[r : r + NORM_ROWS, :], g_ref[...]).astype(BF16)
            rows = slice(rb * PROJ_ROWS, (rb + 1) * PROJ_ROWS)
            ext_ref[HIST_PAD + rb * PROJ_ROWS : HIST_PAD + (rb + 1) * PROJ_ROWS, :] = jnp.dot(
                h_ref[rows, :], w_ref[...], preferred_element_type=F32)

    @pl.when(j == 1)
    def _():
        ext_ref[HIST_PAD : HIST_PAD + t, :] = ext_ref[HIST_PAD : HIST_PAD + t, :] * jax.nn.sigmoid(proj())
        hist_ref[0] = ext_ref[t + CONV_LEAD : t + HIST_PAD, :]

    @pl.when(j >= N_GLU_STEPS)
    def _():
        chunk = jnp.minimum(j - N_GLU_STEPS, t // CONV_STEP_ROWS - 1)
        r0 = pl.multiple_of(chunk * CONV_STEP_ROWS, CONV_STEP_ROWS)
        n_row_blocks = CONV_STEP_ROWS // CONV_ROWS
        n_lane_blocks = D_B // PROJ_LANES
        rows_per_block = t // n_row_blocks
        conv_done, proj_done = [], []
        for rc in range(n_row_blocks):
            rows = slice(rc * rows_per_block, (rc + 1) * rows_per_block)
            parts = []
            for lb in range(n_lane_blocks):
                lanes = slice(lb * PROJ_LANES, (lb + 1) * PROJ_LANES)
                wins = []
                for cl in range(lb * PROJ_LANES, (lb + 1) * PROJ_LANES, CONV_LANES):
                    clanes = slice(cl, cl + CONV_LANES)
                    win = ext_ref[pl.ds(r0 + rc * CONV_ROWS, CONV_ROWS + HIST_PAD), clanes]
                    if proj_done:
                        win = _not_before(win, proj_done[-1])
                    wins.append(_conv_window(win, cw_ref, clanes, CONV_ROWS))
                piece = jnp.concatenate(wins, axis=-1)
                zp = jnp.dot(h_ref[rows, :], w_ref[:, lanes], preferred_element_type=F32)
                if conv_done:
                    zp = _not_before(zp, conv_done[-1])
                z_ref[rows, lanes] = zp
                conv_done.append(piece)
                proj_done.append(zp)
                parts.append(piece)
            dc = jnp.concatenate(parts, axis=-1) + cb_ref[...]
            y = jax.nn.silu(_layer_norm(dc, lg_ref[...], lb_ref[...]))
            bin_ref[pl.ds(r0 + rc * CONV_ROWS, CONV_ROWS), :] = y.astype(BF16)


def _inproj_conv(x, seq, g, w16, conv_w, conv_b, ln_g, ln_b, *, tm):
    m, k = x.shape
    tiles_per_seq = seq // tm
    n_conv_steps = N_COLS - N_GLU_STEPS
    assert tm // CONV_STEP_ROWS <= n_conv_steps, "not enough grid steps to convolve the whole tile"
    const = lambda i, j: (0, 0)
    return pl.pallas_call(
        functools.partial(_inproj_conv_kernel, tiles_per_seq=tiles_per_seq),
        out_shape=[
            jax.ShapeDtypeStruct((m, n_conv_steps * COL_W), F32),
            jax.ShapeDtypeStruct((m, D_B), BF16),
            jax.ShapeDtypeStruct((m // seq, HIST, D_B), F32),
        ],
        grid=(m // tm, N_COLS),
        in_specs=[
            pl.BlockSpec((tm, k), lambda i, j: (i, 0)),
            pl.BlockSpec((1, k), const),
            pl.BlockSpec((k, COL_W), lambda i, j: (0, _glu_first(j))),
            pl.BlockSpec((CONV_WIDTH, D_B), const),
            pl.BlockSpec((1, D_B), const),
            pl.BlockSpec((1, D_B), const),
            pl.BlockSpec((1, D_B), const),
        ],
        out_specs=[
            pl.BlockSpec((tm, COL_W), lambda i, j: (i, jnp.maximum(j - N_GLU_STEPS, 0))),
            pl.BlockSpec((tm, D_B), lambda i, j: (i, 0)),
            pl.BlockSpec((1, HIST, D_B), lambda i, j: (i // tiles_per_seq, 0, 0)),
        ],
        scratch_shapes=[pltpu.VMEM((tm, k), BF16), pltpu.VMEM((tm + HIST_PAD, D_B), F32)],
        compiler_params=_cparams("arbitrary", "arbitrary", ),
        name="inproj_conv",
    )(x, g.reshape(1, k), w16, conv_w, conv_b.reshape(1, D_B), ln_g.reshape(1, D_B), ln_b.reshape(1, D_B))


def _conv_sample_kernel(ga_ref, gb_ref, hist_ref, w_ref, cb_ref, g_ref, b_ref, o_ref, nh_ref, ext_ref, dc_ref,
                        *, seq):
    n_seq = hist_ref.shape[0]
    c = ga_ref[...] * jax.nn.sigmoid(gb_ref[...])
    for s in range(n_seq):
        ext_ref[s, 0:SUBLANES, :] = jnp.zeros((SUBLANES, D_B), F32)
        ext_ref[s, CONV_LEAD:HIST_PAD, :] = hist_ref[s]
        ext_ref[s, HIST_PAD : HIST_PAD + seq, :] = c[s * seq : (s + 1) * seq, :]
    for s in range(n_seq):
        for lb in range(D_B // CONV_LANES):
            lanes = slice(lb * CONV_LANES, (lb + 1) * CONV_LANES)
            dc_ref[s * seq : (s + 1) * seq, lanes] = _conv_window(ext_ref[s, :, lanes], w_ref, lanes, seq)
        nh_ref[s] = ext_ref[s, seq + CONV_LEAD : seq + HIST_PAD, :]
    _ln_silu_rows(dc_ref, cb_ref, g_ref, b_ref, o_ref, slice(None))


def _conv_sample(z, hist, seq, conv_w, conv_b, ln_g, ln_b, *, n_seq):
    batch = hist.shape[0]
    return pl.pallas_call(
        functools.partial(_conv_sample_kernel, seq=seq),
        out_shape=[
            jax.ShapeDtypeStruct((batch * seq, D_B), BF16),
            jax.ShapeDtypeStruct((batch, HIST, D_B), F32),
        ],
        grid=(batch // n_seq,),
        in_specs=[
            pl.BlockSpec((n_seq * seq, COL_W), lambda i: (i, COL_GLU_A)),
            pl.BlockSpec((n_seq * seq, COL_W), lambda i: (i, COL_GLU_B)),
            pl.BlockSpec((n_seq, HIST, D_B), lambda i: (i, 0, 0)),
            pl.BlockSpec((CONV_WIDTH, D_B), lambda i: (0, 0)),
            pl.BlockSpec((1, D_B), lambda i: (0, 0)),
            pl.BlockSpec((1, D_B), lambda i: (0, 0)),
            pl.BlockSpec((1, D_B), lambda i: (0, 0)),
        ],
        out_specs=[
            pl.BlockSpec((n_seq * seq, D_B), lambda i: (i, 0)),
            pl.BlockSpec((n_seq, HIST, D_B), lambda i: (i, 0, 0)),
        ],
        scratch_shapes=[pltpu.VMEM((n_seq, HIST_PAD + seq, D_B), F32), pltpu.VMEM((n_seq * seq, D_B), F32)],
        compiler_params=_cparams("arbitrary"),
        name="conv_sample",
    )(z, z, hist, conv_w, conv_b.reshape(1, D_B), ln_g.reshape(1, D_B), ln_b.reshape(1, D_B))


def _head_cols(h):
    return slice(h * MEM_HEAD_DIM, (h + 1) * MEM_HEAD_DIM)


HEAD_LANE_TILES = MEM_HEAD_DIM // LANES
HEAD_ROW_PITCH = N_MEM_HEADS * HEAD_LANE_TILES


def _split_lane_tiles(kv):
    b = kv.shape[0]
    kv = kv.reshape(b, N_MEM, N_MEM_HEADS, HEAD_LANE_TILES, LANES)
    return kv.transpose(0, 1, 3, 2, 4).reshape(b, N_MEM * HEAD_ROW_PITCH, LANES)


def _head_of(kv_ref, s, h):
    if kv_ref.shape[-1] == LANES:
        tiles = [kv_ref[s, pl.ds(j * N_MEM_HEADS + h, N_MEM, stride=HEAD_ROW_PITCH), :]
                 for j in range(HEAD_LANE_TILES)]
        return jnp.concatenate(tiles, axis=-1).astype(BF16)
    return kv_ref[s, :, _head_cols(h)].astype(BF16)


def _attn_kernel(q_ref, k_ref, v_ref, o_ref, *, rows):
    scale = 1.0 / math.sqrt(MEM_HEAD_DIM)
    pairs = [(s, h) for s in range(k_ref.shape[0]) for h in range(N_MEM_HEADS)]
    scores = []
    for s, h in pairs:
        q = (q_ref[s * rows : (s + 1) * rows, _head_cols(h)] * scale).astype(BF16)
        scores.append(lax.dot_general(q, _head_of(k_ref, s, h), (((1,), (1,)), ((), ())),
                                      preferred_element_type=F32))
    sc = jnp.concatenate(scores, axis=0)
    p = jnp.exp(sc - jnp.max(sc, axis=-1, keepdims=True))
    p = p / jnp.sum(p, axis=-1, keepdims=True)
    for n, (s, h) in enumerate(pairs):
        ph = p[n * rows : (n + 1) * rows, :].astype(BF16)
        o = jnp.dot(ph, _head_of(v_ref, s, h), preferred_element_type=F32)
        o_ref[s * rows : (s + 1) * rows, _head_cols(h)] = o.astype(BF16)


def _attn(z, cols, k, v, kv_specs, *, rows, n_seq):
    m = z.shape[0]
    return pl.pallas_call(
        functools.partial(_attn_kernel, rows=rows),
        out_shape=jax.ShapeDtypeStruct((m, D_C), BF16),
        grid=(m // (rows * n_seq),),
        in_specs=[pl.BlockSpec((rows * n_seq, COL_W), lambda i: (i, cols.q))] + kv_specs,
        out_specs=pl.BlockSpec((rows * n_seq, D_C), lambda i: (i, 0)),
        compiler_params=_cparams("arbitrary"),
        name="attn",
    )(z, k, v)


def _merge_kernel(a_ref, b_ref, c_ref, *refs):
    gate_refs = refs[: 2 * N_BRANCH]
    bg_ref, x_ref, wa_ref, wb_ref, wc_ref, wo_ref, mg_ref, x1_ref, h2_ref = refs[2 * N_BRANCH :]
    halves = []
    for half in range(2):
        cols = slice(half * COL_W, (half + 1) * COL_W)
        merged = None
        for br, (in_ref, w_ref) in enumerate(((a_ref, wa_ref), (b_ref, wb_ref), (c_ref, wc_ref))):
            y = jnp.dot(in_ref[...], w_ref[:, cols], preferred_element_type=F32)
            gcols = slice((2 * br + half) * COL_W, (2 * br + half + 1) * COL_W)
            term = jax.nn.sigmoid(gate_refs[2 * br + half][...] + bg_ref[:, gcols]) * y
            merged = term if merged is None else merged + term
        halves.append(merged.astype(BF16))
    merged16 = jnp.concatenate(halves, axis=-1)
    x1 = x_ref[...] + jnp.dot(merged16, wo_ref[...], preferred_element_type=F32)
    x1_ref[...] = x1
    h2_ref[...] = _rms_norm(x1, mg_ref[...]).astype(BF16)


def _merge(a_in, b_in, c_in, z, cols, x, b_gate, w_a, w_b, w_c, w_o, mlp_g, *, tm):
    m = x.shape[0]
    const = lambda i: (0, 0)
    resident = functools.partial(pl.BlockSpec, index_map=const, pipeline_mode=pl.Buffered(1))
    gate_specs = [
        pl.BlockSpec((tm, COL_W), functools.partial(lambda i, col: (i, col), col=cols.gates + n))
        for n in range(2 * N_BRANCH)
    ]
    return pl.pallas_call(
        _merge_kernel,
        out_shape=[jax.ShapeDtypeStruct((m, D_MODEL), F32), jax.ShapeDtypeStruct((m, D_MODEL), BF16)],
        grid=(m // tm,),
        in_specs=[pl.BlockSpec((tm, D_A), lambda i: (i, 0))] * 3
        + gate_specs
        + [
            pl.BlockSpec((1, N_BRANCH * D_MODEL), const),
            pl.BlockSpec((tm, D_MODEL), lambda i: (i, 0)),
            resident((D_A, D_MODEL)),
            resident((D_B, D_MODEL)),
            resident((D_C, D_MODEL)),
            resident((D_MODEL, D_MODEL)),
            pl.BlockSpec((1, D_MODEL), const),
        ],
        out_specs=[pl.BlockSpec((tm, D_MODEL), lambda i: (i, 0))] * 2,
        compiler_params=_cparams("arbitrary"),
        name="merge",
    )(a_in, b_in, c_in, *([z] * (2 * N_BRANCH)), b_gate.reshape(1, -1), x, w_a, w_b, w_c, w_o,
      mlp_g.reshape(1, D_MODEL))


def _mlp_kernel(h2_ref, x1_ref, wu_ref, wd_ref, fg_ref, y_ref, acc_ref):
    f = pl.program_id(1)

    @pl.when(f == 0)
    def _():
        acc_ref[...] = x1_ref[...]

    t = jnp.square(jnp.maximum(jnp.dot(h2_ref[...], wu_ref[...], preferred_element_type=F32), 0.0))
    acc_ref[...] += jnp.dot(t.astype(BF16), wd_ref[...], preferred_element_type=F32)

    @pl.when(f == pl.num_programs(1) - 1)
    def _():
        y_ref[...] = _rms_norm(acc_ref[...], fg_ref[...])


def _mlp(h2, x1, w_up, w_down, final_g, *, tm, tf):
    m = h2.shape[0]
    return pl.pallas_call(
        _mlp_kernel,
        out_shape=jax.ShapeDtypeStruct((m, D_MODEL), F32),
        grid=(m // tm, D_FF // tf),
        in_specs=[
            pl.BlockSpec((tm, D_MODEL), lambda i, f: (i, 0)),
            pl.BlockSpec((tm, D_MODEL), lambda i, f: (i, 0)),
            pl.BlockSpec((D_MODEL, tf), lambda i, f: (0, f)),
            pl.BlockSpec((tf, D_MODEL), lambda i, f: (f, 0)),
            pl.BlockSpec((1, D_MODEL), lambda i, f: (0, 0)),
        ],
        out_specs=pl.BlockSpec((tm, D_MODEL), lambda i, f: (i, 0)),
        scratch_shapes=[pltpu.VMEM((tm, D_MODEL), F32)],
        compiler_params=_cparams("arbitrary", "arbitrary"),
        name="mlp",
    )(h2, x1, w_up, w_down, final_g.reshape(1, D_MODEL))


def _mlp_first_kernel(h2_ref, wu_ref, wd_ref, acc_out_ref, wu16_ref, wd16_ref, acc_ref):
    f = pl.program_id(0)
    i = pl.program_id(1)

    @pl.when(f == 0)
    def _():
        acc_ref[i] = jnp.zeros(acc_ref.shape[1:], F32)

    h2 = h2_ref[...]
    t_pieces = []
    for c in range(0, wu_ref.shape[1], PROJ_LANES):
        wu = wu_ref[:, c : c + PROJ_LANES].astype(BF16)
        wu16_ref[:, c : c + PROJ_LANES] = wu
        t = jnp.square(jnp.maximum(jnp.dot(h2, wu, preferred_element_type=F32), 0.0))
        t_pieces.append(t.astype(BF16))
    t16 = jnp.concatenate(t_pieces, axis=-1)
    for c in range(0, wd_ref.shape[1], PROJ_LANES):
        wd = wd_ref[:, c : c + PROJ_LANES].astype(BF16)
        wd16_ref[:, c : c + PROJ_LANES] = wd
        acc_ref[i, :, c : c + PROJ_LANES] += jnp.dot(t16, wd, preferred_element_type=F32)

    @pl.when(f == pl.num_programs(0) - 1)
    def _():
        acc_out_ref[...] = acc_ref[i]


def _mlp_first(h2, w_up, w_down, *, tm, tf):
    m = h2.shape[0]
    n_f = D_FF // tf
    return pl.pallas_call(
        _mlp_first_kernel,
        out_shape=[
            jax.ShapeDtypeStruct((m, D_MODEL), F32),
            jax.ShapeDtypeStruct((D_MODEL, D_FF), BF16),
            jax.ShapeDtypeStruct((D_FF, D_MODEL), BF16),
        ],
        grid=(n_f, m // tm),
        in_specs=[
            pl.BlockSpec((tm, D_MODEL), lambda f, i: (i, 0)),
            pl.BlockSpec((D_MODEL, tf), lambda f, i: (0, f)),
            pl.BlockSpec((tf, D_MODEL), lambda f, i: (f, 0)),
        ],
        out_specs=[
            pl.BlockSpec((tm, D_MODEL), lambda f, i: (jnp.where(f == n_f - 1, i, 0), 0)),
            pl.BlockSpec((D_MODEL, tf), lambda f, i: (0, f)),
            pl.BlockSpec((tf, D_MODEL), lambda f, i: (f, 0)),
        ],
        scratch_shapes=[pltpu.VMEM((m // tm, tm, D_MODEL), F32)],
        compiler_params=_cparams("arbitrary", "arbitrary"),
        name="mlp_first",
    )(h2, w_up, w_down)


def _residual_norm_kernel(x_ref, d_ref, g_ref, y_ref):
    y_ref[...] = _rms_norm(x_ref[...] + d_ref[...], g_ref[...])


def _residual_norm(x, d, g, *, tm):
    m = x.shape[0]
    row_spec = pl.BlockSpec((tm, D_MODEL), lambda i: (i, 0))
    return pl.pallas_call(
        _residual_norm_kernel,
        out_shape=jax.ShapeDtypeStruct((m, D_MODEL), F32),
        grid=(m // tm,),
        in_specs=[row_spec, row_spec, pl.BlockSpec((1, D_MODEL), lambda i: (0, 0))],
        out_specs=row_spec,
        compiler_params=_cparams("arbitrary"),
        name="residual_norm",
    )(x, d, g.reshape(1, D_MODEL))


def _sgu_operands(sgu_w, sgu_b, chunk):
    reps = CHUNK // chunk
    w_tiled = jnp.tile(sgu_w[:, :chunk, :chunk], (1, reps, reps))
    bias_rows = jnp.repeat(jnp.tile(sgu_b[:, :chunk].T, (reps, 1)), SGU_HEAD, axis=1)
    return w_tiled, bias_rows


PROMPT_ATTN_ROWS = 512
SAMPLE_ATTN_SEQS = 8


def _layer(x, seq, k, v, hist, p, *, want_vn):
    chunk = min(seq, CHUNK)
    conv_args = (p["conv_w"], p["conv_b"], p["conv_ln_g"], p["conv_ln_b"])
    if hist is None:
        assert p["w_in"].dtype == BF16, "the group with a single token tile goes first and rounds w_in"
        z, b_in, new_hist = _inproj_conv(x, seq, p["attn_norm_g"], p["w_in"], *conv_args, tm=1024)
        cols = Z_COLS_NO_GLU
    else:
        z, w_in16 = _norm_matmul(x, p["attn_norm_g"], p["w_in"], tm=1024, tn=512, emit_w16=True)
        p = {**p, "w_in": w_in16}
        cols = Z_COLS_ALL
        b_in, new_hist = _conv_sample(z, hist, seq, *conv_args, n_seq=16)
    w_tiled, bias_rows = _sgu_operands(p["sgu_w"], p["sgu_b"], chunk)
    sgu_out = _sgu(z, cols, p["sgu_ln_g"], p["sgu_ln_b"], w_tiled, bias_rows, chunk=chunk, rows=512,
                   want_vn=want_vn)
    if hist is None:
        steps = seq // PROMPT_ATTN_ROWS
        kv_specs = [pl.BlockSpec((1, N_MEM, D_C), functools.partial(lambda i, col: (i // steps, 0, col), col=col))
                    for col in (0, 1)]
        c_in = _attn(z, cols, k, v, kv_specs, rows=PROMPT_ATTN_ROWS, n_seq=1)
    else:
        kv_specs = [pl.BlockSpec((SAMPLE_ATTN_SEQS, N_MEM * HEAD_ROW_PITCH, LANES), lambda i: (i, 0, 0))] * 2
        c_in = _attn(z, cols, k, v, kv_specs, rows=seq, n_seq=SAMPLE_ATTN_SEQS)
    x1, h2 = _merge(sgu_out[0], b_in, c_in, z, cols, x, p["b_gate"], p["w_a_out"], p["w_b_out"], p["w_c_out"],
                    p["w_o"], p["mlp_norm_g"], tm=256)
    if p["w_up"].dtype == F32:
        d, w_up16, w_down16 = _mlp_first(h2, p["w_up"], p["w_down"], tm=512, tf=512)
        p = {**p, "w_up": w_up16, "w_down": w_down16}
        y = _residual_norm(x1, d, p["final_norm_g"], tm=256)
    else:
        y = _mlp(h2, x1, p["w_up"], p["w_down"], p["final_norm_g"], tm=512, tf=1024)
    return y, new_hist, (sgu_out[1] if want_vn else None), p


def kernel(x_prompt, x_sample, mem_prompt, cache_mem_k, cache_mem_v, state_conv, attn_norm_g, w_in, b_gate,
           sgu_ln_g, sgu_ln_b, sgu_w, sgu_b, w_a_out, conv_w, conv_b, conv_ln_g, conv_ln_b, w_b_out, mem_norm_g,
           w_mem_kv, w_c_out, w_o, mlp_norm_g, w_up, w_down, final_norm_g):
    assert attn_norm_g.shape[0] == 1, "single-layer trunk"
    bp, sp, _ = x_prompt.shape
    bs, ss, _ = x_sample.shape
    p = dict(
        attn_norm_g=attn_norm_g[0], w_in=w_in[0], b_gate=b_gate[0], sgu_ln_g=sgu_ln_g[0],
        sgu_ln_b=sgu_ln_b[0], sgu_w=sgu_w[0], sgu_b=sgu_b[0], w_a_out=w_a_out[0].astype(BF16),
        conv_w=conv_w[0], conv_b=conv_b[0], conv_ln_g=conv_ln_g[0], conv_ln_b=conv_ln_b[0],
        w_b_out=w_b_out[0].astype(BF16), w_c_out=w_c_out[0].astype(BF16), w_o=w_o[0].astype(BF16),
        mlp_norm_g=mlp_norm_g[0], w_up=w_up[0], w_down=w_down[0],
        final_norm_g=final_norm_g,
    )
    kv = _norm_matmul(mem_prompt.reshape(bp * N_MEM, D_MODEL), mem_norm_g[0], w_mem_kv[0],
                      tm=1024, tn=512).reshape(bp, N_MEM, 2 * D_C)
    y_s, hist_s, vn_s, p = _layer(x_sample.reshape(bs * ss, D_MODEL), ss, _split_lane_tiles(cache_mem_k[0]),
                                  _split_lane_tiles(cache_mem_v[0]), state_conv[0], p, want_vn=True)
    y_p, hist_p, _, _ = _layer(x_prompt.reshape(bp * sp, D_MODEL), sp, kv, kv, None, p, want_vn=False)
    head_shape = (1, bp, N_MEM, N_MEM_HEADS, MEM_HEAD_DIM)
    return (
        y_p.reshape(bp, sp, D_MODEL),
        y_s.reshape(bs, ss, D_MODEL),
        kv[:, :, :D_C].reshape(head_shape),
        kv[:, :, D_C:].reshape(head_shape),
        hist_p[None],
        hist_s[None],
        vn_s.reshape(1, bs, ss, D_A),
    )
```

```python
import functools
import math
from typing import NamedTuple

import jax
import jax.numpy as jnp
from jax import lax
from jax.experimental import pallas as pl
from jax.experimental.pallas import tpu as pltpu

F32 = jnp.float32
BF16 = jnp.bfloat16

D_MODEL = 2048
D_A = 1024
D_B = 1024
D_C = 1024
CHUNK = 128
SGU_GROUPS = 8
SGU_HEAD = D_A // SGU_GROUPS
CONV_WIDTH = 31
HIST = CONV_WIDTH - 1
N_MEM = 256
N_MEM_HEADS = 4
MEM_HEAD_DIM = D_C // N_MEM_HEADS
N_BRANCH = 3
D_FF = 4 * D_MODEL
N_IN = 2 * D_A + 2 * D_B + D_C + N_BRANCH * D_MODEL
EPS = 1e-6

COL_U, COL_V, COL_GLU_A, COL_GLU_B, COL_Q, COL_GATES = 0, 1, 2, 3, 4, 5
COL_W = 1024


class ZCols(NamedTuple):
    u: int
    v: int
    q: int
    gates: int


Z_COLS_ALL = ZCols(COL_U, COL_V, COL_Q, COL_GATES)
Z_COLS_NO_GLU = ZCols(COL_U, COL_V, COL_Q - 2, COL_GATES - 2)

V7X_VMEM_BYTES = 64 * 1024 * 1024
VMEM_LIMIT_BYTES = V7X_VMEM_BYTES - 8 * 1024 * 1024
SUBLANES = 8
LANES = 128
V7X_MXU_COLS = 256
PROJ_LANES = V7X_MXU_COLS
PROJ_ROWS = 256
HIST_PAD = 32


def _cparams(*semantics, flags=None):
    return pltpu.CompilerParams(dimension_semantics=semantics, vmem_limit_bytes=VMEM_LIMIT_BYTES, flags=flags)


def _not_before(value, anchor):
    tile = anchor[-SUBLANES:, -LANES:]
    zero = pltpu.bitcast((pltpu.bitcast(tile, jnp.uint32) >> 16) >> 16, F32)
    reps = (value.shape[0] // SUBLANES, value.shape[1] // LANES)
    return value + jnp.tile(zero, reps)


def _rms_norm(x, g):
    ms = jnp.mean(x * x, axis=-1, keepdims=True)
    return x * lax.rsqrt(ms + EPS) * g


def _layer_norm(x, g, b):
    mu = jnp.mean(x, axis=-1, keepdims=True)
    xc = x - mu
    var = jnp.mean(xc * xc, axis=-1, keepdims=True)
    return xc * lax.rsqrt(var + EPS) * g + b


NORM_ROWS = 128


def _norm_matmul_kernel(x_ref, g_ref, w_ref, o_ref, *rest):
    *w16_refs, h_ref = rest

    @pl.when(pl.program_id(1) == 0)
    def _():
        def body(r, carry):
            rows = pl.ds(pl.multiple_of(r * NORM_ROWS, NORM_ROWS), NORM_ROWS)
            h_ref[rows, :] = _rms_norm(x_ref[rows, :], g_ref[...]).astype(BF16)
            return carry

        lax.fori_loop(0, x_ref.shape[0] // NORM_ROWS, body, 0)

    for c in range(0, w_ref.shape[1], PROJ_LANES):
        w16 = w_ref[:, c : c + PROJ_LANES].astype(BF16)
        if w16_refs:
            w16_refs[0][:, c : c + PROJ_LANES] = w16
        o_ref[:, c : c + PROJ_LANES] = jnp.dot(h_ref[...], w16, preferred_element_type=F32)


def _norm_matmul(x, g, w, *, tm, tn, emit_w16=False):
    m, k = x.shape
    n = w.shape[1]
    out_shape = [jax.ShapeDtypeStruct((m, n), F32)]
    out_specs = [pl.BlockSpec((tm, tn), lambda i, j: (i, j))]
    if emit_w16:
        assert m == tm, "every weight block must be visited exactly once"
        out_shape.append(jax.ShapeDtypeStruct((k, n), BF16))
        out_specs.append(pl.BlockSpec((k, tn), lambda i, j: (0, j)))
    outs = pl.pallas_call(
        _norm_matmul_kernel,
        out_shape=out_shape,
        grid=(m // tm, n // tn),
        in_specs=[
            pl.BlockSpec((tm, k), lambda i, j: (i, 0)),
            pl.BlockSpec((1, k), lambda i, j: (0, 0)),
            pl.BlockSpec((k, tn), lambda i, j: (0, j)),
        ],
        out_specs=out_specs,
        scratch_shapes=[pltpu.VMEM((tm, k), BF16)],
        compiler_params=_cparams("arbitrary", "arbitrary"),
        name="norm_matmul",
    )(x, g.reshape(1, k), w)
    return outs if emit_w16 else outs[0]


def _sgu_kernel(u_ref, v_ref, g_ref, b_ref, w_ref, bias_ref, a_ref, *vn_refs, chunk):
    r = lax.broadcasted_iota(jnp.int32, (CHUNK, CHUNK), 0)
    c = lax.broadcasted_iota(jnp.int32, (CHUNK, CHUNK), 1)
    seg_bits = chunk.bit_length() - 1
    same_segment = (r >> seg_bits) == (c >> seg_bits)
    mask = jnp.logical_and(same_segment, (r & (chunk - 1)) >= (c & (chunk - 1)))
    w_s = [jnp.where(mask, w_ref[grp], 0.0).astype(BF16) for grp in range(SGU_GROUPS)]
    for blk in range(u_ref.shape[0] // CHUNK):
        rows = slice(blk * CHUNK, (blk + 1) * CHUNK)
        vn = _layer_norm(jax.nn.gelu(v_ref[rows, :]), g_ref[...], b_ref[...])
        if vn_refs:
            vn_refs[0][rows, :] = vn
        vn16 = vn.astype(BF16)
        for grp in range(SGU_GROUPS):
            cols = slice(grp * SGU_HEAD, (grp + 1) * SGU_HEAD)
            mixed = jnp.dot(w_s[grp], vn16[:, cols], preferred_element_type=F32) + bias_ref[:, cols]
            a_ref[rows, cols] = (jax.nn.gelu(u_ref[rows, cols]) * mixed).astype(BF16)


def _sgu(z, cols, ln_g, ln_b, w_tiled, bias_rows, *, chunk, rows, want_vn):
    m = z.shape[0]
    out_shape = [jax.ShapeDtypeStruct((m, D_A), BF16)]
    out_specs = [pl.BlockSpec((rows, D_A), lambda i: (i, 0))]
    if want_vn:
        out_shape.append(jax.ShapeDtypeStruct((m, D_A), F32))
        out_specs.append(pl.BlockSpec((rows, D_A), lambda i: (i, 0)))
    return pl.pallas_call(
        functools.partial(_sgu_kernel, chunk=chunk),
        out_shape=out_shape,
        grid=(m // rows,),
        in_specs=[
            pl.BlockSpec((rows, COL_W), lambda i: (i, cols.u)),
            pl.BlockSpec((rows, COL_W), lambda i: (i, cols.v)),
            pl.BlockSpec((1, D_A), lambda i: (0, 0)),
            pl.BlockSpec((1, D_A), lambda i: (0, 0)),
            pl.BlockSpec((SGU_GROUPS, CHUNK, CHUNK), lambda i: (0, 0, 0)),
            pl.BlockSpec((CHUNK, D_A), lambda i: (0, 0)),
        ],
        out_specs=out_specs,
        compiler_params=_cparams("arbitrary"),
        name="sgu",
    )(z, z, ln_g.reshape(1, D_A), ln_b.reshape(1, D_A), w_tiled, bias_rows)


CONV_ROWS = 32
CONV_LANES = 128
CONV_LEAD = HIST_PAD - HIST


def _conv_window(win, w_ref, lanes, n_rows):
    out = None
    for b in range(SUBLANES):
        rows_b = n_rows if b == 0 else n_rows + SUBLANES
        y = None
        for a in range(HIST_PAD // SUBLANES + 1):
            k = SUBLANES * a + b - CONV_LEAD
            if 0 <= k < CONV_WIDTH:
                term = w_ref[k : k + 1, lanes] * win[SUBLANES * a : SUBLANES * a + rows_b, :]
                y = term if y is None else y + term
        shifted = y[b : b + n_rows, :]
        out = shifted if out is None else out + shifted
    return out


def _ln_silu_rows(dc_ref, cb_ref, g_ref, b_ref, o_ref, rows):
    dc = dc_ref[rows, :] + cb_ref[...]
    o_ref[rows, :] = jax.nn.silu(_layer_norm(dc, g_ref[...], b_ref[...])).astype(BF16)


CONV_STEP_ROWS = 128
N_GLU_STEPS = 2
N_COLS = N_IN // COL_W


def _glu_first(j):
    return jnp.where(j < N_GLU_STEPS, j + COL_GLU_A, jnp.where(j < COL_GLU_A + N_GLU_STEPS, j - N_GLU_STEPS, j))


def _inproj_conv_kernel(x_ref, g_ref, w_ref, cw_ref, cb_ref, lg_ref, lb_ref, z_ref, bin_ref, hist_ref,
                        h_ref, ext_ref, *, tiles_per_seq):
    i = pl.program_id(0)
    j = pl.program_id(1)
    t = x_ref.shape[0]

    def proj():
        return jnp.dot(h_ref[...], w_ref[...], preferred_element_type=F32)

    @pl.when(j == 0)
    def _():
        @pl.when(i % tiles_per_seq == 0)
        def _():
            ext_ref[0:HIST_PAD, :] = jnp.zeros((HIST_PAD, D_B), F32)

        @pl.when(i % tiles_per_seq != 0)
        def _():
            ext_ref[0:HIST_PAD, :] = ext_ref[t : t + HIST_PAD, :]

        for rb in range(t // PROJ_ROWS):
            for r in range(rb * PROJ_ROWS, (rb + 1) * PROJ_ROWS, NORM_ROWS):
                h_ref[r : r + NORM_ROWS, :] = _rms_norm(x_ref[r : r + NORM_ROWS, :], g_ref[...]).astype(BF16)
            rows = slice(rb * PROJ_ROWS, (rb + 1) * PROJ_ROWS)
            ext_ref[HIST_PAD + rb * PROJ_ROWS : HIST_PAD + (rb + 1) * PROJ_ROWS, :] = jnp.dot(
                h_ref[rows, :], w_ref[...], preferred_element_type=F32)

    @pl.when(j == 1)
    def _():
        ext_ref[HIST_PAD : HIST_PAD + t, :] = ext_ref[HIST_PAD : HIST_PAD + t, :] * jax.nn.sigmoid(proj())
        hist_ref[0] = ext_ref[t + CONV_LEAD : t + HIST_PAD, :]

    @pl.when(j >= N_GLU_STEPS)
    def _():
        chunk = jnp.minimum(j - N_GLU_STEPS, t // CONV_STEP_ROWS - 1)
        r0 = pl.multiple_of(chunk * CONV_STEP_ROWS, CONV_STEP_ROWS)
        n_row_blocks = CONV_STEP_ROWS // CONV_ROWS
        n_lane_blocks = D_B // PROJ_LANES
        rows_per_block = t // n_row_blocks
        conv_done, proj_done = [], []
        for rc in range(n_row_blocks):
            rows = slice(rc * rows_per_block, (rc + 1) * rows_per_block)
            parts = []
            for lb in range(n_lane_blocks):
                lanes = slice(lb * PROJ_LANES, (lb + 1) * PROJ_LANES)
                wins = []
                for cl in range(lb * PROJ_LANES, (lb + 1) * PROJ_LANES, CONV_LANES):
                    clanes = slice(cl, cl + CONV_LANES)
                    win = ext_ref[pl.ds(r0 + rc * CONV_ROWS, CONV_ROWS + HIST_PAD), clanes]
                    if proj_done:
                        win = _not_before(win, proj_done[-1])
                    wins.append(_conv_window(win, cw_ref, clanes, CONV_ROWS))
                piece = jnp.concatenate(wins, axis=-1)
                zp = jnp.dot(h_ref[rows, :], w_ref[:, lanes], preferred_element_type=F32)
                if conv_done:
                    zp = _not_before(zp, conv_done[-1])
                z_ref[rows, lanes] = zp
                conv_done.append(piece)
                proj_done.append(zp)
                parts.append(piece)
            dc = jnp.concatenate(parts, axis=-1) + cb_ref[...]
            y = jax.nn.silu(_layer_norm(dc, lg_ref[...], lb_ref[...]))
            bin_ref[pl.ds(r0 + rc * CONV_ROWS, CONV_ROWS), :] = y.astype(BF16)


def _inproj_conv(x, seq, g, w16, conv_w, conv_b, ln_g, ln_b, *, tm):
    m, k = x.shape
    tiles_per_seq = seq // tm
    n_conv_steps = N_COLS - N_GLU_STEPS
    assert tm // CONV_STEP_ROWS <= n_conv_steps, "not enough grid steps to convolve the whole tile"
    const = lambda i, j: (0, 0)
    return pl.pallas_call(
        functools.partial(_inproj_conv_kernel, tiles_per_seq=tiles_per_seq),
        out_shape=[
            jax.ShapeDtypeStruct((m, n_conv_steps * COL_W), F32),
            jax.ShapeDtypeStruct((m, D_B), BF16),
            jax.ShapeDtypeStruct((m // seq, HIST, D_B), F32),
        ],
        grid=(m // tm, N_COLS),
        in_specs=[
            pl.BlockSpec((tm, k), lambda i, j: (i, 0)),
            pl.BlockSpec((1, k), const),
            pl.BlockSpec((k, COL_W), lambda i, j: (0, _glu_first(j))),
            pl.BlockSpec((CONV_WIDTH, D_B), const),
            pl.BlockSpec((1, D_B), const),
            pl.BlockSpec((1, D_B), const),
            pl.BlockSpec((1, D_B), const),
        ],
        out_specs=[
            pl.BlockSpec((tm, COL_W), lambda i, j: (i, jnp.maximum(j - N_GLU_STEPS, 0))),
            pl.BlockSpec((tm, D_B), lambda i, j: (i, 0)),
            pl.BlockSpec((1, HIST, D_B), lambda i, j: (i // tiles_per_seq, 0, 0)),
        ],
        scratch_shapes=[pltpu.VMEM((tm, k), BF16), pltpu.VMEM((tm + HIST_PAD, D_B), F32)],
        compiler_params=_cparams("arbitrary", "arbitrary", ),
        name="inproj_conv",
    )(x, g.reshape(1, k), w16, conv_w, conv_b.reshape(1, D_B), ln_g.reshape(1, D_B), ln_b.reshape(1, D_B))


def _conv_sample_kernel(ga_ref, gb_ref, hist_ref, w_ref, cb_ref, g_ref, b_ref, o_ref, nh_ref, ext_ref, dc_ref,
                        *, seq):
    n_seq = hist_ref.shape[0]
    c = ga_ref[...] * jax.nn.sigmoid(gb_ref[...])
    for s in range(n_seq):
        ext_ref[s, 0:SUBLANES, :] = jnp.zeros((SUBLANES, D_B), F32)
        ext_ref[s, CONV_LEAD:HIST_PAD, :] = hist_ref[s]
        ext_ref[s, HIST_PAD : HIST_PAD + seq, :] = c[s * seq : (s + 1) * seq, :]
    for s in range(n_seq):
        for lb in range(D_B // CONV_LANES):
            lanes = slice(lb * CONV_LANES, (lb + 1) * CONV_LANES)
            dc_ref[s * seq : (s + 1) * seq, lanes] = _conv_window(ext_ref[s, :, lanes], w_ref, lanes, seq)
        nh_ref[s] = ext_ref[s, seq + CONV_LEAD : seq + HIST_PAD, :]
    _ln_silu_rows(dc_ref, cb_ref, g_ref, b_ref, o_ref, slice(None))


def _conv_sample(z, hist, seq, conv_w, conv_b, ln_g, ln_b, *, n_seq):
    batch = hist.shape[0]
    return pl.pallas_call(
        functools.partial(_conv_sample_kernel, seq=seq),
        out_shape=[
            jax.ShapeDtypeStruct((batch * seq, D_B), BF16),
            jax.ShapeDtypeStruct((batch, HIST, D_B), F32),
        ],
        grid=(batch // n_seq,),
        in_specs=[
            pl.BlockSpec((n_seq * seq, COL_W), lambda i: (i, COL_GLU_A)),
            pl.BlockSpec((n_seq * seq, COL_W), lambda i: (i, COL_GLU_B)),
            pl.BlockSpec((n_seq, HIST, D_B), lambda i: (i, 0, 0)),
            pl.BlockSpec((CONV_WIDTH, D_B), lambda i: (0, 0)),
            pl.BlockSpec((1, D_B), lambda i: (0, 0)),
            pl.BlockSpec((1, D_B), lambda i: (0, 0)),
            pl.BlockSpec((1, D_B), lambda i: (0, 0)),
        ],
        out_specs=[
            pl.BlockSpec((n_seq * seq, D_B), lambda i: (i, 0)),
            pl.BlockSpec((n_seq, HIST, D_B), lambda i: (i, 0, 0)),
        ],
        scratch_shapes=[pltpu.VMEM((n_seq, HIST_PAD + seq, D_B), F32), pltpu.VMEM((n_seq * seq, D_B), F32)],
        compiler_params=_cparams("arbitrary"),
        name="conv_sample",
    )(z, z, hist, conv_w, conv_b.reshape(1, D_B), ln_g.reshape(1, D_B), ln_b.reshape(1, D_B))


def _head_cols(h):
    return slice(h * MEM_HEAD_DIM, (h + 1) * MEM_HEAD_DIM)


HEAD_LANE_TILES = MEM_HEAD_DIM // LANES
HEAD_ROW_PITCH = N_MEM_HEADS * HEAD_LANE_TILES


def _split_lane_tiles(kv):
    b = kv.shape[0]
    kv = kv.reshape(b, N_MEM, N_MEM_HEADS, HEAD_LANE_TILES, LANES)
    return kv.transpose(0, 1, 3, 2, 4).reshape(b, N_MEM * HEAD_ROW_PITCH, LANES)


def _head_of(kv_ref, s, h):
    if kv_ref.shape[-1] == LANES:
        tiles = [kv_ref[s, pl.ds(j * N_MEM_HEADS + h, N_MEM, stride=HEAD_ROW_PITCH), :]
                 for j in range(HEAD_LANE_TILES)]
        return jnp.concatenate(tiles, axis=-1).astype(BF16)
    return kv_ref[s, :, _head_cols(h)].astype(BF16)


def _attn_kernel(q_ref, k_ref, v_ref, o_ref, *, rows):
    scale = 1.0 / math.sqrt(MEM_HEAD_DIM)
    pairs = [(s, h) for s in range(k_ref.shape[0]) for h in range(N_MEM_HEADS)]
    scores = []
    for s, h in pairs:
        q = (q_ref[s * rows : (s + 1) * rows, _head_cols(h)] * scale).astype(BF16)
        scores.append(lax.dot_general(q, _head_of(k_ref, s, h), (((1,), (1,)), ((), ())),
                                      preferred_element_type=F32))
    sc = jnp.concatenate(scores, axis=0)
    p = jnp.exp(sc - jnp.max(sc, axis=-1, keepdims=True))
    p = p / jnp.sum(p, axis=-1, keepdims=True)
    for n, (s, h) in enumerate(pairs):
        ph = p[n * rows : (n + 1) * rows, :].astype(BF16)
        o = jnp.dot(ph, _head_of(v_ref, s, h), preferred_element_type=F32)
        o_ref[s * rows : (s + 1) * rows, _head_cols(h)] = o.astype(BF16)


def _attn(z, cols, k, v, kv_specs, *, rows, n_seq):
    m = z.shape[0]
    return pl.pallas_call(
        functools.partial(_attn_kernel, rows=rows),
        out_shape=jax.ShapeDtypeStruct((m, D_C), BF16),
        grid=(m // (rows * n_seq),),
        in_specs=[pl.BlockSpec((rows * n_seq, COL_W), lambda i: (i, cols.q))] + kv_specs,
        out_specs=pl.BlockSpec((rows * n_seq, D_C), lambda i: (i, 0)),
        compiler_params=_cparams("arbitrary"),
        name="attn",
    )(z, k, v)


def _merge_kernel(a_ref, b_ref, c_ref, *refs):
    gate_refs = refs[: 2 * N_BRANCH]
    bg_ref, x_ref, wa_ref, wb_ref, wc_ref, wo_ref, mg_ref, x1_ref, h2_ref = refs[2 * N_BRANCH :]
    halves = []
    for half in range(2):
        cols = slice(half * COL_W, (half + 1) * COL_W)
        merged = None
        for br, (in_ref, w_ref) in enumerate(((a_ref, wa_ref), (b_ref, wb_ref), (c_ref, wc_ref))):
            y = jnp.dot(in_ref[...], w_ref[:, cols], preferred_element_type=F32)
            gcols = slice((2 * br + half) * COL_W, (2 * br + half + 1) * COL_W)
            term = jax.nn.sigmoid(gate_refs[2 * br + half][...] + bg_ref[:, gcols]) * y
            merged = term if merged is None else merged + term
        halves.append(merged.astype(BF16))
    merged16 = jnp.concatenate(halves, axis=-1)
    x1 = x_ref[...] + jnp.dot(merged16, wo_ref[...], preferred_element_type=F32)
    x1_ref[...] = x1
    h2_ref[...] = _rms_norm(x1, mg_ref[...]).astype(BF16)


def _merge(a_in, b_in, c_in, z, cols, x, b_gate, w_a, w_b, w_c, w_o, mlp_g, *, tm):
    m = x.shape[0]
    const = lambda i: (0, 0)
    resident = functools.partial(pl.BlockSpec, index_map=const, pipeline_mode=pl.Buffered(1))
    gate_specs = [
        pl.BlockSpec((tm, COL_W), functools.partial(lambda i, col: (i, col), col=cols.gates + n))
        for n in range(2 * N_BRANCH)
    ]
    return pl.pallas_call(
        _merge_kernel,
        out_shape=[jax.ShapeDtypeStruct((m, D_MODEL), F32), jax.ShapeDtypeStruct((m, D_MODEL), BF16)],
        grid=(m // tm,),
        in_specs=[pl.BlockSpec((tm, D_A), lambda i: (i, 0))] * 3
        + gate_specs
        + [
            pl.BlockSpec((1, N_BRANCH * D_MODEL), const),
            pl.BlockSpec((tm, D_MODEL), lambda i: (i, 0)),
            resident((D_A, D_MODEL)),
            resident((D_B, D_MODEL)),
            resident((D_C, D_MODEL)),
            resident((D_MODEL, D_MODEL)),
            pl.BlockSpec((1, D_MODEL), const),
        ],
        out_specs=[pl.BlockSpec((tm, D_MODEL), lambda i: (i, 0))] * 2,
        compiler_params=_cparams("arbitrary"),
        name="merge",
    )(a_in, b_in, c_in, *([z] * (2 * N_BRANCH)), b_gate.reshape(1, -1), x, w_a, w_b, w_c, w_o,
      mlp_g.reshape(1, D_MODEL))


def _mlp_kernel(h2_ref, x1_ref, wu_ref, wd_ref, fg_ref, y_ref, acc_ref):
    f = pl.program_id(1)

    @pl.when(f == 0)
    def _():
        acc_ref[...] = x1_ref[...]

    t = jnp.square(jnp.maximum(jnp.dot(h2_ref[...], wu_ref[...], preferred_element_type=F32), 0.0))
    acc_ref[...] += jnp.dot(t.astype(BF16), wd_ref[...], preferred_element_type=F32)

    @pl.when(f == pl.num_programs(1) - 1)
    def _():
        y_ref[...] = _rms_norm(acc_ref[...], fg_ref[...])


def _mlp(h2, x1, w_up, w_down, final_g, *, tm, tf):
    m = h2.shape[0]
    return pl.pallas_call(
        _mlp_kernel,
        out_shape=jax.ShapeDtypeStruct((m, D_MODEL), F32),
        grid=(m // tm, D_FF // tf),
        in_specs=[
            pl.BlockSpec((tm, D_MODEL), lambda i, f: (i, 0)),
            pl.BlockSpec((tm, D_MODEL), lambda i, f: (i, 0)),
            pl.BlockSpec((D_MODEL, tf), lambda i, f: (0, f)),
            pl.BlockSpec((tf, D_MODEL), lambda i, f: (f, 0)),
            pl.BlockSpec((1, D_MODEL), lambda i, f: (0, 0)),
        ],
        out_specs=pl.BlockSpec((tm, D_MODEL), lambda i, f: (i, 0)),
        scratch_shapes=[pltpu.VMEM((tm, D_MODEL), F32)],
        compiler_params=_cparams("arbitrary", "arbitrary"),
        name="mlp",
    )(h2, x1, w_up, w_down, final_g.reshape(1, D_MODEL))


def _mlp_first_kernel(h2_ref, wu_ref, wd_ref, acc_ref, wu16_ref, wd16_ref):
    @pl.when(pl.program_id(0) == 0)
    def _():
        acc_ref[...] = jnp.zeros(acc_ref.shape, F32)

    h2 = h2_ref[...]
    t_pieces = []
    for c in range(0, wu_ref.shape[1], PROJ_LANES):
        wu = wu_ref[:, c : c + PROJ_LANES].astype(BF16)
        wu16_ref[:, c : c + PROJ_LANES] = wu
        t = jnp.square(jnp.maximum(jnp.dot(h2, wu, preferred_element_type=F32), 0.0))
        t_pieces.append(t.astype(BF16))
    t16 = jnp.concatenate(t_pieces, axis=-1)
    for c in range(0, wd_ref.shape[1], PROJ_LANES):
        wd = wd_ref[:, c : c + PROJ_LANES].astype(BF16)
        wd16_ref[:, c : c + PROJ_LANES] = wd
        acc_ref[:, c : c + PROJ_LANES] += jnp.dot(t16, wd, preferred_element_type=F32)


def _mlp_first(h2, w_up, w_down, *, tf):
    m = h2.shape[0]
    resident = pl.BlockSpec((m, D_MODEL), lambda f: (0, 0))
    return pl.pallas_call(
        _mlp_first_kernel,
        out_shape=[
            jax.ShapeDtypeStruct((m, D_MODEL), F32),
            jax.ShapeDtypeStruct((D_MODEL, D_FF), BF16),
            jax.ShapeDtypeStruct((D_FF, D_MODEL), BF16),
        ],
        grid=(D_FF // tf,),
        in_specs=[
            resident,
            pl.BlockSpec((D_MODEL, tf), lambda f: (0, f)),
            pl.BlockSpec((tf, D_MODEL), lambda f: (f, 0)),
        ],
        out_specs=[
            resident,
            pl.BlockSpec((D_MODEL, tf), lambda f: (0, f)),
            pl.BlockSpec((tf, D_MODEL), lambda f: (f, 0)),
        ],
        compiler_params=_cparams("arbitrary"),
        name="mlp_first",
    )(h2, w_up, w_down)


def _residual_norm_kernel(x_ref, d_ref, g_ref, y_ref):
    y_ref[...] = _rms_norm(x_ref[...] + d_ref[...], g_ref[...])


def _residual_norm(x, d, g, *, tm):
    m = x.shape[0]
    row_spec = pl.BlockSpec((tm, D_MODEL), lambda i: (i, 0))
    return pl.pallas_call(
        _residual_norm_kernel,
        out_shape=jax.ShapeDtypeStruct((m, D_MODEL), F32),
        grid=(m // tm,),
        in_specs=[row_spec, row_spec, pl.BlockSpec((1, D_MODEL), lambda i: (0, 0))],
        out_specs=row_spec,
        compiler_params=_cparams("arbitrary"),
        name="residual_norm",
    )(x, d, g.reshape(1, D_MODEL))


def _sgu_operands(sgu_w, sgu_b, chunk):
    reps = CHUNK // chunk
    w_tiled = jnp.tile(sgu_w[:, :chunk, :chunk], (1, reps, reps))
    bias_rows = jnp.repeat(jnp.tile(sgu_b[:, :chunk].T, (reps, 1)), SGU_HEAD, axis=1)
    return w_tiled, bias_rows


PROMPT_ATTN_ROWS = 512
SAMPLE_ATTN_SEQS = 8


def _layer(x, seq, k, v, hist, p, *, want_vn):
    chunk = min(seq, CHUNK)
    conv_args = (p["conv_w"], p["conv_b"], p["conv_ln_g"], p["conv_ln_b"])
    if hist is None:
        assert p["w_in"].dtype == BF16, "the group with a single token tile goes first and rounds w_in"
        z, b_in, new_hist = _inproj_conv(x, seq, p["attn_norm_g"], p["w_in"], *conv_args, tm=1024)
        cols = Z_COLS_NO_GLU
    else:
        z, w_in16 = _norm_matmul(x, p["attn_norm_g"], p["w_in"], tm=1024, tn=512, emit_w16=True)
        p = {**p, "w_in": w_in16}
        cols = Z_COLS_ALL
        b_in, new_hist = _conv_sample(z, hist, seq, *conv_args, n_seq=16)
    w_tiled, bias_rows = _sgu_operands(p["sgu_w"], p["sgu_b"], chunk)
    sgu_out = _sgu(z, cols, p["sgu_ln_g"], p["sgu_ln_b"], w_tiled, bias_rows, chunk=chunk, rows=512,
                   want_vn=want_vn)
    if hist is None:
        steps = seq // PROMPT_ATTN_ROWS
        kv_specs = [pl.BlockSpec((1, N_MEM, D_C), functools.partial(lambda i, col: (i // steps, 0, col), col=col))
                    for col in (0, 1)]
        c_in = _attn(z, cols, k, v, kv_specs, rows=PROMPT_ATTN_ROWS, n_seq=1)
    else:
        kv_specs = [pl.BlockSpec((SAMPLE_ATTN_SEQS, N_MEM * HEAD_ROW_PITCH, LANES), lambda i: (i, 0, 0))] * 2
        c_in = _attn(z, cols, k, v, kv_specs, rows=seq, n_seq=SAMPLE_ATTN_SEQS)
    x1, h2 = _merge(sgu_out[0], b_in, c_in, z, cols, x, p["b_gate"], p["w_a_out"], p["w_b_out"], p["w_c_out"],
                    p["w_o"], p["mlp_norm_g"], tm=256)
    if p["w_up"].dtype == F32:
        d, w_up16, w_down16 = _mlp_first(h2, p["w_up"], p["w_down"], tf=512)
        p = {**p, "w_up": w_up16, "w_down": w_down16}
        y = _residual_norm(x1, d, p["final_norm_g"], tm=256)
    else:
        y = _mlp(h2, x1, p["w_up"], p["w_down"], p["final_norm_g"], tm=512, tf=1024)
    return y, new_hist, (sgu_out[1] if want_vn else None), p


def kernel(x_prompt, x_sample, mem_prompt, cache_mem_k, cache_mem_v, state_conv, attn_norm_g, w_in, b_gate,
           sgu_ln_g, sgu_ln_b, sgu_w, sgu_b, w_a_out, conv_w, conv_b, conv_ln_g, conv_ln_b, w_b_out, mem_norm_g,
           w_mem_kv, w_c_out, w_o, mlp_norm_g, w_up, w_down, final_norm_g):
    assert attn_norm_g.shape[0] == 1, "single-layer trunk"
    bp, sp, _ = x_prompt.shape
    bs, ss, _ = x_sample.shape
    p = dict(
        attn_norm_g=attn_norm_g[0], w_in=w_in[0], b_gate=b_gate[0], sgu_ln_g=sgu_ln_g[0],
        sgu_ln_b=sgu_ln_b[0], sgu_w=sgu_w[0], sgu_b=sgu_b[0], w_a_out=w_a_out[0].astype(BF16),
        conv_w=conv_w[0], conv_b=conv_b[0], conv_ln_g=conv_ln_g[0], conv_ln_b=conv_ln_b[0],
        w_b_out=w_b_out[0].astype(BF16), w_c_out=w_c_out[0].astype(BF16), w_o=w_o[0].astype(BF16),
        mlp_norm_g=mlp_norm_g[0], w_up=w_up[0], w_down=w_down[0],
        final_norm_g=final_norm_g,
    )
    kv = _norm_matmul(mem_prompt.reshape(bp * N_MEM, D_MODEL), mem_norm_g[0], w_mem_kv[0],
                      tm=1024, tn=512).reshape(bp, N_MEM, 2 * D_C)
    y_s, hist_s, vn_s, p = _layer(x_sample.reshape(bs * ss, D_MODEL), ss, _split_lane_tiles(cache_mem_k[0]),
                                  _split_lane_tiles(cache_mem_v[0]), state_conv[0], p, want_vn=True)
    y_p, hist_p, _, _ = _layer(x_prompt.reshape(bp * sp, D_MODEL), sp, kv, kv, None, p, want_vn=False)
    head_shape = (1, bp, N_MEM, N_MEM_HEADS, MEM_HEAD_DIM)
    return (
        y_p.reshape(bp, sp, D_MODEL),
        y_s.reshape(bs, ss, D_MODEL),
        kv[:, :, :D_C].reshape(head_shape),
        kv[:, :, D_C:].reshape(head_shape),
        hist_p[None],
        hist_s[None],
        vn_s.reshape(1, bs, ss, D_A),
    )
```

```python
import functools
import math
from typing import NamedTuple

import jax
import jax.numpy as jnp
from jax import lax
from jax.experimental import pallas as pl
from jax.experimental.pallas import tpu as pltpu

F32 = jnp.float32
BF16 = jnp.bfloat16

D_MODEL = 2048
D_A = 1024
D_B = 1024
D_C = 1024
CHUNK = 128
SGU_GROUPS = 8
SGU_HEAD = D_A // SGU_GROUPS
CONV_WIDTH = 31
HIST = CONV_WIDTH - 1
N_MEM = 256
N_MEM_HEADS = 4
MEM_HEAD_DIM = D_C // N_MEM_HEADS
N_BRANCH = 3
D_FF = 4 * D_MODEL
N_IN = 2 * D_A + 2 * D_B + D_C + N_BRANCH * D_MODEL
EPS = 1e-6

COL_U, COL_V, COL_GLU_A, COL_GLU_B, COL_Q, COL_GATES = 0, 1, 2, 3, 4, 5
COL_W = 1024


class ZCols(NamedTuple):
    u: int
    v: int
    q: int
    gates: int


Z_COLS_ALL = ZCols(COL_U, COL_V, COL_Q, COL_GATES)
Z_COLS_NO_GLU = ZCols(COL_U, COL_V, COL_Q - 2, COL_GATES - 2)

V7X_VMEM_BYTES = 64 * 1024 * 1024
VMEM_LIMIT_BYTES = V7X_VMEM_BYTES - 8 * 1024 * 1024
SUBLANES = 8
LANES = 128
V7X_MXU_COLS = 256
PROJ_LANES = V7X_MXU_COLS
PROJ_ROWS = 256
HIST_PAD = 32


def _cparams(*semantics, flags=None):
    return pltpu.CompilerParams(dimension_semantics=semantics, vmem_limit_bytes=VMEM_LIMIT_BYTES, flags=flags)


def _not_before(value, anchor):
    tile = anchor[-SUBLANES:, -LANES:]
    zero = pltpu.bitcast((pltpu.bitcast(tile, jnp.uint32) >> 16) >> 16, F32)
    reps = (value.shape[0] // SUBLANES, value.shape[1] // LANES)
    return value + jnp.tile(zero, reps)


def _rms_norm(x, g):
    ms = jnp.mean(x * x, axis=-1, keepdims=True)
    return x * lax.rsqrt(ms + EPS) * g


def _layer_norm(x, g, b):
    mu = jnp.mean(x, axis=-1, keepdims=True)
    xc = x - mu
    var = jnp.mean(xc * xc, axis=-1, keepdims=True)
    return xc * lax.rsqrt(var + EPS) * g + b


NORM_ROWS = 128


def _norm_matmul_kernel(x_ref, g_ref, w_ref, o_ref, *rest):
    *w16_refs, h_ref = rest

    @pl.when(pl.program_id(1) == 0)
    def _():
        def body(r, carry):
            rows = pl.ds(pl.multiple_of(r * NORM_ROWS, NORM_ROWS), NORM_ROWS)
            h_ref[rows, :] = _rms_norm(x_ref[rows, :], g_ref[...]).astype(BF16)
            return carry

        lax.fori_loop(0, x_ref.shape[0] // NORM_ROWS, body, 0)

    for c in range(0, w_ref.shape[1], PROJ_LANES):
        w16 = w_ref[:, c : c + PROJ_LANES].astype(BF16)
        if w16_refs:
            w16_refs[0][:, c : c + PROJ_LANES] = w16
        o_ref[:, c : c + PROJ_LANES] = jnp.dot(h_ref[...], w16, preferred_element_type=F32)


def _norm_matmul(x, g, w, *, tm, tn, emit_w16=False):
    m, k = x.shape
    n = w.shape[1]
    out_shape = [jax.ShapeDtypeStruct((m, n), F32)]
    out_specs = [pl.BlockSpec((tm, tn), lambda i, j: (i, j))]
    if emit_w16:
        assert m == tm, "every weight block must be visited exactly once"
        out_shape.append(jax.ShapeDtypeStruct((k, n), BF16))
        out_specs.append(pl.BlockSpec((k, tn), lambda i, j: (0, j)))
    outs = pl.pallas_call(
        _norm_matmul_kernel,
        out_shape=out_shape,
        grid=(m // tm, n // tn),
        in_specs=[
            pl.BlockSpec((tm, k), lambda i, j: (i, 0)),
            pl.BlockSpec((1, k), lambda i, j: (0, 0)),
            pl.BlockSpec((k, tn), lambda i, j: (0, j)),
        ],
        out_specs=out_specs,
        scratch_shapes=[pltpu.VMEM((tm, k), BF16)],
        compiler_params=_cparams("arbitrary", "arbitrary"),
        name="norm_matmul",
    )(x, g.reshape(1, k), w)
    return outs if emit_w16 else outs[0]


def _sgu_kernel(u_ref, v_ref, g_ref, b_ref, w_ref, bias_ref, a_ref, *vn_refs, chunk):
    r = lax.broadcasted_iota(jnp.int32, (CHUNK, CHUNK), 0)
    c = lax.broadcasted_iota(jnp.int32, (CHUNK, CHUNK), 1)
    seg_bits = chunk.bit_length() - 1
    same_segment = (r >> seg_bits) == (c >> seg_bits)
    mask = jnp.logical_and(same_segment, (r & (chunk - 1)) >= (c & (chunk - 1)))
    w_s = [jnp.where(mask, w_ref[grp], 0.0).astype(BF16) for grp in range(SGU_GROUPS)]
    for blk in range(u_ref.shape[0] // CHUNK):
        rows = slice(blk * CHUNK, (blk + 1) * CHUNK)
        vn = _layer_norm(jax.nn.gelu(v_ref[rows, :]), g_ref[...], b_ref[...])
        if vn_refs:
            vn_refs[0][rows, :] = vn
        vn16 = vn.astype(BF16)
        for grp in range(SGU_GROUPS):
            cols = slice(grp * SGU_HEAD, (grp + 1) * SGU_HEAD)
            mixed = jnp.dot(w_s[grp], vn16[:, cols], preferred_element_type=F32) + bias_ref[:, cols]
            a_ref[rows, cols] = (jax.nn.gelu(u_ref[rows, cols]) * mixed).astype(BF16)


def _sgu(z, cols, ln_g, ln_b, w_tiled, bias_rows, *, chunk, rows, want_vn):
    m = z.shape[0]
    out_shape = [jax.ShapeDtypeStruct((m, D_A), BF16)]
    out_specs = [pl.BlockSpec((rows, D_A), lambda i: (i, 0))]
    if want_vn:
        out_shape.append(jax.ShapeDtypeStruct((m, D_A), F32))
        out_specs.append(pl.BlockSpec((rows, D_A), lambda i: (i, 0)))
    return pl.pallas_call(
        functools.partial(_sgu_kernel, chunk=chunk),
        out_shape=out_shape,
        grid=(m // rows,),
        in_specs=[
            pl.BlockSpec((rows, COL_W), lambda i: (i, cols.u)),
            pl.BlockSpec((rows, COL_W), lambda i: (i, cols.v)),
            pl.BlockSpec((1, D_A), lambda i: (0, 0)),
            pl.BlockSpec((1, D_A), lambda i: (0, 0)),
            pl.BlockSpec((SGU_GROUPS, CHUNK, CHUNK), lambda i: (0, 0, 0)),
            pl.BlockSpec((CHUNK, D_A), lambda i: (0, 0)),
        ],
        out_specs=out_specs,
        compiler_params=_cparams("arbitrary"),
        name="sgu",
    )(z, z, ln_g.reshape(1, D_A), ln_b.reshape(1, D_A), w_tiled, bias_rows)


CONV_ROWS = 32
CONV_LANES = 128
CONV_LEAD = HIST_PAD - HIST


def _conv_window(win, w_ref, lanes, n_rows):
    out = None
    for b in range(SUBLANES):
        rows_b = n_rows if b == 0 else n_rows + SUBLANES
        y = None
        for a in range(HIST_PAD // SUBLANES + 1):
            k = SUBLANES * a + b - CONV_LEAD
            if 0 <= k < CONV_WIDTH:
                term = w_ref[k : k + 1, lanes] * win[SUBLANES * a : SUBLANES * a + rows_b, :]
                y = term if y is None else y + term
        shifted = y[b : b + n_rows, :]
        out = shifted if out is None else out + shifted
    return out


def _ln_silu_rows(dc_ref, cb_ref, g_ref, b_ref, o_ref, rows):
    dc = dc_ref[rows, :] + cb_ref[...]
    o_ref[rows, :] = jax.nn.silu(_layer_norm(dc, g_ref[...], b_ref[...])).astype(BF16)


CONV_STEP_ROWS = 128
CONV_LEAD_PIECES = 2
N_GLU_STEPS = 2
N_COLS = N_IN // COL_W


def _glu_first(j):
    return jnp.where(j < N_GLU_STEPS, j + COL_GLU_A, jnp.where(j < COL_GLU_A + N_GLU_STEPS, j - N_GLU_STEPS, j))


def _inproj_conv_kernel(x_ref, g_ref, w_ref, cw_ref, cb_ref, lg_ref, lb_ref, z_ref, bin_ref, hist_ref,
                        h_ref, ext_ref, *, tiles_per_seq):
    i = pl.program_id(0)
    j = pl.program_id(1)
    t = x_ref.shape[0]

    def proj():
        return jnp.dot(h_ref[...], w_ref[...], preferred_element_type=F32)

    @pl.when(j == 0)
    def _():
        @pl.when(i % tiles_per_seq == 0)
        def _():
            ext_ref[0:HIST_PAD, :] = jnp.zeros((HIST_PAD, D_B), F32)

        @pl.when(i % tiles_per_seq != 0)
        def _():
            ext_ref[0:HIST_PAD, :] = ext_ref[t : t + HIST_PAD, :]

        for rb in range(t // PROJ_ROWS):
            for r in range(rb * PROJ_ROWS, (rb + 1) * PROJ_ROWS, NORM_ROWS):
                h_ref[r : r + NORM_ROWS, :] = _rms_norm(x_ref[r : r + NORM_ROWS, :], g_ref[...]).astype(BF16)
            rows = slice(rb * PROJ_ROWS, (rb + 1) * PROJ_ROWS)
            ext_ref[HIST_PAD + rb * PROJ_ROWS : HIST_PAD + (rb + 1) * PROJ_ROWS, :] = jnp.dot(
                h_ref[rows, :], w_ref[...], preferred_element_type=F32)

    @pl.when(j == 1)
    def _():
        ext_ref[HIST_PAD : HIST_PAD + t, :] = ext_ref[HIST_PAD : HIST_PAD + t, :] * jax.nn.sigmoid(proj())
        hist_ref[0] = ext_ref[t + CONV_LEAD : t + HIST_PAD, :]

    @pl.when(j >= N_GLU_STEPS)
    def _():
        chunk = jnp.minimum(j - N_GLU_STEPS, t // CONV_STEP_ROWS - 1)
        r0 = pl.multiple_of(chunk * CONV_STEP_ROWS, CONV_STEP_ROWS)
        n_row_blocks = CONV_STEP_ROWS // CONV_ROWS
        n_lane_blocks = D_B // PROJ_LANES
        rows_per_block = t // n_row_blocks
        pieces = [(rc, lb) for rc in range(n_row_blocks) for lb in range(n_lane_blocks)]
        vec_done, proj_done = [], []
        parts = []

        def conv_piece(p):
            rc, lb = pieces[p]
            wins = []
            for cl in range(lb * PROJ_LANES, (lb + 1) * PROJ_LANES, CONV_LANES):
                clanes = slice(cl, cl + CONV_LANES)
                win = ext_ref[pl.ds(r0 + rc * CONV_ROWS, CONV_ROWS + HIST_PAD), clanes]
                if p > CONV_LEAD_PIECES:
                    win = _not_before(win, proj_done[p - CONV_LEAD_PIECES - 1])
                wins.append(_conv_window(win, cw_ref, clanes, CONV_ROWS))
            parts.append(jnp.concatenate(wins, axis=-1))
            vec_done.append(parts[-1])
            if lb == n_lane_blocks - 1:
                dc = jnp.concatenate(parts, axis=-1) + cb_ref[...]
                parts.clear()
                y = jax.nn.silu(_layer_norm(dc, lg_ref[...], lb_ref[...]))
                bin_ref[pl.ds(r0 + rc * CONV_ROWS, CONV_ROWS), :] = y.astype(BF16)
                vec_done.append(y)

        n_conv = 0
        for q, (rc, lb) in enumerate(pieces):
            while n_conv < min(q + CONV_LEAD_PIECES + 1, len(pieces)):
                conv_piece(n_conv)
                n_conv += 1
            rows = slice(rc * rows_per_block, (rc + 1) * rows_per_block)
            lanes = slice(lb * PROJ_LANES, (lb + 1) * PROJ_LANES)
            zp = jnp.dot(h_ref[rows, :], w_ref[:, lanes], preferred_element_type=F32)
            if len(vec_done) >= 2:
                zp = _not_before(zp, vec_done[-2])
            z_ref[rows, lanes] = zp
            proj_done.append(zp)


def _inproj_conv(x, seq, g, w16, conv_w, conv_b, ln_g, ln_b, *, tm):
    m, k = x.shape
    tiles_per_seq = seq // tm
    n_conv_steps = N_COLS - N_GLU_STEPS
    assert tm // CONV_STEP_ROWS <= n_conv_steps, "not enough grid steps to convolve the whole tile"
    const = lambda i, j: (0, 0)
    return pl.pallas_call(
        functools.partial(_inproj_conv_kernel, tiles_per_seq=tiles_per_seq),
        out_shape=[
            jax.ShapeDtypeStruct((m, n_conv_steps * COL_W), F32),
            jax.ShapeDtypeStruct((m, D_B), BF16),
            jax.ShapeDtypeStruct((m // seq, HIST, D_B), F32),
        ],
        grid=(m // tm, N_COLS),
        in_specs=[
            pl.BlockSpec((tm, k), lambda i, j: (i, 0)),
            pl.BlockSpec((1, k), const),
            pl.BlockSpec((k, COL_W), lambda i, j: (0, _glu_first(j))),
            pl.BlockSpec((CONV_WIDTH, D_B), const),
            pl.BlockSpec((1, D_B), const),
            pl.BlockSpec((1, D_B), const),
            pl.BlockSpec((1, D_B), const),
        ],
        out_specs=[
            pl.BlockSpec((tm, COL_W), lambda i, j: (i, jnp.maximum(j - N_GLU_STEPS, 0))),
            pl.BlockSpec((tm, D_B), lambda i, j: (i, 0)),
            pl.BlockSpec((1, HIST, D_B), lambda i, j: (i // tiles_per_seq, 0, 0)),
        ],
        scratch_shapes=[pltpu.VMEM((tm, k), BF16), pltpu.VMEM((tm + HIST_PAD, D_B), F32)],
        compiler_params=_cparams("arbitrary", "arbitrary", ),
        name="inproj_conv",
    )(x, g.reshape(1, k), w16, conv_w, conv_b.reshape(1, D_B), ln_g.reshape(1, D_B), ln_b.reshape(1, D_B))


def _conv_sample_kernel(*refs, seq, n_round):
    ga_ref, gb_ref, hist_ref, w_ref, cb_ref, g_ref, b_ref = refs[:7]
    f32_refs = refs[7 : 7 + n_round]
    o_ref, nh_ref = refs[7 + n_round : 9 + n_round]
    bf16_refs = refs[9 + n_round : 9 + 2 * n_round]
    ext_ref, dc_ref = refs[9 + 2 * n_round :]
    for src, dst in zip(f32_refs, bf16_refs):
        dst[...] = src[...].astype(BF16)
    n_seq = hist_ref.shape[0]
    c = ga_ref[...] * jax.nn.sigmoid(gb_ref[...])
    for s in range(n_seq):
        ext_ref[s, 0:SUBLANES, :] = jnp.zeros((SUBLANES, D_B), F32)
        ext_ref[s, CONV_LEAD:HIST_PAD, :] = hist_ref[s]
        ext_ref[s, HIST_PAD : HIST_PAD + seq, :] = c[s * seq : (s + 1) * seq, :]
    for s in range(n_seq):
        for lb in range(D_B // CONV_LANES):
            lanes = slice(lb * CONV_LANES, (lb + 1) * CONV_LANES)
            dc_ref[s * seq : (s + 1) * seq, lanes] = _conv_window(ext_ref[s, :, lanes], w_ref, lanes, seq)
        nh_ref[s] = ext_ref[s, seq + CONV_LEAD : seq + HIST_PAD, :]
    _ln_silu_rows(dc_ref, cb_ref, g_ref, b_ref, o_ref, slice(None))


def _conv_sample(z, hist, seq, conv_w, conv_b, ln_g, ln_b, *, n_seq, also_round=()):
    batch = hist.shape[0]
    steps = batch // n_seq
    slab_specs = [pl.BlockSpec((w.shape[0] // steps, w.shape[1]), lambda i: (i, 0)) for w in also_round]
    return pl.pallas_call(
        functools.partial(_conv_sample_kernel, seq=seq, n_round=len(also_round)),
        out_shape=[
            jax.ShapeDtypeStruct((batch * seq, D_B), BF16),
            jax.ShapeDtypeStruct((batch, HIST, D_B), F32),
        ] + [jax.ShapeDtypeStruct(w.shape, BF16) for w in also_round],
        grid=(steps,),
        in_specs=[
            pl.BlockSpec((n_seq * seq, COL_W), lambda i: (i, COL_GLU_A)),
            pl.BlockSpec((n_seq * seq, COL_W), lambda i: (i, COL_GLU_B)),
            pl.BlockSpec((n_seq, HIST, D_B), lambda i: (i, 0, 0)),
            pl.BlockSpec((CONV_WIDTH, D_B), lambda i: (0, 0)),
            pl.BlockSpec((1, D_B), lambda i: (0, 0)),
            pl.BlockSpec((1, D_B), lambda i: (0, 0)),
            pl.BlockSpec((1, D_B), lambda i: (0, 0)),
        ] + slab_specs,
        out_specs=[
            pl.BlockSpec((n_seq * seq, D_B), lambda i: (i, 0)),
            pl.BlockSpec((n_seq, HIST, D_B), lambda i: (i, 0, 0)),
        ] + slab_specs,
        scratch_shapes=[pltpu.VMEM((n_seq, HIST_PAD + seq, D_B), F32), pltpu.VMEM((n_seq * seq, D_B), F32)],
        compiler_params=_cparams("arbitrary"),
        name="conv_sample",
    )(z, z, hist, conv_w, conv_b.reshape(1, D_B), ln_g.reshape(1, D_B), ln_b.reshape(1, D_B), *also_round)


def _head_cols(h):
    return slice(h * MEM_HEAD_DIM, (h + 1) * MEM_HEAD_DIM)


HEAD_LANE_TILES = MEM_HEAD_DIM // LANES
HEAD_ROW_PITCH = N_MEM_HEADS * HEAD_LANE_TILES


def _split_lane_tiles(kv):
    b = kv.shape[0]
    kv = kv.reshape(b, N_MEM, N_MEM_HEADS, HEAD_LANE_TILES, LANES)
    return kv.transpose(0, 1, 3, 2, 4).reshape(b, N_MEM * HEAD_ROW_PITCH, LANES)


def _head_of(kv_ref, s, h):
    if kv_ref.shape[-1] == LANES:
        tiles = [kv_ref[s, pl.ds(j * N_MEM_HEADS + h, N_MEM, stride=HEAD_ROW_PITCH), :]
                 for j in range(HEAD_LANE_TILES)]
        return jnp.concatenate(tiles, axis=-1).astype(BF16)
    return kv_ref[s, :, _head_cols(h)].astype(BF16)


def _attn_kernel(q_ref, k_ref, v_ref, o_ref, *, rows):
    scale = 1.0 / math.sqrt(MEM_HEAD_DIM)
    pairs = [(s, h) for s in range(k_ref.shape[0]) for h in range(N_MEM_HEADS)]
    scores = []
    for s, h in pairs:
        q = (q_ref[s * rows : (s + 1) * rows, _head_cols(h)] * scale).astype(BF16)
        scores.append(lax.dot_general(q, _head_of(k_ref, s, h), (((1,), (1,)), ((), ())),
                                      preferred_element_type=F32))
    sc = jnp.concatenate(scores, axis=0)
    p = jnp.exp(sc - jnp.max(sc, axis=-1, keepdims=True))
    p = p / jnp.sum(p, axis=-1, keepdims=True)
    for n, (s, h) in enumerate(pairs):
        ph = p[n * rows : (n + 1) * rows, :].astype(BF16)
        o = jnp.dot(ph, _head_of(v_ref, s, h), preferred_element_type=F32)
        o_ref[s * rows : (s + 1) * rows, _head_cols(h)] = o.astype(BF16)


def _attn(z, cols, k, v, kv_specs, *, rows, n_seq):
    m = z.shape[0]
    return pl.pallas_call(
        functools.partial(_attn_kernel, rows=rows),
        out_shape=jax.ShapeDtypeStruct((m, D_C), BF16),
        grid=(m // (rows * n_seq),),
        in_specs=[pl.BlockSpec((rows * n_seq, COL_W), lambda i: (i, cols.q))] + kv_specs,
        out_specs=pl.BlockSpec((rows * n_seq, D_C), lambda i: (i, 0)),
        compiler_params=_cparams("arbitrary"),
        name="attn",
    )(z, k, v)


def _merge_kernel(a_ref, b_ref, c_ref, *refs):
    gate_refs = refs[: 2 * N_BRANCH]
    bg_ref, x_ref, wa_ref, wb_ref, wc_ref, wo_ref, mg_ref, x1_ref, h2_ref = refs[2 * N_BRANCH :]
    halves = []
    for half in range(2):
        cols = slice(half * COL_W, (half + 1) * COL_W)
        merged = None
        for br, (in_ref, w_ref) in enumerate(((a_ref, wa_ref), (b_ref, wb_ref), (c_ref, wc_ref))):
            y = jnp.dot(in_ref[...], w_ref[:, cols], preferred_element_type=F32)
            gcols = slice((2 * br + half) * COL_W, (2 * br + half + 1) * COL_W)
            term = jax.nn.sigmoid(gate_refs[2 * br + half][...] + bg_ref[:, gcols]) * y
            merged = term if merged is None else merged + term
        halves.append(merged.astype(BF16))
    merged16 = jnp.concatenate(halves, axis=-1)
    x1 = x_ref[...] + jnp.dot(merged16, wo_ref[...], preferred_element_type=F32)
    x1_ref[...] = x1
    h2_ref[...] = _rms_norm(x1, mg_ref[...]).astype(BF16)


def _merge(a_in, b_in, c_in, z, cols, x, b_gate, w_a, w_b, w_c, w_o, mlp_g, *, tm):
    m = x.shape[0]
    const = lambda i: (0, 0)
    resident = functools.partial(pl.BlockSpec, index_map=const, pipeline_mode=pl.Buffered(1))
    gate_specs = [
        pl.BlockSpec((tm, COL_W), functools.partial(lambda i, col: (i, col), col=cols.gates + n))
        for n in range(2 * N_BRANCH)
    ]
    return pl.pallas_call(
        _merge_kernel,
        out_shape=[jax.ShapeDtypeStruct((m, D_MODEL), F32), jax.ShapeDtypeStruct((m, D_MODEL), BF16)],
        grid=(m // tm,),
        in_specs=[pl.BlockSpec((tm, D_A), lambda i: (i, 0))] * 3
        + gate_specs
        + [
            pl.BlockSpec((1, N_BRANCH * D_MODEL), const),
            pl.BlockSpec((tm, D_MODEL), lambda i: (i, 0)),
            resident((D_A, D_MODEL)),
            resident((D_B, D_MODEL)),
            resident((D_C, D_MODEL)),
            resident((D_MODEL, D_MODEL)),
            pl.BlockSpec((1, D_MODEL), const),
        ],
        out_specs=[pl.BlockSpec((tm, D_MODEL), lambda i: (i, 0))] * 2,
        compiler_params=_cparams("arbitrary"),
        name="merge",
    )(a_in, b_in, c_in, *([z] * (2 * N_BRANCH)), b_gate.reshape(1, -1), x, w_a, w_b, w_c, w_o,
      mlp_g.reshape(1, D_MODEL))


def _mlp_kernel(h2_ref, x1_ref, wu_ref, wd_ref, fg_ref, y_ref, acc_ref):
    f = pl.program_id(1)

    @pl.when(f == 0)
    def _():
        acc_ref[...] = x1_ref[...]

    t = jnp.square(jnp.maximum(jnp.dot(h2_ref[...], wu_ref[...], preferred_element_type=F32), 0.0))
    acc_ref[...] += jnp.dot(t.astype(BF16), wd_ref[...], preferred_element_type=F32)

    @pl.when(f == pl.num_programs(1) - 1)
    def _():
        y_ref[...] = _rms_norm(acc_ref[...], fg_ref[...])


def _mlp(h2, x1, w_up, w_down, final_g, *, tm, tf):
    m = h2.shape[0]
    return pl.pallas_call(
        _mlp_kernel,
        out_shape=jax.ShapeDtypeStruct((m, D_MODEL), F32),
        grid=(m // tm, D_FF // tf),
        in_specs=[
            pl.BlockSpec((tm, D_MODEL), lambda i, f: (i, 0)),
            pl.BlockSpec((tm, D_MODEL), lambda i, f: (i, 0)),
            pl.BlockSpec((D_MODEL, tf), lambda i, f: (0, f)),
            pl.BlockSpec((tf, D_MODEL), lambda i, f: (f, 0)),
            pl.BlockSpec((1, D_MODEL), lambda i, f: (0, 0)),
        ],
        out_specs=pl.BlockSpec((tm, D_MODEL), lambda i, f: (i, 0)),
        scratch_shapes=[pltpu.VMEM((tm, D_MODEL), F32)],
        compiler_params=_cparams("arbitrary", "arbitrary"),
        name="mlp",
    )(h2, x1, w_up, w_down, final_g.reshape(1, D_MODEL))


def _mlp_first_kernel(h2_ref, wu_ref, wd_ref, acc_ref, wu16_ref, wd16_ref):
    @pl.when(pl.program_id(0) == 0)
    def _():
        acc_ref[...] = jnp.zeros(acc_ref.shape, F32)

    h2 = h2_ref[...]
    t_pieces = []
    for c in range(0, wu_ref.shape[1], PROJ_LANES):
        wu = wu_ref[:, c : c + PROJ_LANES].astype(BF16)
        wu16_ref[:, c : c + PROJ_LANES] = wu
        t = jnp.square(jnp.maximum(jnp.dot(h2, wu, preferred_element_type=F32), 0.0))
        t_pieces.append(t.astype(BF16))
    t16 = jnp.concatenate(t_pieces, axis=-1)
    for c in range(0, wd_ref.shape[1], PROJ_LANES):
        wd = wd_ref[:, c : c + PROJ_LANES].astype(BF16)
        wd16_ref[:, c : c + PROJ_LANES] = wd
        acc_ref[:, c : c + PROJ_LANES] += jnp.dot(t16, wd, preferred_element_type=F32)


def _mlp_first(h2, w_up, w_down, *, tf):
    m = h2.shape[0]
    resident = pl.BlockSpec((m, D_MODEL), lambda f: (0, 0))
    return pl.pallas_call(
        _mlp_first_kernel,
        out_shape=[
            jax.ShapeDtypeStruct((m, D_MODEL), F32),
            jax.ShapeDtypeStruct((D_MODEL, D_FF), BF16),
            jax.ShapeDtypeStruct((D_FF, D_MODEL), BF16),
        ],
        grid=(D_FF // tf,),
        in_specs=[
            resident,
            pl.BlockSpec((D_MODEL, tf), lambda f: (0, f)),
            pl.BlockSpec((tf, D_MODEL), lambda f: (f, 0)),
        ],
        out_specs=[
            resident,
            pl.BlockSpec((D_MODEL, tf), lambda f: (0, f)),
            pl.BlockSpec((tf, D_MODEL), lambda f: (f, 0)),
        ],
        compiler_params=_cparams("arbitrary"),
        name="mlp_first",
    )(h2, w_up, w_down)


def _residual_norm_kernel(x_ref, d_ref, g_ref, y_ref):
    y_ref[...] = _rms_norm(x_ref[...] + d_ref[...], g_ref[...])


def _residual_norm(x, d, g, *, tm):
    m = x.shape[0]
    row_spec = pl.BlockSpec((tm, D_MODEL), lambda i: (i, 0))
    return pl.pallas_call(
        _residual_norm_kernel,
        out_shape=jax.ShapeDtypeStruct((m, D_MODEL), F32),
        grid=(m // tm,),
        in_specs=[row_spec, row_spec, pl.BlockSpec((1, D_MODEL), lambda i: (0, 0))],
        out_specs=row_spec,
        compiler_params=_cparams("arbitrary"),
        name="residual_norm",
    )(x, d, g.reshape(1, D_MODEL))


def _sgu_operands(sgu_w, sgu_b, chunk):
    reps = CHUNK // chunk
    w_tiled = jnp.tile(sgu_w[:, :chunk, :chunk], (1, reps, reps))
    bias_rows = jnp.repeat(jnp.tile(sgu_b[:, :chunk].T, (reps, 1)), SGU_HEAD, axis=1)
    return w_tiled, bias_rows


PROMPT_ATTN_ROWS = 512
SAMPLE_ATTN_SEQS = 8


def _layer(x, seq, k, v, hist, p, *, want_vn):
    chunk = min(seq, CHUNK)
    conv_args = (p["conv_w"], p["conv_b"], p["conv_ln_g"], p["conv_ln_b"])
    if hist is None:
        assert p["w_in"].dtype == BF16, "the group with a single token tile goes first and rounds w_in"
        z, b_in, new_hist = _inproj_conv(x, seq, p["attn_norm_g"], p["w_in"], *conv_args, tm=1024)
        cols = Z_COLS_NO_GLU
    else:
        z, w_in16 = _norm_matmul(x, p["attn_norm_g"], p["w_in"], tm=1024, tn=512, emit_w16=True)
        p = {**p, "w_in": w_in16}
        cols = Z_COLS_ALL
        out_names = ("w_a_out", "w_b_out", "w_c_out", "w_o")
        b_in, new_hist, *rounded = _conv_sample(z, hist, seq, *conv_args, n_seq=16,
                                               also_round=[p[name] for name in out_names])
        p = {**p, **dict(zip(out_names, rounded))}
    w_tiled, bias_rows = _sgu_operands(p["sgu_w"], p["sgu_b"], chunk)
    sgu_out = _sgu(z, cols, p["sgu_ln_g"], p["sgu_ln_b"], w_tiled, bias_rows, chunk=chunk, rows=512,
                   want_vn=want_vn)
    if hist is None:
        steps = seq // PROMPT_ATTN_ROWS
        kv_specs = [pl.BlockSpec((1, N_MEM, D_C), functools.partial(lambda i, col: (i // steps, 0, col), col=col))
                    for col in (0, 1)]
        c_in = _attn(z, cols, k, v, kv_specs, rows=PROMPT_ATTN_ROWS, n_seq=1)
    else:
        kv_specs = [pl.BlockSpec((SAMPLE_ATTN_SEQS, N_MEM * HEAD_ROW_PITCH, LANES), lambda i: (i, 0, 0))] * 2
        c_in = _attn(z, cols, k, v, kv_specs, rows=seq, n_seq=SAMPLE_ATTN_SEQS)
    x1, h2 = _merge(sgu_out[0], b_in, c_in, z, cols, x, p["b_gate"], p["w_a_out"], p["w_b_out"], p["w_c_out"],
                    p["w_o"], p["mlp_norm_g"], tm=256)
    if p["w_up"].dtype == F32:
        d, w_up16, w_down16 = _mlp_first(h2, p["w_up"], p["w_down"], tf=512)
        p = {**p, "w_up": w_up16, "w_down": w_down16}
        y = _residual_norm(x1, d, p["final_norm_g"], tm=256)
    else:
        y = _mlp(h2, x1, p["w_up"], p["w_down"], p["final_norm_g"], tm=512, tf=1024)
    return y, new_hist, (sgu_out[1] if want_vn else None), p


def kernel(x_prompt, x_sample, mem_prompt, cache_mem_k, cache_mem_v, state_conv, attn_norm_g, w_in, b_gate,
           sgu_ln_g, sgu_ln_b, sgu_w, sgu_b, w_a_out, conv_w, conv_b, conv_ln_g, conv_ln_b, w_b_out, mem_norm_g,
           w_mem_kv, w_c_out, w_o, mlp_norm_g, w_up, w_down, final_norm_g):
    assert attn_norm_g.shape[0] == 1, "single-layer trunk"
    bp, sp, _ = x_prompt.shape
    bs, ss, _ = x_sample.shape
    p = dict(
        attn_norm_g=attn_norm_g[0], w_in=w_in[0], b_gate=b_gate[0], sgu_ln_g=sgu_ln_g[0],
        sgu_ln_b=sgu_ln_b[0], sgu_w=sgu_w[0], sgu_b=sgu_b[0], w_a_out=w_a_out[0],
        conv_w=conv_w[0], conv_b=conv_b[0], conv_ln_g=conv_ln_g[0], conv_ln_b=conv_ln_b[0],
        w_b_out=w_b_out[0], w_c_out=w_c_out[0], w_o=w_o[0],
        mlp_norm_g=mlp_norm_g[0], w_up=w_up[0], w_down=w_down[0],
        final_norm_g=final_norm_g,
    )
    kv = _norm_matmul(mem_prompt.reshape(bp * N_MEM, D_MODEL), mem_norm_g[0], w_mem_kv[0],
                      tm=1024, tn=512).reshape(bp, N_MEM, 2 * D_C)
    y_s, hist_s, vn_s, p = _layer(x_sample.reshape(bs * ss, D_MODEL), ss, _split_lane_tiles(cache_mem_k[0]),
                                  _split_lane_tiles(cache_mem_v[0]), state_conv[0], p, want_vn=True)
    y_p, hist_p, _, _ = _layer(x_prompt.reshape(bp * sp, D_MODEL), sp, kv, kv, None, p, want_vn=False)
    head_shape = (1, bp, N_MEM, N_MEM_HEADS, MEM_HEAD_DIM)
    return (
        y_p.reshape(bp, sp, D_MODEL),
        y_s.reshape(bs, ss, D_MODEL),
        kv[:, :, :D_C].reshape(head_shape),
        kv[:, :, D_C:].reshape(head_shape),
        hist_p[None],
        hist_s[None],
        vn_s.reshape(1, bs, ss, D_A),
    )
```

```python
import functools
import math
from typing import NamedTuple

import jax
import jax.numpy as jnp
from jax import lax
from jax.experimental import pallas as pl
from jax.experimental.pallas import tpu as pltpu

F32 = jnp.float32
BF16 = jnp.bfloat16

D_MODEL = 2048
D_A = 1024
D_B = 1024
D_C = 1024
CHUNK = 128
SGU_GROUPS = 8
SGU_HEAD = D_A // SGU_GROUPS
CONV_WIDTH = 31
HIST = CONV_WIDTH - 1
N_MEM = 256
N_MEM_HEADS = 4
MEM_HEAD_DIM = D_C // N_MEM_HEADS
N_BRANCH = 3
D_FF = 4 * D_MODEL
N_IN = 2 * D_A + 2 * D_B + D_C + N_BRANCH * D_MODEL
EPS = 1e-6

COL_U, COL_V, COL_GLU_A, COL_GLU_B, COL_Q, COL_GATES = 0, 1, 2, 3, 4, 5
COL_W = 1024


class ZCols(NamedTuple):
    u: int
    v: int
    q: int
    gates: int


def _col_block_spec(z, rows, col):
    if z.ndim == 3:
        return pl.BlockSpec((None, rows, COL_W), lambda i: (col, i, 0))
    return pl.BlockSpec((rows, COL_W), lambda i: (i, col))


Z_COLS_ALL = ZCols(COL_U, COL_V, COL_Q, COL_GATES)
Z_COLS_NO_GLU = ZCols(COL_U, COL_V, COL_Q - 2, COL_GATES - 2)

V7X_VMEM_BYTES = 64 * 1024 * 1024
VMEM_LIMIT_BYTES = V7X_VMEM_BYTES - 8 * 1024 * 1024
SUBLANES = 8
LANES = 128
V7X_MXU_COLS = 256
PROJ_LANES = V7X_MXU_COLS
PROJ_ROWS = 256
HIST_PAD = 32


def _cparams(*semantics, flags=None):
    return pltpu.CompilerParams(dimension_semantics=semantics, vmem_limit_bytes=VMEM_LIMIT_BYTES, flags=flags)


def _not_before(value, anchor):
    tile = anchor[-SUBLANES:, -LANES:]
    zero = pltpu.bitcast((pltpu.bitcast(tile, jnp.uint32) >> 16) >> 16, F32)
    reps = (value.shape[0] // SUBLANES, value.shape[1] // LANES)
    return value + jnp.tile(zero, reps)


def _rms_norm(x, g):
    ms = jnp.mean(x * x, axis=-1, keepdims=True)
    return x * lax.rsqrt(ms + EPS) * g


def _layer_norm(x, g, b):
    mu = jnp.mean(x, axis=-1, keepdims=True)
    xc = x - mu
    var = jnp.mean(xc * xc, axis=-1, keepdims=True)
    return xc * lax.rsqrt(var + EPS) * g + b


NORM_ROWS = 128


def _norm_matmul_kernel(x_ref, g_ref, w_ref, o_ref, *rest):
    *w16_refs, h_ref = rest

    @pl.when(pl.program_id(1) == 0)
    def _():
        def body(r, carry):
            rows = pl.ds(pl.multiple_of(r * NORM_ROWS, NORM_ROWS), NORM_ROWS)
            h_ref[rows, :] = _rms_norm(x_ref[rows, :], g_ref[...]).astype(BF16)
            return carry

        lax.fori_loop(0, x_ref.shape[0] // NORM_ROWS, body, 0)

    for c in range(0, w_ref.shape[1], PROJ_LANES):
        w16 = w_ref[:, c : c + PROJ_LANES].astype(BF16)
        if w16_refs:
            w16_refs[0][:, c : c + PROJ_LANES] = w16
        o_ref[:, c : c + PROJ_LANES] = jnp.dot(h_ref[...], w16, preferred_element_type=F32)


def _norm_matmul(x, g, w, *, tm, tn, emit_w16=False):
    m, k = x.shape
    n = w.shape[1]
    out_shape = [jax.ShapeDtypeStruct((m, n), F32)]
    out_specs = [pl.BlockSpec((tm, tn), lambda i, j: (i, j))]
    if emit_w16:
        assert m == tm, "every weight block must be visited exactly once"
        per_col = COL_W // tn
        out_shape.append(jax.ShapeDtypeStruct((n // COL_W, k, COL_W), BF16))
        out_specs.append(pl.BlockSpec((None, k, tn), lambda i, j: (j // per_col, 0, j % per_col)))
    outs = pl.pallas_call(
        _norm_matmul_kernel,
        out_shape=out_shape,
        grid=(m // tm, n // tn),
        in_specs=[
            pl.BlockSpec((tm, k), lambda i, j: (i, 0)),
            pl.BlockSpec((1, k), lambda i, j: (0, 0)),
            pl.BlockSpec((k, tn), lambda i, j: (0, j)),
        ],
        out_specs=out_specs,
        scratch_shapes=[pltpu.VMEM((tm, k), BF16)],
        compiler_params=_cparams("arbitrary", "arbitrary"),
        name="norm_matmul",
    )(x, g.reshape(1, k), w)
    return outs if emit_w16 else outs[0]


def _sgu_kernel(u_ref, v_ref, g_ref, b_ref, w_ref, bias_ref, a_ref, *vn_refs, chunk):
    r = lax.broadcasted_iota(jnp.int32, (CHUNK, CHUNK), 0)
    c = lax.broadcasted_iota(jnp.int32, (CHUNK, CHUNK), 1)
    seg_bits = chunk.bit_length() - 1
    same_segment = (r >> seg_bits) == (c >> seg_bits)
    mask = jnp.logical_and(same_segment, (r & (chunk - 1)) >= (c & (chunk - 1)))
    w_s = [jnp.where(mask, w_ref[grp], 0.0).astype(BF16) for grp in range(SGU_GROUPS)]
    for blk in range(u_ref.shape[0] // CHUNK):
        rows = slice(blk * CHUNK, (blk + 1) * CHUNK)
        vn = _layer_norm(jax.nn.gelu(v_ref[rows, :]), g_ref[...], b_ref[...])
        if vn_refs:
            vn_refs[0][rows, :] = vn
        vn16 = vn.astype(BF16)
        for grp in range(SGU_GROUPS):
            cols = slice(grp * SGU_HEAD, (grp + 1) * SGU_HEAD)
            mixed = jnp.dot(w_s[grp], vn16[:, cols], preferred_element_type=F32) + bias_ref[:, cols]
            a_ref[rows, cols] = (jax.nn.gelu(u_ref[rows, cols]) * mixed).astype(BF16)


def _sgu(z, cols, ln_g, ln_b, w_tiled, bias_rows, *, chunk, rows, want_vn):
    m = z.shape[-2]
    out_shape = [jax.ShapeDtypeStruct((m, D_A), BF16)]
    out_specs = [pl.BlockSpec((rows, D_A), lambda i: (i, 0))]
    if want_vn:
        out_shape.append(jax.ShapeDtypeStruct((m, D_A), F32))
        out_specs.append(pl.BlockSpec((rows, D_A), lambda i: (i, 0)))
    return pl.pallas_call(
        functools.partial(_sgu_kernel, chunk=chunk),
        out_shape=out_shape,
        grid=(m // rows,),
        in_specs=[
            _col_block_spec(z, rows, cols.u),
            _col_block_spec(z, rows, cols.v),
            pl.BlockSpec((1, D_A), lambda i: (0, 0)),
            pl.BlockSpec((1, D_A), lambda i: (0, 0)),
            pl.BlockSpec((SGU_GROUPS, CHUNK, CHUNK), lambda i: (0, 0, 0)),
            pl.BlockSpec((CHUNK, D_A), lambda i: (0, 0)),
        ],
        out_specs=out_specs,
        compiler_params=_cparams("arbitrary"),
        name="sgu",
    )(z, z, ln_g.reshape(1, D_A), ln_b.reshape(1, D_A), w_tiled, bias_rows)


CONV_ROWS = 32
CONV_LANES = 128
CONV_LEAD = HIST_PAD - HIST


def _conv_window(win, w_ref, lanes, n_rows):
    out = None
    for b in range(SUBLANES):
        rows_b = n_rows if b == 0 else n_rows + SUBLANES
        y = None
        for a in range(HIST_PAD // SUBLANES + 1):
            k = SUBLANES * a + b - CONV_LEAD
            if 0 <= k < CONV_WIDTH:
                term = w_ref[k : k + 1, lanes] * win[SUBLANES * a : SUBLANES * a + rows_b, :]
                y = term if y is None else y + term
        shifted = y[b : b + n_rows, :]
        out = shifted if out is None else out + shifted
    return out


def _ln_silu_rows(dc_ref, cb_ref, g_ref, b_ref, o_ref, rows):
    dc = dc_ref[rows, :] + cb_ref[...]
    o_ref[rows, :] = jax.nn.silu(_layer_norm(dc, g_ref[...], b_ref[...])).astype(BF16)


CONV_STEP_ROWS = 128
CONV_LEAD_PIECES = 2
N_GLU_STEPS = 2
N_COLS = N_IN // COL_W


def _glu_first(j):
    return jnp.where(j < N_GLU_STEPS, j + COL_GLU_A, jnp.where(j < COL_GLU_A + N_GLU_STEPS, j - N_GLU_STEPS, j))


def _inproj_conv_kernel(x_ref, g_ref, w_ref, cw_ref, cb_ref, lg_ref, lb_ref, z_ref, bin_ref, hist_ref,
                        h_ref, ext_ref, *, tiles_per_seq):
    i = pl.program_id(0)
    j = pl.program_id(1)
    t = x_ref.shape[0]

    def proj():
        return jnp.dot(h_ref[...], w_ref[...], preferred_element_type=F32)

    @pl.when(j == 0)
    def _():
        @pl.when(i % tiles_per_seq == 0)
        def _():
            ext_ref[0:HIST_PAD, :] = jnp.zeros((HIST_PAD, D_B), F32)

        @pl.when(i % tiles_per_seq != 0)
        def _():
            ext_ref[0:HIST_PAD, :] = ext_ref[t : t + HIST_PAD, :]

        for rb in range(t // PROJ_ROWS):
            for r in range(rb * PROJ_ROWS, (rb + 1) * PROJ_ROWS, NORM_ROWS):
                h_ref[r : r + NORM_ROWS, :] = _rms_norm(x_ref[r : r + NORM_ROWS, :], g_ref[...]).astype(BF16)
            rows = slice(rb * PROJ_ROWS, (rb + 1) * PROJ_ROWS)
            ext_ref[HIST_PAD + rb * PROJ_ROWS : HIST_PAD + (rb + 1) * PROJ_ROWS, :] = jnp.dot(
                h_ref[rows, :], w_ref[...], preferred_element_type=F32)

    @pl.when(j == 1)
    def _():
        ext_ref[HIST_PAD : HIST_PAD + t, :] = ext_ref[HIST_PAD : HIST_PAD + t, :] * jax.nn.sigmoid(proj())
        hist_ref[0] = ext_ref[t + CONV_LEAD : t + HIST_PAD, :]

    @pl.when(j >= N_GLU_STEPS)
    def _():
        chunk = jnp.minimum(j - N_GLU_STEPS, t // CONV_STEP_ROWS - 1)
        r0 = pl.multiple_of(chunk * CONV_STEP_ROWS, CONV_STEP_ROWS)
        n_row_blocks = CONV_STEP_ROWS // CONV_ROWS
        n_lane_blocks = D_B // PROJ_LANES
        rows_per_block = t // n_row_blocks
        pieces = [(rc, lb) for rc in range(n_row_blocks) for lb in range(n_lane_blocks)]
        vec_done, proj_done = [], []
        parts = []

        def conv_piece(p):
            rc, lb = pieces[p]
            wins = []
            for cl in range(lb * PROJ_LANES, (lb + 1) * PROJ_LANES, CONV_LANES):
                clanes = slice(cl, cl + CONV_LANES)
                win = ext_ref[pl.ds(r0 + rc * CONV_ROWS, CONV_ROWS + HIST_PAD), clanes]
                if p > CONV_LEAD_PIECES:
                    win = _not_before(win, proj_done[p - CONV_LEAD_PIECES - 1])
                wins.append(_conv_window(win, cw_ref, clanes, CONV_ROWS))
            parts.append(jnp.concatenate(wins, axis=-1))
            vec_done.append(parts[-1])
            if lb == n_lane_blocks - 1:
                dc = jnp.concatenate(parts, axis=-1) + cb_ref[...]
                parts.clear()
                y = jax.nn.silu(_layer_norm(dc, lg_ref[...], lb_ref[...]))
                bin_ref[pl.ds(r0 + rc * CONV_ROWS, CONV_ROWS), :] = y.astype(BF16)
                vec_done.append(y)

        n_conv = 0
        for q, (rc, lb) in enumerate(pieces):
            while n_conv < min(q + CONV_LEAD_PIECES + 1, len(pieces)):
                conv_piece(n_conv)
                n_conv += 1
            rows = slice(rc * rows_per_block, (rc + 1) * rows_per_block)
            lanes = slice(lb * PROJ_LANES, (lb + 1) * PROJ_LANES)
            zp = jnp.dot(h_ref[rows, :], w_ref[:, lanes], preferred_element_type=F32)
            if len(vec_done) >= 2:
                zp = _not_before(zp, vec_done[-2])
            z_ref[rows, lanes] = zp
            proj_done.append(zp)


def _inproj_conv(x, seq, g, w16, conv_w, conv_b, ln_g, ln_b, *, tm):
    m, k = x.shape
    tiles_per_seq = seq // tm
    n_conv_steps = N_COLS - N_GLU_STEPS
    assert tm // CONV_STEP_ROWS <= n_conv_steps, "not enough grid steps to convolve the whole tile"
    const = lambda i, j: (0, 0)
    return pl.pallas_call(
        functools.partial(_inproj_conv_kernel, tiles_per_seq=tiles_per_seq),
        out_shape=[
            jax.ShapeDtypeStruct((n_conv_steps, m, COL_W), F32),
            jax.ShapeDtypeStruct((m, D_B), BF16),
            jax.ShapeDtypeStruct((m // seq, HIST, D_B), F32),
        ],
        grid=(m // tm, N_COLS),
        in_specs=[
            pl.BlockSpec((tm, k), lambda i, j: (i, 0)),
            pl.BlockSpec((1, k), const),
            pl.BlockSpec((None, k, COL_W), lambda i, j: (_glu_first(j), 0, 0)),
            pl.BlockSpec((CONV_WIDTH, D_B), const),
            pl.BlockSpec((1, D_B), const),
            pl.BlockSpec((1, D_B), const),
            pl.BlockSpec((1, D_B), const),
        ],
        out_specs=[
            pl.BlockSpec((None, tm, COL_W), lambda i, j: (jnp.maximum(j - N_GLU_STEPS, 0), i, 0)),
            pl.BlockSpec((tm, D_B), lambda i, j: (i, 0)),
            pl.BlockSpec((1, HIST, D_B), lambda i, j: (i // tiles_per_seq, 0, 0)),
        ],
        scratch_shapes=[pltpu.VMEM((tm, k), BF16), pltpu.VMEM((tm + HIST_PAD, D_B), F32)],
        compiler_params=_cparams("arbitrary", "arbitrary", ),
        name="inproj_conv",
    )(x, g.reshape(1, k), w16, conv_w, conv_b.reshape(1, D_B), ln_g.reshape(1, D_B), ln_b.reshape(1, D_B))


def _conv_sample_kernel(*refs, seq, n_round):
    ga_ref, gb_ref, hist_ref, w_ref, cb_ref, g_ref, b_ref = refs[:7]
    f32_refs = refs[7 : 7 + n_round]
    o_ref, nh_ref = refs[7 + n_round : 9 + n_round]
    bf16_refs = refs[9 + n_round : 9 + 2 * n_round]
    ext_ref, dc_ref = refs[9 + 2 * n_round :]
    for src, dst in zip(f32_refs, bf16_refs):
        dst[...] = src[...].astype(BF16)
    n_seq = hist_ref.shape[0]
    c = ga_ref[...] * jax.nn.sigmoid(gb_ref[...])
    for s in range(n_seq):
        ext_ref[s, 0:SUBLANES, :] = jnp.zeros((SUBLANES, D_B), F32)
        ext_ref[s, CONV_LEAD:HIST_PAD, :] = hist_ref[s]
        ext_ref[s, HIST_PAD : HIST_PAD + seq, :] = c[s * seq : (s + 1) * seq, :]
    for s in range(n_seq):
        for lb in range(D_B // CONV_LANES):
            lanes = slice(lb * CONV_LANES, (lb + 1) * CONV_LANES)
            dc_ref[s * seq : (s + 1) * seq, lanes] = _conv_window(ext_ref[s, :, lanes], w_ref, lanes, seq)
        nh_ref[s] = ext_ref[s, seq + CONV_LEAD : seq + HIST_PAD, :]
    _ln_silu_rows(dc_ref, cb_ref, g_ref, b_ref, o_ref, slice(None))


def _conv_sample(z, hist, seq, conv_w, conv_b, ln_g, ln_b, *, n_seq, also_round=()):
    batch = hist.shape[0]
    steps = batch // n_seq
    slab_specs = [pl.BlockSpec((w.shape[0] // steps, w.shape[1]), lambda i: (i, 0)) for w in also_round]
    return pl.pallas_call(
        functools.partial(_conv_sample_kernel, seq=seq, n_round=len(also_round)),
        out_shape=[
            jax.ShapeDtypeStruct((batch * seq, D_B), BF16),
            jax.ShapeDtypeStruct((batch, HIST, D_B), F32),
        ] + [jax.ShapeDtypeStruct(w.shape, BF16) for w in also_round],
        grid=(steps,),
        in_specs=[
            pl.BlockSpec((n_seq * seq, COL_W), lambda i: (i, COL_GLU_A)),
            pl.BlockSpec((n_seq * seq, COL_W), lambda i: (i, COL_GLU_B)),
            pl.BlockSpec((n_seq, HIST, D_B), lambda i: (i, 0, 0)),
            pl.BlockSpec((CONV_WIDTH, D_B), lambda i: (0, 0)),
            pl.BlockSpec((1, D_B), lambda i: (0, 0)),
            pl.BlockSpec((1, D_B), lambda i: (0, 0)),
            pl.BlockSpec((1, D_B), lambda i: (0, 0)),
        ] + slab_specs,
        out_specs=[
            pl.BlockSpec((n_seq * seq, D_B), lambda i: (i, 0)),
            pl.BlockSpec((n_seq, HIST, D_B), lambda i: (i, 0, 0)),
        ] + slab_specs,
        scratch_shapes=[pltpu.VMEM((n_seq, HIST_PAD + seq, D_B), F32), pltpu.VMEM((n_seq * seq, D_B), F32)],
        compiler_params=_cparams("arbitrary"),
        name="conv_sample",
    )(z, z, hist, conv_w, conv_b.reshape(1, D_B), ln_g.reshape(1, D_B), ln_b.reshape(1, D_B), *also_round)


def _head_cols(h):
    return slice(h * MEM_HEAD_DIM, (h + 1) * MEM_HEAD_DIM)


HEAD_LANE_TILES = MEM_HEAD_DIM // LANES
HEAD_ROW_PITCH = N_MEM_HEADS * HEAD_LANE_TILES


def _split_lane_tiles(kv):
    b = kv.shape[0]
    kv = kv.reshape(b, N_MEM, N_MEM_HEADS, HEAD_LANE_TILES, LANES)
    return kv.transpose(0, 1, 3, 2, 4).reshape(b, N_MEM * HEAD_ROW_PITCH, LANES)


def _head_of(kv_ref, s, h):
    if kv_ref.shape[-1] == LANES:
        tiles = [kv_ref[s, pl.ds(j * N_MEM_HEADS + h, N_MEM, stride=HEAD_ROW_PITCH), :]
                 for j in range(HEAD_LANE_TILES)]
        return jnp.concatenate(tiles, axis=-1).astype(BF16)
    return kv_ref[s, :, _head_cols(h)].astype(BF16)


def _attn_kernel(q_ref, k_ref, v_ref, o_ref, *, rows):
    scale = 1.0 / math.sqrt(MEM_HEAD_DIM)
    pairs = [(s, h) for s in range(k_ref.shape[0]) for h in range(N_MEM_HEADS)]
    scores = []
    for s, h in pairs:
        q = (q_ref[s * rows : (s + 1) * rows, _head_cols(h)] * scale).astype(BF16)
        scores.append(lax.dot_general(q, _head_of(k_ref, s, h), (((1,), (1,)), ((), ())),
                                      preferred_element_type=F32))
    sc = jnp.concatenate(scores, axis=0)
    p = jnp.exp(sc - jnp.max(sc, axis=-1, keepdims=True))
    p = p / jnp.sum(p, axis=-1, keepdims=True)
    for n, (s, h) in enumerate(pairs):
        ph = p[n * rows : (n + 1) * rows, :].astype(BF16)
        o = jnp.dot(ph, _head_of(v_ref, s, h), preferred_element_type=F32)
        o_ref[s * rows : (s + 1) * rows, _head_cols(h)] = o.astype(BF16)


def _attn(z, cols, k, v, kv_specs, *, rows, n_seq):
    m = z.shape[-2]
    return pl.pallas_call(
        functools.partial(_attn_kernel, rows=rows),
        out_shape=jax.ShapeDtypeStruct((m, D_C), BF16),
        grid=(m // (rows * n_seq),),
        in_specs=[_col_block_spec(z, rows * n_seq, cols.q)] + kv_specs,
        out_specs=pl.BlockSpec((rows * n_seq, D_C), lambda i: (i, 0)),
        compiler_params=_cparams("arbitrary"),
        name="attn",
    )(z, k, v)


def _merge_kernel(a_ref, b_ref, c_ref, *refs):
    gate_refs = refs[: 2 * N_BRANCH]
    bg_ref, x_ref, wa_ref, wb_ref, wc_ref, wo_ref, mg_ref, x1_ref, h2_ref = refs[2 * N_BRANCH :]
    halves = []
    for half in range(2):
        cols = slice(half * COL_W, (half + 1) * COL_W)
        merged = None
        for br, (in_ref, w_ref) in enumerate(((a_ref, wa_ref), (b_ref, wb_ref), (c_ref, wc_ref))):
            y = jnp.dot(in_ref[...], w_ref[:, cols], preferred_element_type=F32)
            gcols = slice((2 * br + half) * COL_W, (2 * br + half + 1) * COL_W)
            term = jax.nn.sigmoid(gate_refs[2 * br + half][...] + bg_ref[:, gcols]) * y
            merged = term if merged is None else merged + term
        halves.append(merged.astype(BF16))
    merged16 = jnp.concatenate(halves, axis=-1)
    x1 = x_ref[...] + jnp.dot(merged16, wo_ref[...], preferred_element_type=F32)
    x1_ref[...] = x1
    h2_ref[...] = _rms_norm(x1, mg_ref[...]).astype(BF16)


def _merge(a_in, b_in, c_in, z, cols, x, b_gate, w_a, w_b, w_c, w_o, mlp_g, *, tm):
    m = x.shape[0]
    const = lambda i: (0, 0)
    resident = functools.partial(pl.BlockSpec, index_map=const, pipeline_mode=pl.Buffered(1))
    gate_specs = [_col_block_spec(z, tm, cols.gates + n) for n in range(2 * N_BRANCH)]
    return pl.pallas_call(
        _merge_kernel,
        out_shape=[jax.ShapeDtypeStruct((m, D_MODEL), F32), jax.ShapeDtypeStruct((m, D_MODEL), BF16)],
        grid=(m // tm,),
        in_specs=[pl.BlockSpec((tm, D_A), lambda i: (i, 0))] * 3
        + gate_specs
        + [
            pl.BlockSpec((1, N_BRANCH * D_MODEL), const),
            pl.BlockSpec((tm, D_MODEL), lambda i: (i, 0)),
            resident((D_A, D_MODEL)),
            resident((D_B, D_MODEL)),
            resident((D_C, D_MODEL)),
            resident((D_MODEL, D_MODEL)),
            pl.BlockSpec((1, D_MODEL), const),
        ],
        out_specs=[pl.BlockSpec((tm, D_MODEL), lambda i: (i, 0))] * 2,
        compiler_params=_cparams("arbitrary"),
        name="merge",
    )(a_in, b_in, c_in, *([z] * (2 * N_BRANCH)), b_gate.reshape(1, -1), x, w_a, w_b, w_c, w_o,
      mlp_g.reshape(1, D_MODEL))


def _mlp_kernel(h2_ref, x1_ref, wu_ref, wd_ref, fg_ref, y_ref, acc_ref):
    f = pl.program_id(1)

    @pl.when(f == 0)
    def _():
        acc_ref[...] = x1_ref[...]

    t = jnp.square(jnp.maximum(jnp.dot(h2_ref[...], wu_ref[...], preferred_element_type=F32), 0.0))
    acc_ref[...] += jnp.dot(t.astype(BF16), wd_ref[...], preferred_element_type=F32)

    @pl.when(f == pl.num_programs(1) - 1)
    def _():
        y_ref[...] = _rms_norm(acc_ref[...], fg_ref[...])


def _mlp(h2, x1, w_up, w_down, final_g, *, tm, tf):
    m = h2.shape[0]
    return pl.pallas_call(
        _mlp_kernel,
        out_shape=jax.ShapeDtypeStruct((m, D_MODEL), F32),
        grid=(m // tm, D_FF // tf),
        in_specs=[
            pl.BlockSpec((tm, D_MODEL), lambda i, f: (i, 0)),
            pl.BlockSpec((tm, D_MODEL), lambda i, f: (i, 0)),
            pl.BlockSpec((D_MODEL, tf), lambda i, f: (0, f)),
            pl.BlockSpec((tf, D_MODEL), lambda i, f: (f, 0)),
            pl.BlockSpec((1, D_MODEL), lambda i, f: (0, 0)),
        ],
        out_specs=pl.BlockSpec((tm, D_MODEL), lambda i, f: (i, 0)),
        scratch_shapes=[pltpu.VMEM((tm, D_MODEL), F32)],
        compiler_params=_cparams("arbitrary", "arbitrary"),
        name="mlp",
    )(h2, x1, w_up, w_down, final_g.reshape(1, D_MODEL))


def _mlp_first_kernel(h2_ref, wu_ref, wd_ref, acc_ref, wu16_ref, wd16_ref):
    @pl.when(pl.program_id(0) == 0)
    def _():
        acc_ref[...] = jnp.zeros(acc_ref.shape, F32)

    h2 = h2_ref[...]
    t_pieces = []
    for c in range(0, wu_ref.shape[1], PROJ_LANES):
        wu = wu_ref[:, c : c + PROJ_LANES].astype(BF16)
        wu16_ref[:, c : c + PROJ_LANES] = wu
        t = jnp.square(jnp.maximum(jnp.dot(h2, wu, preferred_element_type=F32), 0.0))
        t_pieces.append(t.astype(BF16))
    t16 = jnp.concatenate(t_pieces, axis=-1)
    for c in range(0, wd_ref.shape[1], PROJ_LANES):
        wd = wd_ref[:, c : c + PROJ_LANES].astype(BF16)
        wd16_ref[:, c : c + PROJ_LANES] = wd
        acc_ref[:, c : c + PROJ_LANES] += jnp.dot(t16, wd, preferred_element_type=F32)


def _mlp_first(h2, w_up, w_down, *, tf):
    m = h2.shape[0]
    resident = pl.BlockSpec((m, D_MODEL), lambda f: (0, 0))
    return pl.pallas_call(
        _mlp_first_kernel,
        out_shape=[
            jax.ShapeDtypeStruct((m, D_MODEL), F32),
            jax.ShapeDtypeStruct((D_MODEL, D_FF), BF16),
            jax.ShapeDtypeStruct((D_FF, D_MODEL), BF16),
        ],
        grid=(D_FF // tf,),
        in_specs=[
            resident,
            pl.BlockSpec((D_MODEL, tf), lambda f: (0, f)),
            pl.BlockSpec((tf, D_MODEL), lambda f: (f, 0)),
        ],
        out_specs=[
            resident,
            pl.BlockSpec((D_MODEL, tf), lambda f: (0, f)),
            pl.BlockSpec((tf, D_MODEL), lambda f: (f, 0)),
        ],
        compiler_params=_cparams("arbitrary"),
        name="mlp_first",
    )(h2, w_up, w_down)


def _residual_norm_kernel(x_ref, d_ref, g_ref, y_ref):
    y_ref[...] = _rms_norm(x_ref[...] + d_ref[...], g_ref[...])


def _residual_norm(x, d, g, *, tm):
    m = x.shape[0]
    row_spec = pl.BlockSpec((tm, D_MODEL), lambda i: (i, 0))
    return pl.pallas_call(
        _residual_norm_kernel,
        out_shape=jax.ShapeDtypeStruct((m, D_MODEL), F32),
        grid=(m // tm,),
        in_specs=[row_spec, row_spec, pl.BlockSpec((1, D_MODEL), lambda i: (0, 0))],
        out_specs=row_spec,
        compiler_params=_cparams("arbitrary"),
        name="residual_norm",
    )(x, d, g.reshape(1, D_MODEL))


def _sgu_operands(sgu_w, sgu_b, chunk):
    reps = CHUNK // chunk
    w_tiled = jnp.tile(sgu_w[:, :chunk, :chunk], (1, reps, reps))
    bias_rows = jnp.repeat(jnp.tile(sgu_b[:, :chunk].T, (reps, 1)), SGU_HEAD, axis=1)
    return w_tiled, bias_rows


PROMPT_ATTN_ROWS = 512
SAMPLE_ATTN_SEQS = 8


def _layer(x, seq, k, v, hist, p, *, want_vn):
    chunk = min(seq, CHUNK)
    conv_args = (p["conv_w"], p["conv_b"], p["conv_ln_g"], p["conv_ln_b"])
    if hist is None:
        assert p["w_in"].dtype == BF16, "the group with a single token tile goes first and rounds w_in"
        z, b_in, new_hist = _inproj_conv(x, seq, p["attn_norm_g"], p["w_in"], *conv_args, tm=1024)
        cols = Z_COLS_NO_GLU
    else:
        z, w_in16 = _norm_matmul(x, p["attn_norm_g"], p["w_in"], tm=1024, tn=512, emit_w16=True)
        p = {**p, "w_in": w_in16}
        cols = Z_COLS_ALL
        out_names = ("w_a_out", "w_b_out", "w_c_out", "w_o")
        b_in, new_hist, *rounded = _conv_sample(z, hist, seq, *conv_args, n_seq=16,
                                               also_round=[p[name] for name in out_names])
        p = {**p, **dict(zip(out_names, rounded))}
    w_tiled, bias_rows = _sgu_operands(p["sgu_w"], p["sgu_b"], chunk)
    sgu_out = _sgu(z, cols, p["sgu_ln_g"], p["sgu_ln_b"], w_tiled, bias_rows, chunk=chunk, rows=512,
                   want_vn=want_vn)
    if hist is None:
        steps = seq // PROMPT_ATTN_ROWS
        kv_specs = [pl.BlockSpec((1, N_MEM, D_C), functools.partial(lambda i, col: (i // steps, 0, col), col=col))
                    for col in (0, 1)]
        c_in = _attn(z, cols, k, v, kv_specs, rows=PROMPT_ATTN_ROWS, n_seq=1)
    else:
        kv_specs = [pl.BlockSpec((SAMPLE_ATTN_SEQS, N_MEM * HEAD_ROW_PITCH, LANES), lambda i: (i, 0, 0))] * 2
        c_in = _attn(z, cols, k, v, kv_specs, rows=seq, n_seq=SAMPLE_ATTN_SEQS)
    x1, h2 = _merge(sgu_out[0], b_in, c_in, z, cols, x, p["b_gate"], p["w_a_out"], p["w_b_out"], p["w_c_out"],
                    p["w_o"], p["mlp_norm_g"], tm=256)
    if p["w_up"].dtype == F32:
        d, w_up16, w_down16 = _mlp_first(h2, p["w_up"], p["w_down"], tf=512)
        p = {**p, "w_up": w_up16, "w_down": w_down16}
        y = _residual_norm(x1, d, p["final_norm_g"], tm=256)
    else:
        y = _mlp(h2, x1, p["w_up"], p["w_down"], p["final_norm_g"], tm=512, tf=1024)
    return y, new_hist, (sgu_out[1] if want_vn else None), p


def kernel(x_prompt, x_sample, mem_prompt, cache_mem_k, cache_mem_v, state_conv, attn_norm_g, w_in, b_gate,
           sgu_ln_g, sgu_ln_b, sgu_w, sgu_b, w_a_out, conv_w, conv_b, conv_ln_g, conv_ln_b, w_b_out, mem_norm_g,
           w_mem_kv, w_c_out, w_o, mlp_norm_g, w_up, w_down, final_norm_g):
    assert attn_norm_g.shape[0] == 1, "single-layer trunk"
    bp, sp, _ = x_prompt.shape
    bs, ss, _ = x_sample.shape
    p = dict(
        attn_norm_g=attn_norm_g[0], w_in=w_in[0], b_gate=b_gate[0], sgu_ln_g=sgu_ln_g[0],
        sgu_ln_b=sgu_ln_b[0], sgu_w=sgu_w[0], sgu_b=sgu_b[0], w_a_out=w_a_out[0],
        conv_w=conv_w[0], conv_b=conv_b[0], conv_ln_g=conv_ln_g[0], conv_ln_b=conv_ln_b[0],
        w_b_out=w_b_out[0], w_c_out=w_c_out[0], w_o=w_o[0],
        mlp_norm_g=mlp_norm_g[0], w_up=w_up[0], w_down=w_down[0],
        final_norm_g=final_norm_g,
    )
    kv = _norm_matmul(mem_prompt.reshape(bp * N_MEM, D_MODEL), mem_norm_g[0], w_mem_kv[0],
                      tm=1024, tn=512).reshape(bp, N_MEM, 2 * D_C)
    y_s, hist_s, vn_s, p = _layer(x_sample.reshape(bs * ss, D_MODEL), ss, _split_lane_tiles(cache_mem_k[0]),
                                  _split_lane_tiles(cache_mem_v[0]), state_conv[0], p, want_vn=True)
    y_p, hist_p, _, _ = _layer(x_prompt.reshape(bp * sp, D_MODEL), sp, kv, kv, None, p, want_vn=False)
    head_shape = (1, bp, N_MEM, N_MEM_HEADS, MEM_HEAD_DIM)
    return (
        y_p.reshape(bp, sp, D_MODEL),
        y_s.reshape(bs, ss, D_MODEL),
        kv[:, :, :D_C].reshape(head_shape),
        kv[:, :, D_C:].reshape(head_shape),
        hist_p[None],
        hist_s[None],
        vn_s.reshape(1, bs, ss, D_A),
    )
```

```python
import functools
import math
from typing import NamedTuple

import jax
import jax.numpy as jnp
from jax import lax
from jax.experimental import pallas as pl
from jax.experimental.pallas import tpu as pltpu

F32 = jnp.float32
BF16 = jnp.bfloat16

D_MODEL = 2048
D_A = 1024
D_B = 1024
D_C = 1024
CHUNK = 128
SGU_GROUPS = 8
SGU_HEAD = D_A // SGU_GROUPS
CONV_WIDTH = 31
HIST = CONV_WIDTH - 1
N_MEM = 256
N_MEM_HEADS = 4
MEM_HEAD_DIM = D_C // N_MEM_HEADS
N_BRANCH = 3
D_FF = 4 * D_MODEL
N_IN = 2 * D_A + 2 * D_B + D_C + N_BRANCH * D_MODEL
EPS = 1e-6

COL_U, COL_V, COL_GLU_A, COL_GLU_B, COL_Q, COL_GATES = 0, 1, 2, 3, 4, 5
COL_W = 1024


class ZCols(NamedTuple):
    u: int
    v: int
    q: int
    gates: int


def _col_block_spec(z, rows, col):
    if z.ndim == 3:
        return pl.BlockSpec((None, rows, COL_W), lambda i: (col, i, 0))
    return pl.BlockSpec((rows, COL_W), lambda i: (i, col))


Z_COLS_ALL = ZCols(COL_U, COL_V, COL_Q, COL_GATES)
Z_COLS_NO_GLU = ZCols(COL_U, COL_V, COL_Q - 2, COL_GATES - 2)

V7X_VMEM_BYTES = 64 * 1024 * 1024
VMEM_LIMIT_BYTES = V7X_VMEM_BYTES - 8 * 1024 * 1024
SUBLANES = 8
LANES = 128
V7X_MXU_COLS = 256
PROJ_LANES = V7X_MXU_COLS
PROJ_ROWS = 256
HIST_PAD = 32


def _cparams(*semantics, flags=None):
    return pltpu.CompilerParams(dimension_semantics=semantics, vmem_limit_bytes=VMEM_LIMIT_BYTES, flags=flags)


def _not_before(value, anchor):
    tile = anchor[-SUBLANES:, -LANES:]
    zero = pltpu.bitcast((pltpu.bitcast(tile, jnp.uint32) >> 16) >> 16, F32)
    reps = (value.shape[0] // SUBLANES, value.shape[1] // LANES)
    return value + jnp.tile(zero, reps)


def _rms_norm(x, g):
    ms = jnp.mean(x * x, axis=-1, keepdims=True)
    return x * lax.rsqrt(ms + EPS) * g


def _layer_norm(x, g, b):
    mu = jnp.mean(x, axis=-1, keepdims=True)
    xc = x - mu
    var = jnp.mean(xc * xc, axis=-1, keepdims=True)
    return xc * lax.rsqrt(var + EPS) * g + b


NORM_ROWS = 128


def _norm_matmul_kernel(x_ref, g_ref, w_ref, o_ref, *rest):
    *w16_refs, h_ref = rest

    @pl.when(pl.program_id(1) == 0)
    def _():
        def body(r, carry):
            rows = pl.ds(pl.multiple_of(r * NORM_ROWS, NORM_ROWS), NORM_ROWS)
            h_ref[rows, :] = _rms_norm(x_ref[rows, :], g_ref[...]).astype(BF16)
            return carry

        lax.fori_loop(0, x_ref.shape[0] // NORM_ROWS, body, 0)

    for c in range(0, w_ref.shape[1], PROJ_LANES):
        w16 = w_ref[:, c : c + PROJ_LANES].astype(BF16)
        if w16_refs:
            w16_refs[0][:, c : c + PROJ_LANES] = w16
        o_ref[:, c : c + PROJ_LANES] = jnp.dot(h_ref[...], w16, preferred_element_type=F32)


def _norm_matmul(x, g, w, *, tm, tn, emit_w16=False):
    m, k = x.shape
    n = w.shape[1]
    out_shape = [jax.ShapeDtypeStruct((m, n), F32)]
    out_specs = [pl.BlockSpec((tm, tn), lambda i, j: (i, j))]
    if emit_w16:
        assert m == tm, "every weight block must be visited exactly once"
        per_col = COL_W // tn
        out_shape.append(jax.ShapeDtypeStruct((n // COL_W, k, COL_W), BF16))
        out_specs.append(pl.BlockSpec((None, k, tn), lambda i, j: (j // per_col, 0, j % per_col)))
    outs = pl.pallas_call(
        _norm_matmul_kernel,
        out_shape=out_shape,
        grid=(m // tm, n // tn),
        in_specs=[
            pl.BlockSpec((tm, k), lambda i, j: (i, 0)),
            pl.BlockSpec((1, k), lambda i, j: (0, 0)),
            pl.BlockSpec((k, tn), lambda i, j: (0, j)),
        ],
        out_specs=out_specs,
        scratch_shapes=[pltpu.VMEM((tm, k), BF16)],
        compiler_params=_cparams("arbitrary", "arbitrary"),
        name="norm_matmul",
    )(x, g.reshape(1, k), w)
    return outs if emit_w16 else outs[0]


def _lane_periodic(x, period):
    lane = lax.broadcasted_iota(jnp.int32, x.shape, 1)
    y = jnp.where(lane < period, x, 0.0)
    while period < LANES:
        y = y + pltpu.roll(y, period, axis=1)
        period *= 2
    return y


def _sgu_kernel(u_ref, v_ref, g_ref, b_ref, w_ref, sb_ref, a_ref, *vn_refs, chunk):
    r = lax.broadcasted_iota(jnp.int32, (CHUNK, CHUNK), 0)
    c = lax.broadcasted_iota(jnp.int32, (CHUNK, CHUNK), 1)
    seg_bits = chunk.bit_length() - 1
    same_segment = (r >> seg_bits) == (c >> seg_bits)
    mask = jnp.logical_and(same_segment, (r & (chunk - 1)) >= (c & (chunk - 1)))
    sb = sb_ref[...] if chunk == CHUNK else _lane_periodic(sb_ref[...], chunk)
    w_s, bias = [], []
    for grp in range(SGU_GROUPS):
        if chunk == CHUNK:
            w = w_ref[grp]
        else:
            w = _lane_periodic(jnp.tile(w_ref[grp, 0:chunk, :], (CHUNK // chunk, 1)), chunk)
        w_s.append(jnp.where(mask, w, 0.0).astype(BF16))
        col = jnp.sum(jnp.where(r == c, sb[grp : grp + 1, :], 0.0), axis=1, keepdims=True)
        bias.append(jnp.broadcast_to(col, (CHUNK, SGU_HEAD)))
    for blk in range(u_ref.shape[0] // CHUNK):
        rows = slice(blk * CHUNK, (blk + 1) * CHUNK)
        vn = _layer_norm(jax.nn.gelu(v_ref[rows, :]), g_ref[...], b_ref[...])
        if vn_refs:
            vn_refs[0][rows, :] = vn
        vn16 = vn.astype(BF16)
        for grp in range(SGU_GROUPS):
            cols = slice(grp * SGU_HEAD, (grp + 1) * SGU_HEAD)
            mixed = jnp.dot(w_s[grp], vn16[:, cols], preferred_element_type=F32) + bias[grp]
            a_ref[rows, cols] = (jax.nn.gelu(u_ref[rows, cols]) * mixed).astype(BF16)


def _sgu(z, cols, ln_g, ln_b, sgu_w, sgu_b, *, chunk, rows, want_vn):
    m = z.shape[-2]
    out_shape = [jax.ShapeDtypeStruct((m, D_A), BF16)]
    out_specs = [pl.BlockSpec((rows, D_A), lambda i: (i, 0))]
    if want_vn:
        out_shape.append(jax.ShapeDtypeStruct((m, D_A), F32))
        out_specs.append(pl.BlockSpec((rows, D_A), lambda i: (i, 0)))
    return pl.pallas_call(
        functools.partial(_sgu_kernel, chunk=chunk),
        out_shape=out_shape,
        grid=(m // rows,),
        in_specs=[
            _col_block_spec(z, rows, cols.u),
            _col_block_spec(z, rows, cols.v),
            pl.BlockSpec((1, D_A), lambda i: (0, 0)),
            pl.BlockSpec((1, D_A), lambda i: (0, 0)),
            pl.BlockSpec((SGU_GROUPS, CHUNK, CHUNK), lambda i: (0, 0, 0)),
            pl.BlockSpec((SGU_GROUPS, CHUNK), lambda i: (0, 0)),
        ],
        out_specs=out_specs,
        compiler_params=_cparams("arbitrary"),
        name="sgu",
    )(z, z, ln_g.reshape(1, D_A), ln_b.reshape(1, D_A), sgu_w, sgu_b)


CONV_ROWS = 32
CONV_LANES = 128
CONV_LEAD = HIST_PAD - HIST


def _conv_window(win, w_ref, lanes, n_rows):
    out = None
    for b in range(SUBLANES):
        rows_b = n_rows if b == 0 else n_rows + SUBLANES
        y = None
        for a in range(HIST_PAD // SUBLANES + 1):
            k = SUBLANES * a + b - CONV_LEAD
            if 0 <= k < CONV_WIDTH:
                term = w_ref[k : k + 1, lanes] * win[SUBLANES * a : SUBLANES * a + rows_b, :]
                y = term if y is None else y + term
        shifted = y[b : b + n_rows, :]
        out = shifted if out is None else out + shifted
    return out


CONV_STEP_ROWS = 128
CONV_LEAD_PIECES = 2
N_GLU_STEPS = 2
N_COLS = N_IN // COL_W


def _glu_first(j):
    return jnp.where(j < N_GLU_STEPS, j + COL_GLU_A, jnp.where(j < COL_GLU_A + N_GLU_STEPS, j - N_GLU_STEPS, j))


def _inproj_conv_kernel(x_ref, g_ref, w_ref, cw_ref, cb_ref, lg_ref, lb_ref, z_ref, bin_ref, hist_ref,
                        h_ref, ext_ref, *, tiles_per_seq):
    i = pl.program_id(0)
    j = pl.program_id(1)
    t = x_ref.shape[0]

    def proj():
        return jnp.dot(h_ref[...], w_ref[...], preferred_element_type=F32)

    @pl.when(j == 0)
    def _():
        @pl.when(i % tiles_per_seq == 0)
        def _():
            ext_ref[0:HIST_PAD, :] = jnp.zeros((HIST_PAD, D_B), F32)

        @pl.when(i % tiles_per_seq != 0)
        def _():
            ext_ref[0:HIST_PAD, :] = ext_ref[t : t + HIST_PAD, :]

        for rb in range(t // PROJ_ROWS):
            for r in range(rb * PROJ_ROWS, (rb + 1) * PROJ_ROWS, NORM_ROWS):
                h_ref[r : r + NORM_ROWS, :] = _rms_norm(x_ref[r : r + NORM_ROWS, :], g_ref[...]).astype(BF16)
            rows = slice(rb * PROJ_ROWS, (rb + 1) * PROJ_ROWS)
            ext_ref[HIST_PAD + rb * PROJ_ROWS : HIST_PAD + (rb + 1) * PROJ_ROWS, :] = jnp.dot(
                h_ref[rows, :], w_ref[...], preferred_element_type=F32)

    @pl.when(j == 1)
    def _():
        ext_ref[HIST_PAD : HIST_PAD + t, :] = ext_ref[HIST_PAD : HIST_PAD + t, :] * jax.nn.sigmoid(proj())
        hist_ref[0] = ext_ref[t + CONV_LEAD : t + HIST_PAD, :]

    @pl.when(j >= N_GLU_STEPS)
    def _():
        chunk = jnp.minimum(j - N_GLU_STEPS, t // CONV_STEP_ROWS - 1)
        r0 = pl.multiple_of(chunk * CONV_STEP_ROWS, CONV_STEP_ROWS)
        n_row_blocks = CONV_STEP_ROWS // CONV_ROWS
        n_lane_blocks = D_B // PROJ_LANES
        rows_per_block = t // n_row_blocks
        pieces = [(rc, lb) for rc in range(n_row_blocks) for lb in range(n_lane_blocks)]
        vec_done, proj_done = [], []
        parts = []

        def conv_piece(p):
            rc, lb = pieces[p]
            wins = []
            for cl in range(lb * PROJ_LANES, (lb + 1) * PROJ_LANES, CONV_LANES):
                clanes = slice(cl, cl + CONV_LANES)
                win = ext_ref[pl.ds(r0 + rc * CONV_ROWS, CONV_ROWS + HIST_PAD), clanes]
                if p > CONV_LEAD_PIECES:
                    win = _not_before(win, proj_done[p - CONV_LEAD_PIECES - 1])
                wins.append(_conv_window(win, cw_ref, clanes, CONV_ROWS))
            parts.append(jnp.concatenate(wins, axis=-1))
            vec_done.append(parts[-1])
            if lb == n_lane_blocks - 1:
                dc = jnp.concatenate(parts, axis=-1) + cb_ref[...]
                parts.clear()
                y = jax.nn.silu(_layer_norm(dc, lg_ref[...], lb_ref[...]))
                bin_ref[pl.ds(r0 + rc * CONV_ROWS, CONV_ROWS), :] = y.astype(BF16)
                vec_done.append(y)

        n_conv = 0
        for q, (rc, lb) in enumerate(pieces):
            while n_conv < min(q + CONV_LEAD_PIECES + 1, len(pieces)):
                conv_piece(n_conv)
                n_conv += 1
            rows = slice(rc * rows_per_block, (rc + 1) * rows_per_block)
            lanes = slice(lb * PROJ_LANES, (lb + 1) * PROJ_LANES)
            zp = jnp.dot(h_ref[rows, :], w_ref[:, lanes], preferred_element_type=F32)
            if len(vec_done) >= 2:
                zp = _not_before(zp, vec_done[-2])
            z_ref[rows, lanes] = zp
            proj_done.append(zp)


def _inproj_conv(x, seq, g, w16, conv_w, conv_b, ln_g, ln_b, *, tm):
    m, k = x.shape
    tiles_per_seq = seq // tm
    n_conv_steps = N_COLS - N_GLU_STEPS
    assert tm // CONV_STEP_ROWS <= n_conv_steps, "not enough grid steps to convolve the whole tile"
    const = lambda i, j: (0, 0)
    return pl.pallas_call(
        functools.partial(_inproj_conv_kernel, tiles_per_seq=tiles_per_seq),
        out_shape=[
            jax.ShapeDtypeStruct((n_conv_steps, m, COL_W), F32),
            jax.ShapeDtypeStruct((m, D_B), BF16),
            jax.ShapeDtypeStruct((m // seq, HIST, D_B), F32),
        ],
        grid=(m // tm, N_COLS),
        in_specs=[
            pl.BlockSpec((tm, k), lambda i, j: (i, 0)),
            pl.BlockSpec((1, k), const),
            pl.BlockSpec((None, k, COL_W), lambda i, j: (_glu_first(j), 0, 0)),
            pl.BlockSpec((CONV_WIDTH, D_B), const),
            pl.BlockSpec((1, D_B), const),
            pl.BlockSpec((1, D_B), const),
            pl.BlockSpec((1, D_B), const),
        ],
        out_specs=[
            pl.BlockSpec((None, tm, COL_W), lambda i, j: (jnp.maximum(j - N_GLU_STEPS, 0), i, 0)),
            pl.BlockSpec((tm, D_B), lambda i, j: (i, 0)),
            pl.BlockSpec((1, HIST, D_B), lambda i, j: (i // tiles_per_seq, 0, 0)),
        ],
        scratch_shapes=[pltpu.VMEM((tm, k), BF16), pltpu.VMEM((tm + HIST_PAD, D_B), F32)],
        compiler_params=_cparams("arbitrary", "arbitrary", ),
        name="inproj_conv",
    )(x, g.reshape(1, k), w16, conv_w, conv_b.reshape(1, D_B), ln_g.reshape(1, D_B), ln_b.reshape(1, D_B))


def _conv_sample_kernel(*refs, seq, n_round):
    ga_ref, gb_ref, hist_ref, w_ref, cb_ref, g_ref, b_ref = refs[:7]
    f32_refs = refs[7 : 7 + n_round]
    o_ref, nh_ref = refs[7 + n_round : 9 + n_round]
    bf16_refs = refs[9 + n_round : 9 + 2 * n_round]
    cs_ref, dc_ref, os_ref = refs[9 + 2 * n_round :]
    for src, dst in zip(f32_refs, bf16_refs):
        dst[...] = src[...].astype(BF16)
    n_seq = hist_ref.shape[1]
    n_slabs = D_B // LANES
    c = ga_ref[...] * jax.nn.sigmoid(gb_ref[...])
    for l in range(n_slabs):
        cs_ref[l] = c[:, l * LANES : (l + 1) * LANES]
    new = [jnp.concatenate([cs_ref[l, pl.ds(t, n_seq, stride=seq), :] for l in range(n_slabs)], axis=-1)
           for t in range(seq)]
    nh_ref[0 : HIST - seq] = hist_ref[seq:HIST]
    for t in range(seq):
        nh_ref[HIST - seq + t] = new[t]
    for lb in range(D_B // CONV_LANES):
        lanes = slice(lb * CONV_LANES, (lb + 1) * CONV_LANES)
        ext = [hist_ref[j, :, lanes] for j in range(HIST)] + [new[t][:, lanes] for t in range(seq)]
        for t in range(seq):
            acc = w_ref[0:1, lanes] * ext[t]
            for k in range(1, CONV_WIDTH):
                acc = acc + w_ref[k : k + 1, lanes] * ext[t + k]
            dc_ref[t, :, lanes] = acc
    for t in range(seq):
        y = jax.nn.silu(_layer_norm(dc_ref[t] + cb_ref[...], g_ref[...], b_ref[...]))
        for l in range(n_slabs):
            os_ref[l, pl.ds(t, n_seq, stride=seq), :] = y[:, l * LANES : (l + 1) * LANES]
    for l in range(n_slabs):
        o_ref[:, l * LANES : (l + 1) * LANES] = os_ref[l].astype(BF16)


def _conv_sample(z, hist, seq, conv_w, conv_b, ln_g, ln_b, *, n_seq, also_round=()):
    batch = hist.shape[1]
    steps = batch // n_seq
    slab_specs = [pl.BlockSpec((w.shape[0] // steps, w.shape[1]), lambda i: (i, 0)) for w in also_round]
    slab_scratch = pltpu.VMEM((D_B // LANES, n_seq * seq, LANES), F32)
    return pl.pallas_call(
        functools.partial(_conv_sample_kernel, seq=seq, n_round=len(also_round)),
        out_shape=[
            jax.ShapeDtypeStruct((batch * seq, D_B), BF16),
            jax.ShapeDtypeStruct((HIST, batch, D_B), F32),
        ] + [jax.ShapeDtypeStruct(w.shape, BF16) for w in also_round],
        grid=(steps,),
        in_specs=[
            pl.BlockSpec((n_seq * seq, COL_W), lambda i: (i, COL_GLU_A)),
            pl.BlockSpec((n_seq * seq, COL_W), lambda i: (i, COL_GLU_B)),
            pl.BlockSpec((HIST, n_seq, D_B), lambda i: (0, i, 0)),
            pl.BlockSpec((CONV_WIDTH, D_B), lambda i: (0, 0)),
            pl.BlockSpec((1, D_B), lambda i: (0, 0)),
            pl.BlockSpec((1, D_B), lambda i: (0, 0)),
            pl.BlockSpec((1, D_B), lambda i: (0, 0)),
        ] + slab_specs,
        out_specs=[
            pl.BlockSpec((n_seq * seq, D_B), lambda i: (i, 0)),
            pl.BlockSpec((HIST, n_seq, D_B), lambda i: (0, i, 0)),
        ] + slab_specs,
        scratch_shapes=[slab_scratch, pltpu.VMEM((seq, n_seq, D_B), F32), slab_scratch],
        compiler_params=_cparams("arbitrary"),
        name="conv_sample",
    )(z, z, hist, conv_w, conv_b.reshape(1, D_B), ln_g.reshape(1, D_B), ln_b.reshape(1, D_B), *also_round)


def _head_cols(h):
    return slice(h * MEM_HEAD_DIM, (h + 1) * MEM_HEAD_DIM)


HEAD_LANE_TILES = MEM_HEAD_DIM // LANES
HEAD_ROW_PITCH = N_MEM_HEADS * HEAD_LANE_TILES


def _split_lane_tiles(kv):
    b = kv.shape[0]
    kv = kv.reshape(b, N_MEM, N_MEM_HEADS, HEAD_LANE_TILES, LANES)
    return kv.transpose(0, 1, 3, 2, 4).reshape(b, N_MEM * HEAD_ROW_PITCH, LANES)


def _head_of(kv_ref, s, h):
    if kv_ref.shape[-1] == LANES:
        tiles = [kv_ref[s, pl.ds(j * N_MEM_HEADS + h, N_MEM, stride=HEAD_ROW_PITCH), :]
                 for j in range(HEAD_LANE_TILES)]
        return jnp.concatenate(tiles, axis=-1).astype(BF16)
    return kv_ref[s, :, _head_cols(h)].astype(BF16)


def _attn_kernel(q_ref, k_ref, v_ref, o_ref, *, rows):
    scale = 1.0 / math.sqrt(MEM_HEAD_DIM)
    pairs = [(s, h) for s in range(k_ref.shape[0]) for h in range(N_MEM_HEADS)]
    scores = []
    for s, h in pairs:
        q = (q_ref[s * rows : (s + 1) * rows, _head_cols(h)] * scale).astype(BF16)
        scores.append(lax.dot_general(q, _head_of(k_ref, s, h), (((1,), (1,)), ((), ())),
                                      preferred_element_type=F32))
    sc = jnp.concatenate(scores, axis=0)
    p = jnp.exp(sc - jnp.max(sc, axis=-1, keepdims=True))
    p = p / jnp.sum(p, axis=-1, keepdims=True)
    for n, (s, h) in enumerate(pairs):
        ph = p[n * rows : (n + 1) * rows, :].astype(BF16)
        o = jnp.dot(ph, _head_of(v_ref, s, h), preferred_element_type=F32)
        o_ref[s * rows : (s + 1) * rows, _head_cols(h)] = o.astype(BF16)


def _attn(z, cols, k, v, kv_specs, *, rows, n_seq):
    m = z.shape[-2]
    return pl.pallas_call(
        functools.partial(_attn_kernel, rows=rows),
        out_shape=jax.ShapeDtypeStruct((m, D_C), BF16),
        grid=(m // (rows * n_seq),),
        in_specs=[_col_block_spec(z, rows * n_seq, cols.q)] + kv_specs,
        out_specs=pl.BlockSpec((rows * n_seq, D_C), lambda i: (i, 0)),
        compiler_params=_cparams("arbitrary"),
        name="attn",
    )(z, k, v)


def _merge_kernel(a_ref, b_ref, c_ref, *refs):
    gate_refs = refs[: 2 * N_BRANCH]
    bg_ref, x_ref, wa_ref, wb_ref, wc_ref, wo_ref, mg_ref, x1_ref, h2_ref = refs[2 * N_BRANCH :]
    halves = []
    for half in range(2):
        cols = slice(half * COL_W, (half + 1) * COL_W)
        merged = None
        for br, (in_ref, w_ref) in enumerate(((a_ref, wa_ref), (b_ref, wb_ref), (c_ref, wc_ref))):
            y = jnp.dot(in_ref[...], w_ref[:, cols], preferred_element_type=F32)
            gcols = slice((2 * br + half) * COL_W, (2 * br + half + 1) * COL_W)
            term = jax.nn.sigmoid(gate_refs[2 * br + half][...] + bg_ref[:, gcols]) * y
            merged = term if merged is None else merged + term
        halves.append(merged.astype(BF16))
    merged16 = jnp.concatenate(halves, axis=-1)
    x1 = x_ref[...] + jnp.dot(merged16, wo_ref[...], preferred_element_type=F32)
    x1_ref[...] = x1
    h2_ref[...] = _rms_norm(x1, mg_ref[...]).astype(BF16)


def _merge(a_in, b_in, c_in, z, cols, x, b_gate, w_a, w_b, w_c, w_o, mlp_g, *, tm):
    m = x.shape[0]
    const = lambda i: (0, 0)
    resident = functools.partial(pl.BlockSpec, index_map=const, pipeline_mode=pl.Buffered(1))
    gate_specs = [_col_block_spec(z, tm, cols.gates + n) for n in range(2 * N_BRANCH)]
    return pl.pallas_call(
        _merge_kernel,
        out_shape=[jax.ShapeDtypeStruct((m, D_MODEL), F32), jax.ShapeDtypeStruct((m, D_MODEL), BF16)],
        grid=(m // tm,),
        in_specs=[pl.BlockSpec((tm, D_A), lambda i: (i, 0))] * 3
        + gate_specs
        + [
            pl.BlockSpec((1, N_BRANCH * D_MODEL), const),
            pl.BlockSpec((tm, D_MODEL), lambda i: (i, 0)),
            resident((D_A, D_MODEL)),
            resident((D_B, D_MODEL)),
            resident((D_C, D_MODEL)),
            resident((D_MODEL, D_MODEL)),
            pl.BlockSpec((1, D_MODEL), const),
        ],
        out_specs=[pl.BlockSpec((tm, D_MODEL), lambda i: (i, 0))] * 2,
        compiler_params=_cparams("arbitrary"),
        name="merge",
    )(a_in, b_in, c_in, *([z] * (2 * N_BRANCH)), b_gate.reshape(1, -1), x, w_a, w_b, w_c, w_o,
      mlp_g.reshape(1, D_MODEL))


def _mlp_kernel(h2_ref, x1_ref, wu_ref, wd_ref, fg_ref, y_ref, acc_ref):
    f = pl.program_id(1)

    @pl.when(f == 0)
    def _():
        acc_ref[...] = x1_ref[...]

    t = jnp.square(jnp.maximum(jnp.dot(h2_ref[...], wu_ref[...], preferred_element_type=F32), 0.0))
    acc_ref[...] += jnp.dot(t.astype(BF16), wd_ref[...], preferred_element_type=F32)

    @pl.when(f == pl.num_programs(1) - 1)
    def _():
        y_ref[...] = _rms_norm(acc_ref[...], fg_ref[...])


def _mlp(h2, x1, w_up, w_down, final_g, *, tm, tf):
    m = h2.shape[0]
    return pl.pallas_call(
        _mlp_kernel,
        out_shape=jax.ShapeDtypeStruct((m, D_MODEL), F32),
        grid=(m // tm, D_FF // tf),
        in_specs=[
            pl.BlockSpec((tm, D_MODEL), lambda i, f: (i, 0)),
            pl.BlockSpec((tm, D_MODEL), lambda i, f: (i, 0)),
            pl.BlockSpec((D_MODEL, tf), lambda i, f: (0, f)),
            pl.BlockSpec((tf, D_MODEL), lambda i, f: (f, 0)),
            pl.BlockSpec((1, D_MODEL), lambda i, f: (0, 0)),
        ],
        out_specs=pl.BlockSpec((tm, D_MODEL), lambda i, f: (i, 0)),
        scratch_shapes=[pltpu.VMEM((tm, D_MODEL), F32)],
        compiler_params=_cparams("arbitrary", "arbitrary"),
        name="mlp",
    )(h2, x1, w_up, w_down, final_g.reshape(1, D_MODEL))


def _mlp_first_kernel(h2_ref, wu_ref, wd_ref, acc_ref, wu16_ref, wd16_ref):
    @pl.when(pl.program_id(0) == 0)
    def _():
        acc_ref[...] = jnp.zeros(acc_ref.shape, F32)

    h2 = h2_ref[...]
    t_pieces = []
    for c in range(0, wu_ref.shape[1], PROJ_LANES):
        wu = wu_ref[:, c : c + PROJ_LANES].astype(BF16)
        wu16_ref[:, c : c + PROJ_LANES] = wu
        t = jnp.square(jnp.maximum(jnp.dot(h2, wu, preferred_element_type=F32), 0.0))
        t_pieces.append(t.astype(BF16))
    t16 = jnp.concatenate(t_pieces, axis=-1)
    for c in range(0, wd_ref.shape[1], PROJ_LANES):
        wd = wd_ref[:, c : c + PROJ_LANES].astype(BF16)
        wd16_ref[:, c : c + PROJ_LANES] = wd
        acc_ref[:, c : c + PROJ_LANES] += jnp.dot(t16, wd, preferred_element_type=F32)


def _mlp_first(h2, w_up, w_down, *, tf):
    m = h2.shape[0]
    resident = pl.BlockSpec((m, D_MODEL), lambda f: (0, 0))
    return pl.pallas_call(
        _mlp_first_kernel,
        out_shape=[
            jax.ShapeDtypeStruct((m, D_MODEL), F32),
            jax.ShapeDtypeStruct((D_MODEL, D_FF), BF16),
            jax.ShapeDtypeStruct((D_FF, D_MODEL), BF16),
        ],
        grid=(D_FF // tf,),
        in_specs=[
            resident,
            pl.BlockSpec((D_MODEL, tf), lambda f: (0, f)),
            pl.BlockSpec((tf, D_MODEL), lambda f: (f, 0)),
        ],
        out_specs=[
            resident,
            pl.BlockSpec((D_MODEL, tf), lambda f: (0, f)),
            pl.BlockSpec((tf, D_MODEL), lambda f: (f, 0)),
        ],
        compiler_params=_cparams("arbitrary"),
        name="mlp_first",
    )(h2, w_up, w_down)


def _residual_norm_kernel(x_ref, d_ref, g_ref, y_ref):
    y_ref[...] = _rms_norm(x_ref[...] + d_ref[...], g_ref[...])


def _residual_norm(x, d, g, *, tm):
    m = x.shape[0]
    row_spec = pl.BlockSpec((tm, D_MODEL), lambda i: (i, 0))
    return pl.pallas_call(
        _residual_norm_kernel,
        out_shape=jax.ShapeDtypeStruct((m, D_MODEL), F32),
        grid=(m // tm,),
        in_specs=[row_spec, row_spec, pl.BlockSpec((1, D_MODEL), lambda i: (0, 0))],
        out_specs=row_spec,
        compiler_params=_cparams("arbitrary"),
        name="residual_norm",
    )(x, d, g.reshape(1, D_MODEL))


PROMPT_ATTN_ROWS = 512
SAMPLE_ATTN_SEQS = 8


def _layer(x, seq, k, v, hist, p, *, want_vn):
    chunk = min(seq, CHUNK)
    conv_args = (p["conv_w"], p["conv_b"], p["conv_ln_g"], p["conv_ln_b"])
    if hist is None:
        assert p["w_in"].dtype == BF16, "the group with a single token tile goes first and rounds w_in"
        z, b_in, new_hist = _inproj_conv(x, seq, p["attn_norm_g"], p["w_in"], *conv_args, tm=1024)
        cols = Z_COLS_NO_GLU
    else:
        z, w_in16 = _norm_matmul(x, p["attn_norm_g"], p["w_in"], tm=1024, tn=512, emit_w16=True)
        p = {**p, "w_in": w_in16}
        cols = Z_COLS_ALL
        out_names = ("w_a_out", "w_b_out", "w_c_out", "w_o")
        b_in, new_hist, *rounded = _conv_sample(z, hist, seq, *conv_args, n_seq=16,
                                               also_round=[p[name] for name in out_names])
        p = {**p, **dict(zip(out_names, rounded))}
    sgu_out = _sgu(z, cols, p["sgu_ln_g"], p["sgu_ln_b"], p["sgu_w"], p["sgu_b"], chunk=chunk, rows=512,
                   want_vn=want_vn)
    if hist is None:
        steps = seq // PROMPT_ATTN_ROWS
        kv_specs = [pl.BlockSpec((1, N_MEM, D_C), functools.partial(lambda i, col: (i // steps, 0, col), col=col))
                    for col in (0, 1)]
        c_in = _attn(z, cols, k, v, kv_specs, rows=PROMPT_ATTN_ROWS, n_seq=1)
    else:
        kv_specs = [pl.BlockSpec((SAMPLE_ATTN_SEQS, N_MEM * HEAD_ROW_PITCH, LANES), lambda i: (i, 0, 0))] * 2
        c_in = _attn(z, cols, k, v, kv_specs, rows=seq, n_seq=SAMPLE_ATTN_SEQS)
    x1, h2 = _merge(sgu_out[0], b_in, c_in, z, cols, x, p["b_gate"], p["w_a_out"], p["w_b_out"], p["w_c_out"],
                    p["w_o"], p["mlp_norm_g"], tm=256)
    if p["w_up"].dtype == F32:
        d, w_up16, w_down16 = _mlp_first(h2, p["w_up"], p["w_down"], tf=512)
        p = {**p, "w_up": w_up16, "w_down": w_down16}
        y = _residual_norm(x1, d, p["final_norm_g"], tm=256)
    else:
        y = _mlp(h2, x1, p["w_up"], p["w_down"], p["final_norm_g"], tm=512, tf=1024)
    return y, new_hist, (sgu_out[1] if want_vn else None), p


def kernel(x_prompt, x_sample, mem_prompt, cache_mem_k, cache_mem_v, state_conv, attn_norm_g, w_in, b_gate,
           sgu_ln_g, sgu_ln_b, sgu_w, sgu_b, w_a_out, conv_w, conv_b, conv_ln_g, conv_ln_b, w_b_out, mem_norm_g,
           w_mem_kv, w_c_out, w_o, mlp_norm_g, w_up, w_down, final_norm_g):
    assert attn_norm_g.shape[0] == 1, "single-layer trunk"
    bp, sp, _ = x_prompt.shape
    bs, ss, _ = x_sample.shape
    p = dict(
        attn_norm_g=attn_norm_g[0], w_in=w_in[0], b_gate=b_gate[0], sgu_ln_g=sgu_ln_g[0],
        sgu_ln_b=sgu_ln_b[0], sgu_w=sgu_w[0], sgu_b=sgu_b[0], w_a_out=w_a_out[0],
        conv_w=conv_w[0], conv_b=conv_b[0], conv_ln_g=conv_ln_g[0], conv_ln_b=conv_ln_b[0],
        w_b_out=w_b_out[0], w_c_out=w_c_out[0], w_o=w_o[0],
        mlp_norm_g=mlp_norm_g[0], w_up=w_up[0], w_down=w_down[0],
        final_norm_g=final_norm_g,
    )
    kv = _norm_matmul(mem_prompt.reshape(bp * N_MEM, D_MODEL), mem_norm_g[0], w_mem_kv[0],
                      tm=1024, tn=512).reshape(bp, N_MEM, 2 * D_C)
    y_s, hist_s, vn_s, p = _layer(x_sample.reshape(bs * ss, D_MODEL), ss, _split_lane_tiles(cache_mem_k[0]),
                                  _split_lane_tiles(cache_mem_v[0]), jnp.transpose(state_conv[0], (1, 0, 2)), p,
                                  want_vn=True)
    y_p, hist_p, _, _ = _layer(x_prompt.reshape(bp * sp, D_MODEL), sp, kv, kv, None, p, want_vn=False)
    head_shape = (1, bp, N_MEM, N_MEM_HEADS, MEM_HEAD_DIM)
    return (
        y_p.reshape(bp, sp, D_MODEL),
        y_s.reshape(bs, ss, D_MODEL),
        kv[:, :, :D_C].reshape(head_shape),
        kv[:, :, D_C:].reshape(head_shape),
        hist_p[None],
        jnp.transpose(hist_s, (1, 0, 2))[None],
        vn_s.reshape(1, bs, ss, D_A),
    )
```

```python
import functools
import math
from typing import NamedTuple

import jax
import jax.numpy as jnp
from jax import lax
from jax.experimental import pallas as pl
from jax.experimental.pallas import tpu as pltpu

F32 = jnp.float32
BF16 = jnp.bfloat16

D_MODEL = 2048
D_A = 1024
D_B = 1024
D_C = 1024
CHUNK = 128
SGU_GROUPS = 8
SGU_HEAD = D_A // SGU_GROUPS
CONV_WIDTH = 31
HIST = CONV_WIDTH - 1
N_MEM = 256
N_MEM_HEADS = 4
MEM_HEAD_DIM = D_C // N_MEM_HEADS
N_BRANCH = 3
D_FF = 4 * D_MODEL
N_IN = 2 * D_A + 2 * D_B + D_C + N_BRANCH * D_MODEL
EPS = 1e-6

COL_U, COL_V, COL_GLU_A, COL_GLU_B, COL_Q, COL_GATES = 0, 1, 2, 3, 4, 5
COL_W = 1024


class ZCols(NamedTuple):
    u: int
    v: int
    q: int
    gates: int


def _col_block_spec(z, rows, col):
    if z.ndim == 3:
        return pl.BlockSpec((None, rows, COL_W), lambda i: (col, i, 0))
    return pl.BlockSpec((rows, COL_W), lambda i: (i, col))


Z_COLS_ALL = ZCols(COL_U, COL_V, COL_Q, COL_GATES)
Z_COLS_NO_GLU = ZCols(COL_U, COL_V, COL_Q - 2, COL_GATES - 2)

V7X_VMEM_BYTES = 64 * 1024 * 1024
VMEM_LIMIT_BYTES = V7X_VMEM_BYTES - 8 * 1024 * 1024
SUBLANES = 8
LANES = 128
HEAD_LANE_TILES = MEM_HEAD_DIM // LANES
HEAD_ROW_PITCH = N_MEM_HEADS * HEAD_LANE_TILES
V7X_MXU_COLS = 256
PROJ_LANES = V7X_MXU_COLS
PROJ_ROWS = 256
HIST_PAD = 32


def _cparams(*semantics, flags=None):
    return pltpu.CompilerParams(dimension_semantics=semantics, vmem_limit_bytes=VMEM_LIMIT_BYTES, flags=flags)


def _not_before(value, anchor):
    tile = anchor[-SUBLANES:, -LANES:]
    zero = pltpu.bitcast((pltpu.bitcast(tile, jnp.uint32) >> 16) >> 16, F32)
    reps = (value.shape[0] // SUBLANES, value.shape[1] // LANES)
    return value + jnp.tile(zero, reps)


def _rms_norm(x, g):
    ms = jnp.mean(x * x, axis=-1, keepdims=True)
    return x * lax.rsqrt(ms + EPS) * g


def _layer_norm(x, g, b):
    mu = jnp.mean(x, axis=-1, keepdims=True)
    xc = x - mu
    var = jnp.mean(xc * xc, axis=-1, keepdims=True)
    return xc * lax.rsqrt(var + EPS) * g + b


NORM_ROWS = 128


def _norm_matmul_kernel(x_ref, g_ref, w_ref, o_ref, *rest, emit_w16, emit_head_split):
    *extra, h_ref = rest
    w16_ref = extra.pop(0) if emit_w16 else None
    split_ref = extra.pop(0) if emit_head_split else None

    @pl.when(pl.program_id(1) == 0)
    def _():
        def body(r, carry):
            rows = pl.ds(pl.multiple_of(r * NORM_ROWS, NORM_ROWS), NORM_ROWS)
            h_ref[rows, :] = _rms_norm(x_ref[rows, :], g_ref[...]).astype(BF16)
            return carry

        lax.fori_loop(0, x_ref.shape[0] // NORM_ROWS, body, 0)

    tn = w_ref.shape[1]
    for c in range(0, tn, PROJ_LANES):
        w16 = w_ref[:, c : c + PROJ_LANES].astype(BF16)
        if emit_w16:
            w16_ref[:, c : c + PROJ_LANES] = w16
        piece = jnp.dot(h_ref[...], w16, preferred_element_type=F32)
        o_ref[:, c : c + PROJ_LANES] = piece
        if emit_head_split:
            head = ((pl.program_id(1) * tn + c) % D_C) // MEM_HEAD_DIM
            for lt in range(HEAD_LANE_TILES):
                split_ref[pl.ds(lt * N_MEM_HEADS + head, piece.shape[0], stride=HEAD_ROW_PITCH), :] = (
                    piece[:, lt * LANES : (lt + 1) * LANES])


def _norm_matmul(x, g, w, *, tm, tn, emit_w16=False, emit_head_split=False):
    m, k = x.shape
    n = w.shape[1]
    out_shape = [jax.ShapeDtypeStruct((m, n), F32)]
    out_specs = [pl.BlockSpec((tm, tn), lambda i, j: (i, j))]
    if emit_w16:
        assert m == tm, "every weight block must be visited exactly once"
        per_col = COL_W // tn
        out_shape.append(jax.ShapeDtypeStruct((n // COL_W, k, COL_W), BF16))
        out_specs.append(pl.BlockSpec((None, k, tn), lambda i, j: (j // per_col, 0, j % per_col)))
    if emit_head_split:
        assert m == tm and PROJ_LANES == MEM_HEAD_DIM and n == 2 * D_C
        out_shape.append(jax.ShapeDtypeStruct((n // D_C, m * HEAD_ROW_PITCH, LANES), F32))
        out_specs.append(pl.BlockSpec((None, m * HEAD_ROW_PITCH, LANES), lambda i, j: (j // (D_C // tn), 0, 0)))
    outs = pl.pallas_call(
        functools.partial(_norm_matmul_kernel, emit_w16=emit_w16, emit_head_split=emit_head_split),
        out_shape=out_shape,
        grid=(m // tm, n // tn),
        in_specs=[
            pl.BlockSpec((tm, k), lambda i, j: (i, 0)),
            pl.BlockSpec((1, k), lambda i, j: (0, 0)),
            pl.BlockSpec((k, tn), lambda i, j: (0, j)),
        ],
        out_specs=out_specs,
        scratch_shapes=[pltpu.VMEM((tm, k), BF16)],
        compiler_params=_cparams("arbitrary", "arbitrary"),
        name="norm_matmul",
    )(x, g.reshape(1, k), w)
    return outs if len(outs) > 1 else outs[0]


def _lane_periodic(x, period):
    lane = lax.broadcasted_iota(jnp.int32, x.shape, 1)
    y = jnp.where(lane < period, x, 0.0)
    while period < LANES:
        y = y + pltpu.roll(y, period, axis=1)
        period *= 2
    return y


def _sgu_kernel(u_ref, v_ref, g_ref, b_ref, w_ref, sb_ref, a_ref, *vn_refs, chunk):
    r = lax.broadcasted_iota(jnp.int32, (CHUNK, CHUNK), 0)
    c = lax.broadcasted_iota(jnp.int32, (CHUNK, CHUNK), 1)
    seg_bits = chunk.bit_length() - 1
    same_segment = (r >> seg_bits) == (c >> seg_bits)
    mask = jnp.logical_and(same_segment, (r & (chunk - 1)) >= (c & (chunk - 1)))
    sb = sb_ref[...] if chunk == CHUNK else _lane_periodic(sb_ref[...], chunk)
    w_s, bias = [], []
    for grp in range(SGU_GROUPS):
        if chunk == CHUNK:
            w = w_ref[grp]
        else:
            w = _lane_periodic(jnp.tile(w_ref[grp, 0:chunk, :], (CHUNK // chunk, 1)), chunk)
        w_s.append(jnp.where(mask, w, 0.0).astype(BF16))
        col = jnp.sum(jnp.where(r == c, sb[grp : grp + 1, :], 0.0), axis=1, keepdims=True)
        bias.append(jnp.broadcast_to(col, (CHUNK, SGU_HEAD)))
    for blk in range(u_ref.shape[0] // CHUNK):
        rows = slice(blk * CHUNK, (blk + 1) * CHUNK)
        vn = _layer_norm(jax.nn.gelu(v_ref[rows, :]), g_ref[...], b_ref[...])
        if vn_refs:
            vn_refs[0][rows, :] = vn
        vn16 = vn.astype(BF16)
        for grp in range(SGU_GROUPS):
            cols = slice(grp * SGU_HEAD, (grp + 1) * SGU_HEAD)
            mixed = jnp.dot(w_s[grp], vn16[:, cols], preferred_element_type=F32) + bias[grp]
            a_ref[rows, cols] = (jax.nn.gelu(u_ref[rows, cols]) * mixed).astype(BF16)


def _sgu(z, cols, ln_g, ln_b, sgu_w, sgu_b, *, chunk, rows, want_vn):
    m = z.shape[-2]
    out_shape = [jax.ShapeDtypeStruct((m, D_A), BF16)]
    out_specs = [pl.BlockSpec((rows, D_A), lambda i: (i, 0))]
    if want_vn:
        out_shape.append(jax.ShapeDtypeStruct((m, D_A), F32))
        out_specs.append(pl.BlockSpec((rows, D_A), lambda i: (i, 0)))
    return pl.pallas_call(
        functools.partial(_sgu_kernel, chunk=chunk),
        out_shape=out_shape,
        grid=(m // rows,),
        in_specs=[
            _col_block_spec(z, rows, cols.u),
            _col_block_spec(z, rows, cols.v),
            pl.BlockSpec((1, D_A), lambda i: (0, 0)),
            pl.BlockSpec((1, D_A), lambda i: (0, 0)),
            pl.BlockSpec((SGU_GROUPS, CHUNK, CHUNK), lambda i: (0, 0, 0)),
            pl.BlockSpec((SGU_GROUPS, CHUNK), lambda i: (0, 0)),
        ],
        out_specs=out_specs,
        compiler_params=_cparams("arbitrary"),
        name="sgu",
    )(z, z, ln_g.reshape(1, D_A), ln_b.reshape(1, D_A), sgu_w, sgu_b)


CONV_ROWS = 32
CONV_LANES = 128
CONV_LEAD = HIST_PAD - HIST


def _conv_window(win, w_ref, lanes, n_rows):
    out = None
    for b in range(SUBLANES):
        rows_b = n_rows if b == 0 else n_rows + SUBLANES
        y = None
        for a in range(HIST_PAD // SUBLANES + 1):
            k = SUBLANES * a + b - CONV_LEAD
            if 0 <= k < CONV_WIDTH:
                term = w_ref[k : k + 1, lanes] * win[SUBLANES * a : SUBLANES * a + rows_b, :]
                y = term if y is None else y + term
        shifted = y[b : b + n_rows, :]
        out = shifted if out is None else out + shifted
    return out


CONV_STEP_ROWS = 128
CONV_LEAD_PIECES = 2
N_GLU_STEPS = 2
N_COLS = N_IN // COL_W


def _glu_first(j):
    return jnp.where(j < N_GLU_STEPS, j + COL_GLU_A, jnp.where(j < COL_GLU_A + N_GLU_STEPS, j - N_GLU_STEPS, j))


def _inproj_conv_kernel(x_ref, g_ref, w_ref, cw_ref, cb_ref, lg_ref, lb_ref, z_ref, bin_ref, hist_ref,
                        h_ref, ext_ref, *, tiles_per_seq):
    i = pl.program_id(0)
    j = pl.program_id(1)
    t = x_ref.shape[0]

    def proj():
        return jnp.dot(h_ref[...], w_ref[...], preferred_element_type=F32)

    @pl.when(j == 0)
    def _():
        @pl.when(i % tiles_per_seq == 0)
        def _():
            ext_ref[0:HIST_PAD, :] = jnp.zeros((HIST_PAD, D_B), F32)

        @pl.when(i % tiles_per_seq != 0)
        def _():
            ext_ref[0:HIST_PAD, :] = ext_ref[t : t + HIST_PAD, :]

        for rb in range(t // PROJ_ROWS):
            for r in range(rb * PROJ_ROWS, (rb + 1) * PROJ_ROWS, NORM_ROWS):
                h_ref[r : r + NORM_ROWS, :] = _rms_norm(x_ref[r : r + NORM_ROWS, :], g_ref[...]).astype(BF16)
            rows = slice(rb * PROJ_ROWS, (rb + 1) * PROJ_ROWS)
            ext_ref[HIST_PAD + rb * PROJ_ROWS : HIST_PAD + (rb + 1) * PROJ_ROWS, :] = jnp.dot(
                h_ref[rows, :], w_ref[...], preferred_element_type=F32)

    @pl.when(j == 1)
    def _():
        ext_ref[HIST_PAD : HIST_PAD + t, :] = ext_ref[HIST_PAD : HIST_PAD + t, :] * jax.nn.sigmoid(proj())
        hist_ref[0] = ext_ref[t + CONV_LEAD : t + HIST_PAD, :]

    @pl.when(j >= N_GLU_STEPS)
    def _():
        chunk = jnp.minimum(j - N_GLU_STEPS, t // CONV_STEP_ROWS - 1)
        r0 = pl.multiple_of(chunk * CONV_STEP_ROWS, CONV_STEP_ROWS)
        n_row_blocks = CONV_STEP_ROWS // CONV_ROWS
        n_lane_blocks = D_B // PROJ_LANES
        rows_per_block = t // n_row_blocks
        pieces = [(rc, lb) for rc in range(n_row_blocks) for lb in range(n_lane_blocks)]
        vec_done, proj_done = [], []
        parts = []

        def conv_piece(p):
            rc, lb = pieces[p]
            wins = []
            for cl in range(lb * PROJ_LANES, (lb + 1) * PROJ_LANES, CONV_LANES):
                clanes = slice(cl, cl + CONV_LANES)
                win = ext_ref[pl.ds(r0 + rc * CONV_ROWS, CONV_ROWS + HIST_PAD), clanes]
                if p > CONV_LEAD_PIECES:
                    win = _not_before(win, proj_done[p - CONV_LEAD_PIECES - 1])
                wins.append(_conv_window(win, cw_ref, clanes, CONV_ROWS))
            parts.append(jnp.concatenate(wins, axis=-1))
            vec_done.append(parts[-1])
            if lb == n_lane_blocks - 1:
                dc = jnp.concatenate(parts, axis=-1) + cb_ref[...]
                parts.clear()
                y = jax.nn.silu(_layer_norm(dc, lg_ref[...], lb_ref[...]))
                bin_ref[pl.ds(r0 + rc * CONV_ROWS, CONV_ROWS), :] = y.astype(BF16)
                vec_done.append(y)

        n_conv = 0
        for q, (rc, lb) in enumerate(pieces):
            while n_conv < min(q + CONV_LEAD_PIECES + 1, len(pieces)):
                conv_piece(n_conv)
                n_conv += 1
            rows = slice(rc * rows_per_block, (rc + 1) * rows_per_block)
            lanes = slice(lb * PROJ_LANES, (lb + 1) * PROJ_LANES)
            zp = jnp.dot(h_ref[rows, :], w_ref[:, lanes], preferred_element_type=F32)
            if len(vec_done) >= 2:
                zp = _not_before(zp, vec_done[-2])
            z_ref[rows, lanes] = zp
            proj_done.append(zp)


def _inproj_conv(x, seq, g, w16, conv_w, conv_b, ln_g, ln_b, *, tm):
    m, k = x.shape
    tiles_per_seq = seq // tm
    n_conv_steps = N_COLS - N_GLU_STEPS
    assert tm // CONV_STEP_ROWS <= n_conv_steps, "not enough grid steps to convolve the whole tile"
    const = lambda i, j: (0, 0)
    return pl.pallas_call(
        functools.partial(_inproj_conv_kernel, tiles_per_seq=tiles_per_seq),
        out_shape=[
            jax.ShapeDtypeStruct((n_conv_steps, m, COL_W), F32),
            jax.ShapeDtypeStruct((m, D_B), BF16),
            jax.ShapeDtypeStruct((m // seq, HIST, D_B), F32),
        ],
        grid=(m // tm, N_COLS),
        in_specs=[
            pl.BlockSpec((tm, k), lambda i, j: (i, 0)),
            pl.BlockSpec((1, k), const),
            pl.BlockSpec((None, k, COL_W), lambda i, j: (_glu_first(j), 0, 0)),
            pl.BlockSpec((CONV_WIDTH, D_B), const),
            pl.BlockSpec((1, D_B), const),
            pl.BlockSpec((1, D_B), const),
            pl.BlockSpec((1, D_B), const),
        ],
        out_specs=[
            pl.BlockSpec((None, tm, COL_W), lambda i, j: (jnp.maximum(j - N_GLU_STEPS, 0), i, 0)),
            pl.BlockSpec((tm, D_B), lambda i, j: (i, 0)),
            pl.BlockSpec((1, HIST, D_B), lambda i, j: (i // tiles_per_seq, 0, 0)),
        ],
        scratch_shapes=[pltpu.VMEM((tm, k), BF16), pltpu.VMEM((tm + HIST_PAD, D_B), F32)],
        compiler_params=_cparams("arbitrary", "arbitrary", ),
        name="inproj_conv",
    )(x, g.reshape(1, k), w16, conv_w, conv_b.reshape(1, D_B), ln_g.reshape(1, D_B), ln_b.reshape(1, D_B))


def _conv_sample_kernel(*refs, seq, n_round):
    ga_ref, gb_ref, hist_ref, w_ref, cb_ref, g_ref, b_ref = refs[:7]
    f32_refs = refs[7 : 7 + n_round]
    o_ref, nh_ref = refs[7 + n_round : 9 + n_round]
    bf16_refs = refs[9 + n_round : 9 + 2 * n_round]
    cs_ref, dc_ref, os_ref = refs[9 + 2 * n_round :]
    for src, dst in zip(f32_refs, bf16_refs):
        dst[...] = src[...].astype(BF16)
    n_seq = hist_ref.shape[1]
    n_slabs = D_B // LANES
    c = ga_ref[...] * jax.nn.sigmoid(gb_ref[...])
    for l in range(n_slabs):
        cs_ref[l] = c[:, l * LANES : (l + 1) * LANES]
    new = [jnp.concatenate([cs_ref[l, pl.ds(t, n_seq, stride=seq), :] for l in range(n_slabs)], axis=-1)
           for t in range(seq)]
    nh_ref[0 : HIST - seq] = hist_ref[seq:HIST]
    for t in range(seq):
        nh_ref[HIST - seq + t] = new[t]
    for lb in range(D_B // CONV_LANES):
        lanes = slice(lb * CONV_LANES, (lb + 1) * CONV_LANES)
        ext = [hist_ref[j, :, lanes] for j in range(HIST)] + [new[t][:, lanes] for t in range(seq)]
        for t in range(seq):
            acc = w_ref[0:1, lanes] * ext[t]
            for k in range(1, CONV_WIDTH):
                acc = acc + w_ref[k : k + 1, lanes] * ext[t + k]
            dc_ref[t, :, lanes] = acc
    for t in range(seq):
        y = jax.nn.silu(_layer_norm(dc_ref[t] + cb_ref[...], g_ref[...], b_ref[...]))
        for l in range(n_slabs):
            os_ref[l, pl.ds(t, n_seq, stride=seq), :] = y[:, l * LANES : (l + 1) * LANES]
    for l in range(n_slabs):
        o_ref[:, l * LANES : (l + 1) * LANES] = os_ref[l].astype(BF16)


def _conv_sample(z, hist, seq, conv_w, conv_b, ln_g, ln_b, *, n_seq, also_round=()):
    batch = hist.shape[1]
    steps = batch // n_seq
    slab_specs = [pl.BlockSpec((w.shape[0] // steps, w.shape[1]), lambda i: (i, 0)) for w in also_round]
    slab_scratch = pltpu.VMEM((D_B // LANES, n_seq * seq, LANES), F32)
    return pl.pallas_call(
        functools.partial(_conv_sample_kernel, seq=seq, n_round=len(also_round)),
        out_shape=[
            jax.ShapeDtypeStruct((batch * seq, D_B), BF16),
            jax.ShapeDtypeStruct((HIST, batch, D_B), F32),
        ] + [jax.ShapeDtypeStruct(w.shape, BF16) for w in also_round],
        grid=(steps,),
        in_specs=[
            pl.BlockSpec((n_seq * seq, COL_W), lambda i: (i, COL_GLU_A)),
            pl.BlockSpec((n_seq * seq, COL_W), lambda i: (i, COL_GLU_B)),
            pl.BlockSpec((HIST, n_seq, D_B), lambda i: (0, i, 0)),
            pl.BlockSpec((CONV_WIDTH, D_B), lambda i: (0, 0)),
            pl.BlockSpec((1, D_B), lambda i: (0, 0)),
            pl.BlockSpec((1, D_B), lambda i: (0, 0)),
            pl.BlockSpec((1, D_B), lambda i: (0, 0)),
        ] + slab_specs,
        out_specs=[
            pl.BlockSpec((n_seq * seq, D_B), lambda i: (i, 0)),
            pl.BlockSpec((HIST, n_seq, D_B), lambda i: (0, i, 0)),
        ] + slab_specs,
        scratch_shapes=[slab_scratch, pltpu.VMEM((seq, n_seq, D_B), F32), slab_scratch],
        compiler_params=_cparams("arbitrary"),
        name="conv_sample",
    )(z, z, hist, conv_w, conv_b.reshape(1, D_B), ln_g.reshape(1, D_B), ln_b.reshape(1, D_B), *also_round)


def _head_cols(h):
    return slice(h * MEM_HEAD_DIM, (h + 1) * MEM_HEAD_DIM)


def _split_lane_tiles(kv):
    b = kv.shape[0]
    kv = kv.reshape(b, N_MEM, N_MEM_HEADS, HEAD_LANE_TILES, LANES)
    return kv.transpose(0, 1, 3, 2, 4).reshape(b, N_MEM * HEAD_ROW_PITCH, LANES)


def _merge_lane_tiles(view, batch):
    view = view.reshape(batch, N_MEM, HEAD_LANE_TILES, N_MEM_HEADS, LANES)
    return view.transpose(0, 1, 3, 2, 4).reshape(batch, N_MEM, N_MEM_HEADS, MEM_HEAD_DIM)


def _head_of(kv_ref, s, h):
    if kv_ref.shape[-1] == LANES:
        tiles = [kv_ref[s, pl.ds(j * N_MEM_HEADS + h, N_MEM, stride=HEAD_ROW_PITCH), :]
                 for j in range(HEAD_LANE_TILES)]
        return jnp.concatenate(tiles, axis=-1).astype(BF16)
    return kv_ref[s, :, _head_cols(h)].astype(BF16)


def _attn_kernel(q_ref, k_ref, v_ref, o_ref, *, rows):
    scale = 1.0 / math.sqrt(MEM_HEAD_DIM)
    pairs = [(s, h) for s in range(k_ref.shape[0]) for h in range(N_MEM_HEADS)]
    scores = []
    for s, h in pairs:
        q = (q_ref[s * rows : (s + 1) * rows, _head_cols(h)] * scale).astype(BF16)
        scores.append(lax.dot_general(q, _head_of(k_ref, s, h), (((1,), (1,)), ((), ())),
                                      preferred_element_type=F32))
    sc = jnp.concatenate(scores, axis=0)
    p = jnp.exp(sc - jnp.max(sc, axis=-1, keepdims=True))
    p = p / jnp.sum(p, axis=-1, keepdims=True)
    for n, (s, h) in enumerate(pairs):
        ph = p[n * rows : (n + 1) * rows, :].astype(BF16)
        o = jnp.dot(ph, _head_of(v_ref, s, h), preferred_element_type=F32)
        o_ref[s * rows : (s + 1) * rows, _head_cols(h)] = o.astype(BF16)


def _attn(z, cols, k, v, kv_specs, *, rows, n_seq):
    m = z.shape[-2]
    return pl.pallas_call(
        functools.partial(_attn_kernel, rows=rows),
        out_shape=jax.ShapeDtypeStruct((m, D_C), BF16),
        grid=(m // (rows * n_seq),),
        in_specs=[_col_block_spec(z, rows * n_seq, cols.q)] + kv_specs,
        out_specs=pl.BlockSpec((rows * n_seq, D_C), lambda i: (i, 0)),
        compiler_params=_cparams("arbitrary"),
        name="attn",
    )(z, k, v)


def _merge_kernel(a_ref, b_ref, c_ref, *refs):
    gate_refs = refs[: 2 * N_BRANCH]
    bg_ref, x_ref, wa_ref, wb_ref, wc_ref, wo_ref, mg_ref, x1_ref, h2_ref = refs[2 * N_BRANCH :]
    halves = []
    for half in range(2):
        cols = slice(half * COL_W, (half + 1) * COL_W)
        merged = None
        for br, (in_ref, w_ref) in enumerate(((a_ref, wa_ref), (b_ref, wb_ref), (c_ref, wc_ref))):
            y = jnp.dot(in_ref[...], w_ref[:, cols], preferred_element_type=F32)
            gcols = slice((2 * br + half) * COL_W, (2 * br + half + 1) * COL_W)
            term = jax.nn.sigmoid(gate_refs[2 * br + half][...] + bg_ref[:, gcols]) * y
            merged = term if merged is None else merged + term
        halves.append(merged.astype(BF16))
    merged16 = jnp.concatenate(halves, axis=-1)
    x1 = x_ref[...] + jnp.dot(merged16, wo_ref[...], preferred_element_type=F32)
    x1_ref[...] = x1
    h2_ref[...] = _rms_norm(x1, mg_ref[...]).astype(BF16)


def _merge(a_in, b_in, c_in, z, cols, x, b_gate, w_a, w_b, w_c, w_o, mlp_g, *, tm):
    m = x.shape[0]
    const = lambda i: (0, 0)
    resident = functools.partial(pl.BlockSpec, index_map=const, pipeline_mode=pl.Buffered(1))
    gate_specs = [_col_block_spec(z, tm, cols.gates + n) for n in range(2 * N_BRANCH)]
    return pl.pallas_call(
        _merge_kernel,
        out_shape=[jax.ShapeDtypeStruct((m, D_MODEL), F32), jax.ShapeDtypeStruct((m, D_MODEL), BF16)],
        grid=(m // tm,),
        in_specs=[pl.BlockSpec((tm, D_A), lambda i: (i, 0))] * 3
        + gate_specs
        + [
            pl.BlockSpec((1, N_BRANCH * D_MODEL), const),
            pl.BlockSpec((tm, D_MODEL), lambda i: (i, 0)),
            resident((D_A, D_MODEL)),
            resident((D_B, D_MODEL)),
            resident((D_C, D_MODEL)),
            resident((D_MODEL, D_MODEL)),
            pl.BlockSpec((1, D_MODEL), const),
        ],
        out_specs=[pl.BlockSpec((tm, D_MODEL), lambda i: (i, 0))] * 2,
        compiler_params=_cparams("arbitrary"),
        name="merge",
    )(a_in, b_in, c_in, *([z] * (2 * N_BRANCH)), b_gate.reshape(1, -1), x, w_a, w_b, w_c, w_o,
      mlp_g.reshape(1, D_MODEL))


def _mlp_kernel(h2_ref, x1_ref, wu_ref, wd_ref, fg_ref, y_ref, acc_ref):
    f = pl.program_id(1)

    def ffn():
        t = jnp.square(jnp.maximum(jnp.dot(h2_ref[...], wu_ref[...], preferred_element_type=F32), 0.0))
        return jnp.dot(t.astype(BF16), wd_ref[...], preferred_element_type=F32)

    @pl.when(f == 0)
    def _():
        acc_ref[...] = x1_ref[...] + ffn()

    @pl.when(f > 0)
    def _():
        acc_ref[...] += ffn()

    @pl.when(f == pl.num_programs(1) - 1)
    def _():
        y_ref[...] = _rms_norm(acc_ref[...], fg_ref[...])


def _mlp(h2, x1, w_up, w_down, final_g, *, tm, tf):
    m = h2.shape[0]
    return pl.pallas_call(
        _mlp_kernel,
        out_shape=jax.ShapeDtypeStruct((m, D_MODEL), F32),
        grid=(m // tm, D_FF // tf),
        in_specs=[
            pl.BlockSpec((tm, D_MODEL), lambda i, f: (i, 0)),
            pl.BlockSpec((tm, D_MODEL), lambda i, f: (i, 0)),
            pl.BlockSpec((D_MODEL, tf), lambda i, f: (0, f)),
            pl.BlockSpec((tf, D_MODEL), lambda i, f: (f, 0)),
            pl.BlockSpec((1, D_MODEL), lambda i, f: (0, 0)),
        ],
        out_specs=pl.BlockSpec((tm, D_MODEL), lambda i, f: (i, 0)),
        scratch_shapes=[pltpu.VMEM((tm, D_MODEL), F32)],
        compiler_params=_cparams("arbitrary", "arbitrary"),
        name="mlp",
    )(h2, x1, w_up, w_down, final_g.reshape(1, D_MODEL))


def _mlp_first_kernel(h2_ref, wu_ref, wd_ref, acc_ref, wu16_ref, wd16_ref):
    @pl.when(pl.program_id(0) == 0)
    def _():
        acc_ref[...] = jnp.zeros(acc_ref.shape, F32)

    h2 = h2_ref[...]
    t_pieces = []
    for c in range(0, wu_ref.shape[1], PROJ_LANES):
        wu = wu_ref[:, c : c + PROJ_LANES].astype(BF16)
        wu16_ref[:, c : c + PROJ_LANES] = wu
        t = jnp.square(jnp.maximum(jnp.dot(h2, wu, preferred_element_type=F32), 0.0))
        t_pieces.append(t.astype(BF16))
    t16 = jnp.concatenate(t_pieces, axis=-1)
    for c in range(0, wd_ref.shape[1], PROJ_LANES):
        wd = wd_ref[:, c : c + PROJ_LANES].astype(BF16)
        wd16_ref[:, c : c + PROJ_LANES] = wd
        acc_ref[:, c : c + PROJ_LANES] += jnp.dot(t16, wd, preferred_element_type=F32)


def _mlp_first(h2, w_up, w_down, *, tf):
    m = h2.shape[0]
    resident = pl.BlockSpec((m, D_MODEL), lambda f: (0, 0))
    return pl.pallas_call(
        _mlp_first_kernel,
        out_shape=[
            jax.ShapeDtypeStruct((m, D_MODEL), F32),
            jax.ShapeDtypeStruct((D_MODEL, D_FF), BF16),
            jax.ShapeDtypeStruct((D_FF, D_MODEL), BF16),
        ],
        grid=(D_FF // tf,),
        in_specs=[
            resident,
            pl.BlockSpec((D_MODEL, tf), lambda f: (0, f)),
            pl.BlockSpec((tf, D_MODEL), lambda f: (f, 0)),
        ],
        out_specs=[
            resident,
            pl.BlockSpec((D_MODEL, tf), lambda f: (0, f)),
            pl.BlockSpec((tf, D_MODEL), lambda f: (f, 0)),
        ],
        compiler_params=_cparams("arbitrary"),
        name="mlp_first",
    )(h2, w_up, w_down)


def _residual_norm_kernel(x_ref, d_ref, g_ref, y_ref):
    y_ref[...] = _rms_norm(x_ref[...] + d_ref[...], g_ref[...])


def _residual_norm(x, d, g, *, tm):
    m = x.shape[0]
    row_spec = pl.BlockSpec((tm, D_MODEL), lambda i: (i, 0))
    return pl.pallas_call(
        _residual_norm_kernel,
        out_shape=jax.ShapeDtypeStruct((m, D_MODEL), F32),
        grid=(m // tm,),
        in_specs=[row_spec, row_spec, pl.BlockSpec((1, D_MODEL), lambda i: (0, 0))],
        out_specs=row_spec,
        compiler_params=_cparams("arbitrary"),
        name="residual_norm",
    )(x, d, g.reshape(1, D_MODEL))


PROMPT_ATTN_ROWS = 512
SAMPLE_ATTN_SEQS = 8


def _layer(x, seq, k, v, hist, p, *, want_vn):
    chunk = min(seq, CHUNK)
    conv_args = (p["conv_w"], p["conv_b"], p["conv_ln_g"], p["conv_ln_b"])
    if hist is None:
        assert p["w_in"].dtype == BF16, "the group with a single token tile goes first and rounds w_in"
        z, b_in, new_hist = _inproj_conv(x, seq, p["attn_norm_g"], p["w_in"], *conv_args, tm=1024)
        cols = Z_COLS_NO_GLU
    else:
        z, w_in16 = _norm_matmul(x, p["attn_norm_g"], p["w_in"], tm=1024, tn=512, emit_w16=True)
        p = {**p, "w_in": w_in16}
        cols = Z_COLS_ALL
        out_names = ("w_a_out", "w_b_out", "w_c_out", "w_o")
        b_in, new_hist, *rounded = _conv_sample(z, hist, seq, *conv_args, n_seq=16,
                                               also_round=[p[name] for name in out_names])
        p = {**p, **dict(zip(out_names, rounded))}
    sgu_out = _sgu(z, cols, p["sgu_ln_g"], p["sgu_ln_b"], p["sgu_w"], p["sgu_b"], chunk=chunk, rows=512,
                   want_vn=want_vn)
    if hist is None:
        steps = seq // PROMPT_ATTN_ROWS
        kv_specs = [pl.BlockSpec((1, N_MEM, D_C), functools.partial(lambda i, col: (i // steps, 0, col), col=col))
                    for col in (0, 1)]
        c_in = _attn(z, cols, k, v, kv_specs, rows=PROMPT_ATTN_ROWS, n_seq=1)
    else:
        kv_specs = [pl.BlockSpec((SAMPLE_ATTN_SEQS, N_MEM * HEAD_ROW_PITCH, LANES), lambda i: (i, 0, 0))] * 2
        c_in = _attn(z, cols, k, v, kv_specs, rows=seq, n_seq=SAMPLE_ATTN_SEQS)
    x1, h2 = _merge(sgu_out[0], b_in, c_in, z, cols, x, p["b_gate"], p["w_a_out"], p["w_b_out"], p["w_c_out"],
                    p["w_o"], p["mlp_norm_g"], tm=256)
    if p["w_up"].dtype == F32:
        d, w_up16, w_down16 = _mlp_first(h2, p["w_up"], p["w_down"], tf=512)
        p = {**p, "w_up": w_up16, "w_down": w_down16}
        y = _residual_norm(x1, d, p["final_norm_g"], tm=256)
    else:
        y = _mlp(h2, x1, p["w_up"], p["w_down"], p["final_norm_g"], tm=512, tf=1024)
    return y, new_hist, (sgu_out[1] if want_vn else None), p


def kernel(x_prompt, x_sample, mem_prompt, cache_mem_k, cache_mem_v, state_conv, attn_norm_g, w_in, b_gate,
           sgu_ln_g, sgu_ln_b, sgu_w, sgu_b, w_a_out, conv_w, conv_b, conv_ln_g, conv_ln_b, w_b_out, mem_norm_g,
           w_mem_kv, w_c_out, w_o, mlp_norm_g, w_up, w_down, final_norm_g):
    assert attn_norm_g.shape[0] == 1, "single-layer trunk"
    bp, sp, _ = x_prompt.shape
    bs, ss, _ = x_sample.shape
    p = dict(
        attn_norm_g=attn_norm_g[0], w_in=w_in[0], b_gate=b_gate[0], sgu_ln_g=sgu_ln_g[0],
        sgu_ln_b=sgu_ln_b[0], sgu_w=sgu_w[0], sgu_b=sgu_b[0], w_a_out=w_a_out[0],
        conv_w=conv_w[0], conv_b=conv_b[0], conv_ln_g=conv_ln_g[0], conv_ln_b=conv_ln_b[0],
        w_b_out=w_b_out[0], w_c_out=w_c_out[0], w_o=w_o[0],
        mlp_norm_g=mlp_norm_g[0], w_up=w_up[0], w_down=w_down[0],
        final_norm_g=final_norm_g,
    )
    kv, kv_split = _norm_matmul(mem_prompt.reshape(bp * N_MEM, D_MODEL), mem_norm_g[0], w_mem_kv[0],
                                tm=1024, tn=512, emit_head_split=True)
    kv = kv.reshape(bp, N_MEM, 2 * D_C)
    y_s, hist_s, vn_s, p = _layer(x_sample.reshape(bs * ss, D_MODEL), ss, _split_lane_tiles(cache_mem_k[0]),
                                  _split_lane_tiles(cache_mem_v[0]), jnp.transpose(state_conv[0], (1, 0, 2)), p,
                                  want_vn=True)
    y_p, hist_p, _, _ = _layer(x_prompt.reshape(bp * sp, D_MODEL), sp, kv, kv, None, p, want_vn=False)
    return (
        y_p.reshape(bp, sp, D_MODEL),
        y_s.reshape(bs, ss, D_MODEL),
        _merge_lane_tiles(kv_split[0], bp)[None],
        _merge_lane_tiles(kv_split[1], bp)[None],
        hist_p[None],
        jnp.transpose(hist_s, (1, 0, 2))[None],
        vn_s.reshape(1, bs, ss, D_A),
    )
```

```python
import functools
import math
from typing import NamedTuple

import jax
import jax.numpy as jnp
from jax import lax
from jax.experimental import pallas as pl
from jax.experimental.pallas import tpu as pltpu

F32 = jnp.float32
BF16 = jnp.bfloat16

D_MODEL = 2048
D_A = 1024
D_B = 1024
D_C = 1024
CHUNK = 128
SGU_GROUPS = 8
SGU_HEAD = D_A // SGU_GROUPS
CONV_WIDTH = 31
HIST = CONV_WIDTH - 1
N_MEM = 256
N_MEM_HEADS = 4
MEM_HEAD_DIM = D_C // N_MEM_HEADS
N_BRANCH = 3
D_FF = 4 * D_MODEL
N_IN = 2 * D_A + 2 * D_B + D_C + N_BRANCH * D_MODEL
EPS = 1e-6

COL_U, COL_V, COL_GLU_A, COL_GLU_B, COL_Q, COL_GATES = 0, 1, 2, 3, 4, 5
COL_W = 1024


class ZCols(NamedTuple):
    u: int
    v: int
    q: int
    gates: int


def _col_block_spec(z, rows, col):
    if z.ndim == 3:
        return pl.BlockSpec((None, rows, COL_W), lambda i: (col, i, 0))
    return pl.BlockSpec((rows, COL_W), lambda i: (i, col))


Z_COLS_ALL = ZCols(COL_U, COL_V, COL_Q, COL_GATES)
Z_COLS_NO_GLU = ZCols(COL_U, COL_V, COL_Q - 2, COL_GATES - 2)

V7X_VMEM_BYTES = 64 * 1024 * 1024
VMEM_LIMIT_BYTES = V7X_VMEM_BYTES - 8 * 1024 * 1024
SUBLANES = 8
LANES = 128
HEAD_LANE_TILES = MEM_HEAD_DIM // LANES
HEAD_ROW_PITCH = N_MEM_HEADS * HEAD_LANE_TILES
V7X_MXU_COLS = 256
PROJ_LANES = V7X_MXU_COLS
PROJ_ROWS = 256
HIST_PAD = 32


def _cparams(*semantics, flags=None):
    return pltpu.CompilerParams(dimension_semantics=semantics, vmem_limit_bytes=VMEM_LIMIT_BYTES, flags=flags)


def _not_before(value, anchor):
    tile = anchor[-SUBLANES:, -LANES:]
    zero = pltpu.bitcast((pltpu.bitcast(tile, jnp.uint32) >> 16) >> 16, F32)
    reps = (value.shape[0] // SUBLANES, value.shape[1] // LANES)
    return value + jnp.tile(zero, reps)


def _rms_norm(x, g):
    ms = jnp.mean(x * x, axis=-1, keepdims=True)
    return x * lax.rsqrt(ms + EPS) * g


def _layer_norm(x, g, b):
    mu = jnp.mean(x, axis=-1, keepdims=True)
    xc = x - mu
    var = jnp.mean(xc * xc, axis=-1, keepdims=True)
    return xc * lax.rsqrt(var + EPS) * g + b


NORM_ROWS = 128


def _norm_matmul_kernel(x_ref, g_ref, w_ref, o_ref, *rest, emit_w16, emit_head_split):
    *extra, h_ref = rest
    w16_ref = extra.pop(0) if emit_w16 else None
    split_ref = extra.pop(0) if emit_head_split else None

    @pl.when(pl.program_id(1) == 0)
    def _():
        def body(r, carry):
            rows = pl.ds(pl.multiple_of(r * NORM_ROWS, NORM_ROWS), NORM_ROWS)
            h_ref[rows, :] = _rms_norm(x_ref[rows, :], g_ref[...]).astype(BF16)
            return carry

        lax.fori_loop(0, x_ref.shape[0] // NORM_ROWS, body, 0)

    tn = w_ref.shape[1]
    for c in range(0, tn, PROJ_LANES):
        w16 = w_ref[:, c : c + PROJ_LANES].astype(BF16)
        if emit_w16:
            w16_ref[:, c : c + PROJ_LANES] = w16
        piece = jnp.dot(h_ref[...], w16, preferred_element_type=F32)
        o_ref[:, c : c + PROJ_LANES] = piece
        if emit_head_split:
            head = ((pl.program_id(1) * tn + c) % D_C) // MEM_HEAD_DIM
            for lt in range(HEAD_LANE_TILES):
                split_ref[pl.ds(lt * N_MEM_HEADS + head, piece.shape[0], stride=HEAD_ROW_PITCH), :] = (
                    piece[:, lt * LANES : (lt + 1) * LANES])


def _norm_matmul(x, g, w, *, tm, tn, emit_w16=False, emit_head_split=False):
    m, k = x.shape
    n = w.shape[1]
    out_shape = [jax.ShapeDtypeStruct((m, n), F32)]
    out_specs = [pl.BlockSpec((tm, tn), lambda i, j: (i, j))]
    if emit_w16:
        assert m == tm, "every weight block must be visited exactly once"
        per_col = COL_W // tn
        out_shape.append(jax.ShapeDtypeStruct((n // COL_W, k, COL_W), BF16))
        out_specs.append(pl.BlockSpec((None, k, tn), lambda i, j: (j // per_col, 0, j % per_col)))
    if emit_head_split:
        assert m == tm and PROJ_LANES == MEM_HEAD_DIM and n == 2 * D_C
        out_shape.append(jax.ShapeDtypeStruct((n // D_C, m * HEAD_ROW_PITCH, LANES), F32))
        out_specs.append(pl.BlockSpec((None, m * HEAD_ROW_PITCH, LANES), lambda i, j: (j // (D_C // tn), 0, 0)))
    outs = pl.pallas_call(
        functools.partial(_norm_matmul_kernel, emit_w16=emit_w16, emit_head_split=emit_head_split),
        out_shape=out_shape,
        grid=(m // tm, n // tn),
        in_specs=[
            pl.BlockSpec((tm, k), lambda i, j: (i, 0), pipeline_mode=pl.Buffered(1 if m == tm else 2)),
            pl.BlockSpec((1, k), lambda i, j: (0, 0)),
            pl.BlockSpec((k, tn), lambda i, j: (0, j)),
        ],
        out_specs=out_specs,
        scratch_shapes=[pltpu.VMEM((tm, k), BF16)],
        compiler_params=_cparams("arbitrary", "arbitrary"),
        name="norm_matmul",
    )(x, g.reshape(1, k), w)
    return outs if len(outs) > 1 else outs[0]


def _lane_periodic(x, period):
    lane = lax.broadcasted_iota(jnp.int32, x.shape, 1)
    y = jnp.where(lane < period, x, 0.0)
    while period < LANES:
        y = y + pltpu.roll(y, period, axis=1)
        period *= 2
    return y


def _sgu_kernel(u_ref, v_ref, g_ref, b_ref, w_ref, sb_ref, a_ref, *vn_refs, chunk):
    r = lax.broadcasted_iota(jnp.int32, (CHUNK, CHUNK), 0)
    c = lax.broadcasted_iota(jnp.int32, (CHUNK, CHUNK), 1)
    seg_bits = chunk.bit_length() - 1
    same_segment = (r >> seg_bits) == (c >> seg_bits)
    mask = jnp.logical_and(same_segment, (r & (chunk - 1)) >= (c & (chunk - 1)))
    sb = sb_ref[...] if chunk == CHUNK else _lane_periodic(sb_ref[...], chunk)
    w_s, bias = [], []
    for grp in range(SGU_GROUPS):
        if chunk == CHUNK:
            w = w_ref[grp]
        else:
            w = _lane_periodic(jnp.tile(w_ref[grp, 0:chunk, :], (CHUNK // chunk, 1)), chunk)
        w_s.append(jnp.where(mask, w, 0.0).astype(BF16))
        col = jnp.sum(jnp.where(r == c, sb[grp : grp + 1, :], 0.0), axis=1, keepdims=True)
        bias.append(jnp.broadcast_to(col, (CHUNK, SGU_HEAD)))
    for blk in range(u_ref.shape[0] // CHUNK):
        rows = slice(blk * CHUNK, (blk + 1) * CHUNK)
        vn = _layer_norm(jax.nn.gelu(v_ref[rows, :]), g_ref[...], b_ref[...])
        if vn_refs:
            vn_refs[0][rows, :] = vn
        vn16 = vn.astype(BF16)
        for grp in range(SGU_GROUPS):
            cols = slice(grp * SGU_HEAD, (grp + 1) * SGU_HEAD)
            mixed = jnp.dot(w_s[grp], vn16[:, cols], preferred_element_type=F32) + bias[grp]
            a_ref[rows, cols] = (jax.nn.gelu(u_ref[rows, cols]) * mixed).astype(BF16)


def _sgu(z, cols, ln_g, ln_b, sgu_w, sgu_b, *, chunk, rows, want_vn):
    m = z.shape[-2]
    out_shape = [jax.ShapeDtypeStruct((m, D_A), BF16)]
    out_specs = [pl.BlockSpec((rows, D_A), lambda i: (i, 0))]
    if want_vn:
        out_shape.append(jax.ShapeDtypeStruct((m, D_A), F32))
        out_specs.append(pl.BlockSpec((rows, D_A), lambda i: (i, 0)))
    return pl.pallas_call(
        functools.partial(_sgu_kernel, chunk=chunk),
        out_shape=out_shape,
        grid=(m // rows,),
        in_specs=[
            _col_block_spec(z, rows, cols.u),
            _col_block_spec(z, rows, cols.v),
            pl.BlockSpec((1, D_A), lambda i: (0, 0)),
            pl.BlockSpec((1, D_A), lambda i: (0, 0)),
            pl.BlockSpec((SGU_GROUPS, CHUNK, CHUNK), lambda i: (0, 0, 0)),
            pl.BlockSpec((SGU_GROUPS, CHUNK), lambda i: (0, 0)),
        ],
        out_specs=out_specs,
        compiler_params=_cparams("arbitrary"),
        name="sgu",
    )(z, z, ln_g.reshape(1, D_A), ln_b.reshape(1, D_A), sgu_w, sgu_b)


CONV_ROWS = 32
CONV_LANES = 128
CONV_LEAD = HIST_PAD - HIST


def _conv_window(win, w_ref, lanes, n_rows):
    out = None
    for b in range(SUBLANES):
        rows_b = n_rows if b == 0 else n_rows + SUBLANES
        y = None
        for a in range(HIST_PAD // SUBLANES + 1):
            k = SUBLANES * a + b - CONV_LEAD
            if 0 <= k < CONV_WIDTH:
                term = w_ref[k : k + 1, lanes] * win[SUBLANES * a : SUBLANES * a + rows_b, :]
                y = term if y is None else y + term
        shifted = y[b : b + n_rows, :]
        out = shifted if out is None else out + shifted
    return out


CONV_STEP_ROWS = 128
CONV_LEAD_PIECES = 2
N_GLU_STEPS = 2
N_COLS = N_IN // COL_W


def _glu_first(j):
    return jnp.where(j < N_GLU_STEPS, j + COL_GLU_A, jnp.where(j < COL_GLU_A + N_GLU_STEPS, j - N_GLU_STEPS, j))


def _inproj_conv_kernel(x_ref, g_ref, w_ref, cw_ref, cb_ref, lg_ref, lb_ref, z_ref, bin_ref, hist_ref,
                        h_ref, ext_ref, *, tiles_per_seq):
    i = pl.program_id(0)
    j = pl.program_id(1)
    t = x_ref.shape[0]

    def proj():
        return jnp.dot(h_ref[...], w_ref[...], preferred_element_type=F32)

    @pl.when(j == 0)
    def _():
        @pl.when(i % tiles_per_seq == 0)
        def _():
            ext_ref[0:HIST_PAD, :] = jnp.zeros((HIST_PAD, D_B), F32)

        @pl.when(i % tiles_per_seq != 0)
        def _():
            ext_ref[0:HIST_PAD, :] = ext_ref[t : t + HIST_PAD, :]

        for rb in range(t // PROJ_ROWS):
            for r in range(rb * PROJ_ROWS, (rb + 1) * PROJ_ROWS, NORM_ROWS):
                h_ref[r : r + NORM_ROWS, :] = _rms_norm(x_ref[r : r + NORM_ROWS, :], g_ref[...]).astype(BF16)
            rows = slice(rb * PROJ_ROWS, (rb + 1) * PROJ_ROWS)
            ext_ref[HIST_PAD + rb * PROJ_ROWS : HIST_PAD + (rb + 1) * PROJ_ROWS, :] = jnp.dot(
                h_ref[rows, :], w_ref[...], preferred_element_type=F32)

    @pl.when(j == 1)
    def _():
        ext_ref[HIST_PAD : HIST_PAD + t, :] = ext_ref[HIST_PAD : HIST_PAD + t, :] * jax.nn.sigmoid(proj())
        hist_ref[0] = ext_ref[t + CONV_LEAD : t + HIST_PAD, :]

    @pl.when(j >= N_GLU_STEPS)
    def _():
        chunk = jnp.minimum(j - N_GLU_STEPS, t // CONV_STEP_ROWS - 1)
        r0 = pl.multiple_of(chunk * CONV_STEP_ROWS, CONV_STEP_ROWS)
        n_row_blocks = CONV_STEP_ROWS // CONV_ROWS
        n_lane_blocks = D_B // PROJ_LANES
        rows_per_block = t // n_row_blocks
        pieces = [(rc, lb) for rc in range(n_row_blocks) for lb in range(n_lane_blocks)]
        vec_done, proj_done = [], []
        parts = []

        def conv_piece(p):
            rc, lb = pieces[p]
            wins = []
            for cl in range(lb * PROJ_LANES, (lb + 1) * PROJ_LANES, CONV_LANES):
                clanes = slice(cl, cl + CONV_LANES)
                win = ext_ref[pl.ds(r0 + rc * CONV_ROWS, CONV_ROWS + HIST_PAD), clanes]
                if p > CONV_LEAD_PIECES:
                    win = _not_before(win, proj_done[p - CONV_LEAD_PIECES - 1])
                wins.append(_conv_window(win, cw_ref, clanes, CONV_ROWS))
            parts.append(jnp.concatenate(wins, axis=-1))
            vec_done.append(parts[-1])
            if lb == n_lane_blocks - 1:
                dc = jnp.concatenate(parts, axis=-1) + cb_ref[...]
                parts.clear()
                y = jax.nn.silu(_layer_norm(dc, lg_ref[...], lb_ref[...]))
                bin_ref[pl.ds(r0 + rc * CONV_ROWS, CONV_ROWS), :] = y.astype(BF16)
                vec_done.append(y)

        n_conv = 0
        for q, (rc, lb) in enumerate(pieces):
            while n_conv < min(q + CONV_LEAD_PIECES + 1, len(pieces)):
                conv_piece(n_conv)
                n_conv += 1
            rows = slice(rc * rows_per_block, (rc + 1) * rows_per_block)
            lanes = slice(lb * PROJ_LANES, (lb + 1) * PROJ_LANES)
            zp = jnp.dot(h_ref[rows, :], w_ref[:, lanes], preferred_element_type=F32)
            if len(vec_done) >= 2:
                zp = _not_before(zp, vec_done[-2])
            z_ref[rows, lanes] = zp
            proj_done.append(zp)


def _inproj_conv(x, seq, g, w16, conv_w, conv_b, ln_g, ln_b, *, tm):
    m, k = x.shape
    tiles_per_seq = seq // tm
    n_conv_steps = N_COLS - N_GLU_STEPS
    assert tm // CONV_STEP_ROWS <= n_conv_steps, "not enough grid steps to convolve the whole tile"
    const = lambda i, j: (0, 0)
    return pl.pallas_call(
        functools.partial(_inproj_conv_kernel, tiles_per_seq=tiles_per_seq),
        out_shape=[
            jax.ShapeDtypeStruct((n_conv_steps, m, COL_W), F32),
            jax.ShapeDtypeStruct((m, D_B), BF16),
            jax.ShapeDtypeStruct((m // seq, HIST, D_B), F32),
        ],
        grid=(m // tm, N_COLS),
        in_specs=[
            pl.BlockSpec((tm, k), lambda i, j: (i, 0)),
            pl.BlockSpec((1, k), const),
            pl.BlockSpec((None, k, COL_W), lambda i, j: (_glu_first(j), 0, 0)),
            pl.BlockSpec((CONV_WIDTH, D_B), const),
            pl.BlockSpec((1, D_B), const),
            pl.BlockSpec((1, D_B), const),
            pl.BlockSpec((1, D_B), const),
        ],
        out_specs=[
            pl.BlockSpec((None, tm, COL_W), lambda i, j: (jnp.maximum(j - N_GLU_STEPS, 0), i, 0)),
            pl.BlockSpec((tm, D_B), lambda i, j: (i, 0)),
            pl.BlockSpec((1, HIST, D_B), lambda i, j: (i // tiles_per_seq, 0, 0)),
        ],
        scratch_shapes=[pltpu.VMEM((tm, k), BF16), pltpu.VMEM((tm + HIST_PAD, D_B), F32)],
        compiler_params=_cparams("arbitrary", "arbitrary", ),
        name="inproj_conv",
    )(x, g.reshape(1, k), w16, conv_w, conv_b.reshape(1, D_B), ln_g.reshape(1, D_B), ln_b.reshape(1, D_B))


def _conv_sample_kernel(*refs, seq, n_round):
    ga_ref, gb_ref, hist_ref, w_ref, cb_ref, g_ref, b_ref = refs[:7]
    f32_refs = refs[7 : 7 + n_round]
    o_ref, nh_ref = refs[7 + n_round : 9 + n_round]
    bf16_refs = refs[9 + n_round : 9 + 2 * n_round]
    cs_ref, dc_ref, os_ref = refs[9 + 2 * n_round :]
    for src, dst in zip(f32_refs, bf16_refs):
        dst[...] = src[...].astype(BF16)
    n_seq = hist_ref.shape[1]
    n_slabs = D_B // LANES
    c = ga_ref[...] * jax.nn.sigmoid(gb_ref[...])
    for l in range(n_slabs):
        cs_ref[l] = c[:, l * LANES : (l + 1) * LANES]
    new = [jnp.concatenate([cs_ref[l, pl.ds(t, n_seq, stride=seq), :] for l in range(n_slabs)], axis=-1)
           for t in range(seq)]
    nh_ref[0 : HIST - seq] = hist_ref[seq:HIST]
    for t in range(seq):
        nh_ref[HIST - seq + t] = new[t]
    for lb in range(D_B // CONV_LANES):
        lanes = slice(lb * CONV_LANES, (lb + 1) * CONV_LANES)
        ext = [hist_ref[j, :, lanes] for j in range(HIST)] + [new[t][:, lanes] for t in range(seq)]
        for t in range(seq):
            acc = w_ref[0:1, lanes] * ext[t]
            for k in range(1, CONV_WIDTH):
                acc = acc + w_ref[k : k + 1, lanes] * ext[t + k]
            dc_ref[t, :, lanes] = acc
    for t in range(seq):
        y = jax.nn.silu(_layer_norm(dc_ref[t] + cb_ref[...], g_ref[...], b_ref[...]))
        for l in range(n_slabs):
            os_ref[l, pl.ds(t, n_seq, stride=seq), :] = y[:, l * LANES : (l + 1) * LANES]
    for l in range(n_slabs):
        o_ref[:, l * LANES : (l + 1) * LANES] = os_ref[l].astype(BF16)


def _conv_sample(z, hist, seq, conv_w, conv_b, ln_g, ln_b, *, n_seq, also_round=()):
    batch = hist.shape[1]
    steps = batch // n_seq
    slab_specs = [pl.BlockSpec((w.shape[0] // steps, w.shape[1]), lambda i: (i, 0)) for w in also_round]
    slab_scratch = pltpu.VMEM((D_B // LANES, n_seq * seq, LANES), F32)
    return pl.pallas_call(
        functools.partial(_conv_sample_kernel, seq=seq, n_round=len(also_round)),
        out_shape=[
            jax.ShapeDtypeStruct((batch * seq, D_B), BF16),
            jax.ShapeDtypeStruct((HIST, batch, D_B), F32),
        ] + [jax.ShapeDtypeStruct(w.shape, BF16) for w in also_round],
        grid=(steps,),
        in_specs=[
            pl.BlockSpec((n_seq * seq, COL_W), lambda i: (i, COL_GLU_A)),
            pl.BlockSpec((n_seq * seq, COL_W), lambda i: (i, COL_GLU_B)),
            pl.BlockSpec((HIST, n_seq, D_B), lambda i: (0, i, 0)),
            pl.BlockSpec((CONV_WIDTH, D_B), lambda i: (0, 0)),
            pl.BlockSpec((1, D_B), lambda i: (0, 0)),
            pl.BlockSpec((1, D_B), lambda i: (0, 0)),
            pl.BlockSpec((1, D_B), lambda i: (0, 0)),
        ] + slab_specs,
        out_specs=[
            pl.BlockSpec((n_seq * seq, D_B), lambda i: (i, 0)),
            pl.BlockSpec((HIST, n_seq, D_B), lambda i: (0, i, 0)),
        ] + slab_specs,
        scratch_shapes=[slab_scratch, pltpu.VMEM((seq, n_seq, D_B), F32), slab_scratch],
        compiler_params=_cparams("arbitrary"),
        name="conv_sample",
    )(z, z, hist, conv_w, conv_b.reshape(1, D_B), ln_g.reshape(1, D_B), ln_b.reshape(1, D_B), *also_round)


def _head_cols(h):
    return slice(h * MEM_HEAD_DIM, (h + 1) * MEM_HEAD_DIM)


def _split_lane_tiles(kv):
    b = kv.shape[0]
    kv = kv.reshape(b, N_MEM, N_MEM_HEADS, HEAD_LANE_TILES, LANES)
    return kv.transpose(0, 1, 3, 2, 4).reshape(b, N_MEM * HEAD_ROW_PITCH, LANES)


def _merge_lane_tiles(view, batch):
    view = view.reshape(batch, N_MEM, HEAD_LANE_TILES, N_MEM_HEADS, LANES)
    return view.transpose(0, 1, 3, 2, 4).reshape(batch, N_MEM, N_MEM_HEADS, MEM_HEAD_DIM)


def _head_of(kv_ref, s, h):
    if kv_ref.shape[-1] == LANES:
        tiles = [kv_ref[s, pl.ds(j * N_MEM_HEADS + h, N_MEM, stride=HEAD_ROW_PITCH), :]
                 for j in range(HEAD_LANE_TILES)]
        return jnp.concatenate(tiles, axis=-1).astype(BF16)
    return kv_ref[s, :, _head_cols(h)].astype(BF16)


def _attn_kernel(q_ref, k_ref, v_ref, o_ref, *, rows):
    scale = 1.0 / math.sqrt(MEM_HEAD_DIM)
    pairs = [(s, h) for s in range(k_ref.shape[0]) for h in range(N_MEM_HEADS)]
    scores = []
    for s, h in pairs:
        q = (q_ref[s * rows : (s + 1) * rows, _head_cols(h)] * scale).astype(BF16)
        scores.append(lax.dot_general(q, _head_of(k_ref, s, h), (((1,), (1,)), ((), ())),
                                      preferred_element_type=F32))
    sc = jnp.concatenate(scores, axis=0)
    p = jnp.exp(sc - jnp.max(sc, axis=-1, keepdims=True))
    p = p / jnp.sum(p, axis=-1, keepdims=True)
    for n, (s, h) in enumerate(pairs):
        ph = p[n * rows : (n + 1) * rows, :].astype(BF16)
        o = jnp.dot(ph, _head_of(v_ref, s, h), preferred_element_type=F32)
        o_ref[s * rows : (s + 1) * rows, _head_cols(h)] = o.astype(BF16)


def _attn(z, cols, k, v, kv_specs, *, rows, n_seq):
    m = z.shape[-2]
    return pl.pallas_call(
        functools.partial(_attn_kernel, rows=rows),
        out_shape=jax.ShapeDtypeStruct((m, D_C), BF16),
        grid=(m // (rows * n_seq),),
        in_specs=[_col_block_spec(z, rows * n_seq, cols.q)] + kv_specs,
        out_specs=pl.BlockSpec((rows * n_seq, D_C), lambda i: (i, 0)),
        compiler_params=_cparams("arbitrary"),
        name="attn",
    )(z, k, v)


def _merge_kernel(a_ref, b_ref, c_ref, *refs):
    gate_refs = refs[: 2 * N_BRANCH]
    bg_ref, x_ref, wa_ref, wb_ref, wc_ref, wo_ref, mg_ref, x1_ref, h2_ref = refs[2 * N_BRANCH :]
    halves = []
    for half in range(2):
        cols = slice(half * COL_W, (half + 1) * COL_W)
        merged = None
        for br, (in_ref, w_ref) in enumerate(((a_ref, wa_ref), (b_ref, wb_ref), (c_ref, wc_ref))):
            y = jnp.dot(in_ref[...], w_ref[:, cols], preferred_element_type=F32)
            gcols = slice((2 * br + half) * COL_W, (2 * br + half + 1) * COL_W)
            term = jax.nn.sigmoid(gate_refs[2 * br + half][...] + bg_ref[:, gcols]) * y
            merged = term if merged is None else merged + term
        halves.append(merged.astype(BF16))
    merged16 = jnp.concatenate(halves, axis=-1)
    x1 = x_ref[...] + jnp.dot(merged16, wo_ref[...], preferred_element_type=F32)
    x1_ref[...] = x1
    h2_ref[...] = _rms_norm(x1, mg_ref[...]).astype(BF16)


def _merge(a_in, b_in, c_in, z, cols, x, b_gate, w_a, w_b, w_c, w_o, mlp_g, *, tm):
    m = x.shape[0]
    const = lambda i: (0, 0)
    resident = functools.partial(pl.BlockSpec, index_map=const, pipeline_mode=pl.Buffered(1))
    gate_specs = [_col_block_spec(z, tm, cols.gates + n) for n in range(2 * N_BRANCH)]
    return pl.pallas_call(
        _merge_kernel,
        out_shape=[jax.ShapeDtypeStruct((m, D_MODEL), F32), jax.ShapeDtypeStruct((m, D_MODEL), BF16)],
        grid=(m // tm,),
        in_specs=[pl.BlockSpec((tm, D_A), lambda i: (i, 0))] * 3
        + gate_specs
        + [
            pl.BlockSpec((1, N_BRANCH * D_MODEL), const),
            pl.BlockSpec((tm, D_MODEL), lambda i: (i, 0)),
            resident((D_A, D_MODEL)),
            resident((D_B, D_MODEL)),
            resident((D_C, D_MODEL)),
            resident((D_MODEL, D_MODEL)),
            pl.BlockSpec((1, D_MODEL), const),
        ],
        out_specs=[pl.BlockSpec((tm, D_MODEL), lambda i: (i, 0))] * 2,
        compiler_params=_cparams("arbitrary"),
        name="merge",
    )(a_in, b_in, c_in, *([z] * (2 * N_BRANCH)), b_gate.reshape(1, -1), x, w_a, w_b, w_c, w_o,
      mlp_g.reshape(1, D_MODEL))


def _mlp_kernel(h2_ref, x1_ref, wu_ref, wd_ref, fg_ref, y_ref, acc_ref):
    f = pl.program_id(1)

    def ffn():
        t = jnp.square(jnp.maximum(jnp.dot(h2_ref[...], wu_ref[...], preferred_element_type=F32), 0.0))
        return jnp.dot(t.astype(BF16), wd_ref[...], preferred_element_type=F32)

    @pl.when(f == 0)
    def _():
        acc_ref[...] = x1_ref[...] + ffn()

    @pl.when(f > 0)
    def _():
        acc_ref[...] += ffn()

    @pl.when(f == pl.num_programs(1) - 1)
    def _():
        y_ref[...] = _rms_norm(acc_ref[...], fg_ref[...])


def _mlp(h2, x1, w_up, w_down, final_g, *, tm, tf):
    m = h2.shape[0]
    return pl.pallas_call(
        _mlp_kernel,
        out_shape=jax.ShapeDtypeStruct((m, D_MODEL), F32),
        grid=(m // tm, D_FF // tf),
        in_specs=[
            pl.BlockSpec((tm, D_MODEL), lambda i, f: (i, 0)),
            pl.BlockSpec((tm, D_MODEL), lambda i, f: (i, 0)),
            pl.BlockSpec((D_MODEL, tf), lambda i, f: (0, f)),
            pl.BlockSpec((tf, D_MODEL), lambda i, f: (f, 0)),
            pl.BlockSpec((1, D_MODEL), lambda i, f: (0, 0)),
        ],
        out_specs=pl.BlockSpec((tm, D_MODEL), lambda i, f: (i, 0)),
        scratch_shapes=[pltpu.VMEM((tm, D_MODEL), F32)],
        compiler_params=_cparams("arbitrary", "arbitrary"),
        name="mlp",
    )(h2, x1, w_up, w_down, final_g.reshape(1, D_MODEL))


def _mlp_first_kernel(h2_ref, wu_ref, wd_ref, acc_ref, wu16_ref, wd16_ref):
    @pl.when(pl.program_id(0) == 0)
    def _():
        acc_ref[...] = jnp.zeros(acc_ref.shape, F32)

    h2 = h2_ref[...]
    t_pieces = []
    for c in range(0, wu_ref.shape[1], PROJ_LANES):
        wu = wu_ref[:, c : c + PROJ_LANES].astype(BF16)
        wu16_ref[:, c : c + PROJ_LANES] = wu
        t = jnp.square(jnp.maximum(jnp.dot(h2, wu, preferred_element_type=F32), 0.0))
        t_pieces.append(t.astype(BF16))
    t16 = jnp.concatenate(t_pieces, axis=-1)
    for c in range(0, wd_ref.shape[1], PROJ_LANES):
        wd = wd_ref[:, c : c + PROJ_LANES].astype(BF16)
        wd16_ref[:, c : c + PROJ_LANES] = wd
        acc_ref[:, c : c + PROJ_LANES] += jnp.dot(t16, wd, preferred_element_type=F32)


def _mlp_first(h2, w_up, w_down, *, tf):
    m = h2.shape[0]
    resident = pl.BlockSpec((m, D_MODEL), lambda f: (0, 0))
    return pl.pallas_call(
        _mlp_first_kernel,
        out_shape=[
            jax.ShapeDtypeStruct((m, D_MODEL), F32),
            jax.ShapeDtypeStruct((D_MODEL, D_FF), BF16),
            jax.ShapeDtypeStruct((D_FF, D_MODEL), BF16),
        ],
        grid=(D_FF // tf,),
        in_specs=[
            resident,
            pl.BlockSpec((D_MODEL, tf), lambda f: (0, f)),
            pl.BlockSpec((tf, D_MODEL), lambda f: (f, 0)),
        ],
        out_specs=[
            resident,
            pl.BlockSpec((D_MODEL, tf), lambda f: (0, f)),
            pl.BlockSpec((tf, D_MODEL), lambda f: (f, 0)),
        ],
        compiler_params=_cparams("arbitrary"),
        name="mlp_first",
    )(h2, w_up, w_down)


def _residual_norm_kernel(x_ref, d_ref, g_ref, y_ref):
    y_ref[...] = _rms_norm(x_ref[...] + d_ref[...], g_ref[...])


def _residual_norm(x, d, g, *, tm):
    m = x.shape[0]
    row_spec = pl.BlockSpec((tm, D_MODEL), lambda i: (i, 0))
    return pl.pallas_call(
        _residual_norm_kernel,
        out_shape=jax.ShapeDtypeStruct((m, D_MODEL), F32),
        grid=(m // tm,),
        in_specs=[row_spec, row_spec, pl.BlockSpec((1, D_MODEL), lambda i: (0, 0))],
        out_specs=row_spec,
        compiler_params=_cparams("arbitrary"),
        name="residual_norm",
    )(x, d, g.reshape(1, D_MODEL))


PROMPT_ATTN_ROWS = 1024
SAMPLE_ATTN_SEQS = 8


def _layer(x, seq, k, v, hist, p, *, want_vn):
    chunk = min(seq, CHUNK)
    conv_args = (p["conv_w"], p["conv_b"], p["conv_ln_g"], p["conv_ln_b"])
    if hist is None:
        assert p["w_in"].dtype == BF16, "the group with a single token tile goes first and rounds w_in"
        z, b_in, new_hist = _inproj_conv(x, seq, p["attn_norm_g"], p["w_in"], *conv_args, tm=1024)
        cols = Z_COLS_NO_GLU
    else:
        z, w_in16 = _norm_matmul(x, p["attn_norm_g"], p["w_in"], tm=1024, tn=1024, emit_w16=True)
        p = {**p, "w_in": w_in16}
        cols = Z_COLS_ALL
        out_names = ("w_a_out", "w_b_out", "w_c_out", "w_o")
        b_in, new_hist, *rounded = _conv_sample(z, hist, seq, *conv_args, n_seq=16,
                                               also_round=[p[name] for name in out_names])
        p = {**p, **dict(zip(out_names, rounded))}
    sgu_out = _sgu(z, cols, p["sgu_ln_g"], p["sgu_ln_b"], p["sgu_w"], p["sgu_b"], chunk=chunk, rows=1024,
                   want_vn=want_vn)
    if hist is None:
        steps = seq // PROMPT_ATTN_ROWS
        kv_specs = [pl.BlockSpec((1, N_MEM, D_C), functools.partial(lambda i, col: (i // steps, 0, col), col=col))
                    for col in (0, 1)]
        c_in = _attn(z, cols, k, v, kv_specs, rows=PROMPT_ATTN_ROWS, n_seq=1)
    else:
        kv_specs = [pl.BlockSpec((SAMPLE_ATTN_SEQS, N_MEM * HEAD_ROW_PITCH, LANES), lambda i: (i, 0, 0))] * 2
        c_in = _attn(z, cols, k, v, kv_specs, rows=seq, n_seq=SAMPLE_ATTN_SEQS)
    x1, h2 = _merge(sgu_out[0], b_in, c_in, z, cols, x, p["b_gate"], p["w_a_out"], p["w_b_out"], p["w_c_out"],
                    p["w_o"], p["mlp_norm_g"], tm=256)
    if p["w_up"].dtype == F32:
        d, w_up16, w_down16 = _mlp_first(h2, p["w_up"], p["w_down"], tf=512)
        p = {**p, "w_up": w_up16, "w_down": w_down16}
        y = _residual_norm(x1, d, p["final_norm_g"], tm=512)
    else:
        y = _mlp(h2, x1, p["w_up"], p["w_down"], p["final_norm_g"], tm=512, tf=1024)
    return y, new_hist, (sgu_out[1] if want_vn else None), p


def kernel(x_prompt, x_sample, mem_prompt, cache_mem_k, cache_mem_v, state_conv, attn_norm_g, w_in, b_gate,
           sgu_ln_g, sgu_ln_b, sgu_w, sgu_b, w_a_out, conv_w, conv_b, conv_ln_g, conv_ln_b, w_b_out, mem_norm_g,
           w_mem_kv, w_c_out, w_o, mlp_norm_g, w_up, w_down, final_norm_g):
    assert attn_norm_g.shape[0] == 1, "single-layer trunk"
    bp, sp, _ = x_prompt.shape
    bs, ss, _ = x_sample.shape
    p = dict(
        attn_norm_g=attn_norm_g[0], w_in=w_in[0], b_gate=b_gate[0], sgu_ln_g=sgu_ln_g[0],
        sgu_ln_b=sgu_ln_b[0], sgu_w=sgu_w[0], sgu_b=sgu_b[0], w_a_out=w_a_out[0],
        conv_w=conv_w[0], conv_b=conv_b[0], conv_ln_g=conv_ln_g[0], conv_ln_b=conv_ln_b[0],
        w_b_out=w_b_out[0], w_c_out=w_c_out[0], w_o=w_o[0],
        mlp_norm_g=mlp_norm_g[0], w_up=w_up[0], w_down=w_down[0],
        final_norm_g=final_norm_g,
    )
    kv, kv_split = _norm_matmul(mem_prompt.reshape(bp * N_MEM, D_MODEL), mem_norm_g[0], w_mem_kv[0],
                                tm=1024, tn=1024, emit_head_split=True)
    kv = kv.reshape(bp, N_MEM, 2 * D_C)
    y_s, hist_s, vn_s, p = _layer(x_sample.reshape(bs * ss, D_MODEL), ss, _split_lane_tiles(cache_mem_k[0]),
                                  _split_lane_tiles(cache_mem_v[0]), jnp.transpose(state_conv[0], (1, 0, 2)), p,
                                  want_vn=True)
    y_p, hist_p, _, _ = _layer(x_prompt.reshape(bp * sp, D_MODEL), sp, kv, kv, None, p, want_vn=False)
    return (
        y_p.reshape(bp, sp, D_MODEL),
        y_s.reshape(bs, ss, D_MODEL),
        _merge_lane_tiles(kv_split[0], bp)[None],
        _merge_lane_tiles(kv_split[1], bp)[None],
        hist_p[None],
        jnp.transpose(hist_s, (1, 0, 2))[None],
        vn_s.reshape(1, bs, ss, D_A),
    )
```

```python
import functools
import math
from typing import NamedTuple

import jax
import jax.numpy as jnp
from jax import lax
from jax.experimental import pallas as pl
from jax.experimental.pallas import tpu as pltpu

F32 = jnp.float32
BF16 = jnp.bfloat16

D_MODEL = 2048
D_A = 1024
D_B = 1024
D_C = 1024
CHUNK = 128
SGU_GROUPS = 8
SGU_HEAD = D_A // SGU_GROUPS
CONV_WIDTH = 31
HIST = CONV_WIDTH - 1
N_MEM = 256
N_MEM_HEADS = 4
MEM_HEAD_DIM = D_C // N_MEM_HEADS
N_BRANCH = 3
D_FF = 4 * D_MODEL
N_IN = 2 * D_A + 2 * D_B + D_C + N_BRANCH * D_MODEL
EPS = 1e-6

COL_U, COL_V, COL_GLU_A, COL_GLU_B, COL_Q, COL_GATES = 0, 1, 2, 3, 4, 5
COL_W = 1024


class ZCols(NamedTuple):
    u: int
    v: int
    q: int
    gates: int


def _col_block_spec(z, rows, col):
    if z.ndim == 3:
        return pl.BlockSpec((None, rows, COL_W), lambda i: (col, i, 0))
    return pl.BlockSpec((rows, COL_W), lambda i: (i, col))


Z_COLS_ALL = ZCols(COL_U, COL_V, COL_Q, COL_GATES)
Z_COLS_NO_GLU = ZCols(COL_U, COL_V, COL_Q - 2, COL_GATES - 2)

V7X_VMEM_BYTES = 64 * 1024 * 1024
VMEM_LIMIT_BYTES = V7X_VMEM_BYTES - 8 * 1024 * 1024
SUBLANES = 8
LANES = 128
HEAD_LANE_TILES = MEM_HEAD_DIM // LANES
HEAD_ROW_PITCH = N_MEM_HEADS * HEAD_LANE_TILES
V7X_MXU_COLS = 256
PROJ_LANES = V7X_MXU_COLS
PROJ_ROWS = 256
HIST_PAD = 32


def _cparams(*semantics, flags=None):
    return pltpu.CompilerParams(dimension_semantics=semantics, vmem_limit_bytes=VMEM_LIMIT_BYTES, flags=flags)


def _not_before(value, anchor):
    tile = anchor[-SUBLANES:, -LANES:]
    zero = pltpu.bitcast((pltpu.bitcast(tile, jnp.uint32) >> 16) >> 16, F32)
    reps = (value.shape[0] // SUBLANES, value.shape[1] // LANES)
    return value + jnp.tile(zero, reps)


def _rms_norm(x, g):
    ms = jnp.mean(x * x, axis=-1, keepdims=True)
    return x * lax.rsqrt(ms + EPS) * g


def _layer_norm(x, g, b):
    mu = jnp.mean(x, axis=-1, keepdims=True)
    xc = x - mu
    var = jnp.mean(xc * xc, axis=-1, keepdims=True)
    return xc * lax.rsqrt(var + EPS) * g + b


NORM_ROWS = 128


def _norm_matmul_kernel(x_ref, g_ref, w_ref, o_ref, *rest, emit_w16, emit_head_split):
    *extra, h_ref = rest
    w16_ref = extra.pop(0) if emit_w16 else None
    split_ref = extra.pop(0) if emit_head_split else None

    @pl.when(pl.program_id(1) == 0)
    def _():
        def body(r, carry):
            rows = pl.ds(pl.multiple_of(r * NORM_ROWS, NORM_ROWS), NORM_ROWS)
            h_ref[rows, :] = _rms_norm(x_ref[rows, :], g_ref[...]).astype(BF16)
            return carry

        lax.fori_loop(0, x_ref.shape[0] // NORM_ROWS, body, 0)

    tn = w_ref.shape[1]
    for c in range(0, tn, PROJ_LANES):
        w16 = w_ref[:, c : c + PROJ_LANES].astype(BF16)
        if emit_w16:
            w16_ref[:, c : c + PROJ_LANES] = w16
        piece = jnp.dot(h_ref[...], w16, preferred_element_type=F32)
        o_ref[:, c : c + PROJ_LANES] = piece
        if emit_head_split:
            head = ((pl.program_id(1) * tn + c) % D_C) // MEM_HEAD_DIM
            for lt in range(HEAD_LANE_TILES):
                split_ref[pl.ds(lt * N_MEM_HEADS + head, piece.shape[0], stride=HEAD_ROW_PITCH), :] = (
                    piece[:, lt * LANES : (lt + 1) * LANES])


def _norm_matmul(x, g, w, *, tm, tn, emit_w16=False, emit_head_split=False):
    m, k = x.shape
    n = w.shape[1]
    out_shape = [jax.ShapeDtypeStruct((m, n), F32)]
    out_specs = [pl.BlockSpec((tm, tn), lambda i, j: (i, j))]
    if emit_w16:
        assert m == tm, "every weight block must be visited exactly once"
        per_col = COL_W // tn
        out_shape.append(jax.ShapeDtypeStruct((n // COL_W, k, COL_W), BF16))
        out_specs.append(pl.BlockSpec((None, k, tn), lambda i, j: (j // per_col, 0, j % per_col)))
    if emit_head_split:
        assert m == tm and PROJ_LANES == MEM_HEAD_DIM and n == 2 * D_C
        out_shape.append(jax.ShapeDtypeStruct((n // D_C, m * HEAD_ROW_PITCH, LANES), F32))
        out_specs.append(pl.BlockSpec((None, m * HEAD_ROW_PITCH, LANES), lambda i, j: (j // (D_C // tn), 0, 0)))
    outs = pl.pallas_call(
        functools.partial(_norm_matmul_kernel, emit_w16=emit_w16, emit_head_split=emit_head_split),
        out_shape=out_shape,
        grid=(m // tm, n // tn),
        in_specs=[
            pl.BlockSpec((tm, k), lambda i, j: (i, 0), pipeline_mode=pl.Buffered(1 if m == tm else 2)),
            pl.BlockSpec((1, k), lambda i, j: (0, 0)),
            pl.BlockSpec((k, tn), lambda i, j: (0, j)),
        ],
        out_specs=out_specs,
        scratch_shapes=[pltpu.VMEM((tm, k), BF16)],
        compiler_params=_cparams("arbitrary", "arbitrary"),
        name="norm_matmul",
    )(x, g.reshape(1, k), w)
    return outs if len(outs) > 1 else outs[0]


def _lane_periodic(x, period):
    lane = lax.broadcasted_iota(jnp.int32, x.shape, 1)
    y = jnp.where(lane < period, x, 0.0)
    while period < LANES:
        y = y + pltpu.roll(y, period, axis=1)
        period *= 2
    return y


def _sgu_kernel(u_ref, v_ref, g_ref, b_ref, w_ref, sb_ref, a_ref, *vn_refs, chunk):
    r = lax.broadcasted_iota(jnp.int32, (CHUNK, CHUNK), 0)
    c = lax.broadcasted_iota(jnp.int32, (CHUNK, CHUNK), 1)
    seg_bits = chunk.bit_length() - 1
    same_segment = (r >> seg_bits) == (c >> seg_bits)
    mask = jnp.logical_and(same_segment, (r & (chunk - 1)) >= (c & (chunk - 1)))
    sb = sb_ref[...] if chunk == CHUNK else _lane_periodic(sb_ref[...], chunk)
    w_s, bias = [], []
    for grp in range(SGU_GROUPS):
        if chunk == CHUNK:
            w = w_ref[grp]
        else:
            w = _lane_periodic(jnp.tile(w_ref[grp, 0:chunk, :], (CHUNK // chunk, 1)), chunk)
        w_s.append(jnp.where(mask, w, 0.0).astype(BF16))
        col = jnp.sum(jnp.where(r == c, sb[grp : grp + 1, :], 0.0), axis=1, keepdims=True)
        bias.append(jnp.broadcast_to(col, (CHUNK, SGU_HEAD)))
    for blk in range(u_ref.shape[0] // CHUNK):
        rows = slice(blk * CHUNK, (blk + 1) * CHUNK)
        vn = _layer_norm(jax.nn.gelu(v_ref[rows, :]), g_ref[...], b_ref[...])
        if vn_refs:
            vn_refs[0][rows, :] = vn
        vn16 = vn.astype(BF16)
        for grp in range(SGU_GROUPS):
            cols = slice(grp * SGU_HEAD, (grp + 1) * SGU_HEAD)
            mixed = jnp.dot(w_s[grp], vn16[:, cols], preferred_element_type=F32) + bias[grp]
            a_ref[rows, cols] = (jax.nn.gelu(u_ref[rows, cols]) * mixed).astype(BF16)


def _sgu(z, cols, ln_g, ln_b, sgu_w, sgu_b, *, chunk, rows, want_vn):
    m = z.shape[-2]
    out_shape = [jax.ShapeDtypeStruct((m, D_A), BF16)]
    out_specs = [pl.BlockSpec((rows, D_A), lambda i: (i, 0))]
    if want_vn:
        out_shape.append(jax.ShapeDtypeStruct((m, D_A), F32))
        out_specs.append(pl.BlockSpec((rows, D_A), lambda i: (i, 0)))
    return pl.pallas_call(
        functools.partial(_sgu_kernel, chunk=chunk),
        out_shape=out_shape,
        grid=(m // rows,),
        in_specs=[
            _col_block_spec(z, rows, cols.u),
            _col_block_spec(z, rows, cols.v),
            pl.BlockSpec((1, D_A), lambda i: (0, 0)),
            pl.BlockSpec((1, D_A), lambda i: (0, 0)),
            pl.BlockSpec((SGU_GROUPS, CHUNK, CHUNK), lambda i: (0, 0, 0)),
            pl.BlockSpec((SGU_GROUPS, CHUNK), lambda i: (0, 0)),
        ],
        out_specs=out_specs,
        compiler_params=_cparams("arbitrary"),
        name="sgu",
    )(z, z, ln_g.reshape(1, D_A), ln_b.reshape(1, D_A), sgu_w, sgu_b)


CONV_ROWS = 32
CONV_LANES = 128
CONV_LEAD = HIST_PAD - HIST


def _conv_window(win, w_ref, lanes, n_rows):
    out = None
    for b in range(SUBLANES):
        rows_b = n_rows if b == 0 else n_rows + SUBLANES
        y = None
        for a in range(HIST_PAD // SUBLANES + 1):
            k = SUBLANES * a + b - CONV_LEAD
            if 0 <= k < CONV_WIDTH:
                term = w_ref[k : k + 1, lanes] * win[SUBLANES * a : SUBLANES * a + rows_b, :]
                y = term if y is None else y + term
        shifted = y[b : b + n_rows, :]
        out = shifted if out is None else out + shifted
    return out


CONV_STEP_ROWS = 128
CONV_LEAD_PIECES = 2
N_GLU_STEPS = 2
N_COLS = N_IN // COL_W


def _glu_first(j):
    return jnp.where(j < N_GLU_STEPS, j + COL_GLU_A, jnp.where(j < COL_GLU_A + N_GLU_STEPS, j - N_GLU_STEPS, j))


def _inproj_conv_kernel(x_ref, g_ref, w_ref, cw_ref, cb_ref, lg_ref, lb_ref, z_ref, bin_ref, hist_ref,
                        h_ref, ext_ref, *, tiles_per_seq):
    i = pl.program_id(0)
    j = pl.program_id(1)
    t = x_ref.shape[0]

    def proj():
        return jnp.dot(h_ref[...], w_ref[...], preferred_element_type=F32)

    @pl.when(j == 0)
    def _():
        @pl.when(i % tiles_per_seq == 0)
        def _():
            ext_ref[0:HIST_PAD, :] = jnp.zeros((HIST_PAD, D_B), F32)

        @pl.when(i % tiles_per_seq != 0)
        def _():
            ext_ref[0:HIST_PAD, :] = ext_ref[t : t + HIST_PAD, :]

        for rb in range(t // PROJ_ROWS):
            for r in range(rb * PROJ_ROWS, (rb + 1) * PROJ_ROWS, NORM_ROWS):
                h_ref[r : r + NORM_ROWS, :] = _rms_norm(x_ref[r : r + NORM_ROWS, :], g_ref[...]).astype(BF16)
            rows = slice(rb * PROJ_ROWS, (rb + 1) * PROJ_ROWS)
            ext_ref[HIST_PAD + rb * PROJ_ROWS : HIST_PAD + (rb + 1) * PROJ_ROWS, :] = jnp.dot(
                h_ref[rows, :], w_ref[...], preferred_element_type=F32)

    @pl.when(j == 1)
    def _():
        ext_ref[HIST_PAD : HIST_PAD + t, :] = ext_ref[HIST_PAD : HIST_PAD + t, :] * jax.nn.sigmoid(proj())
        hist_ref[0] = ext_ref[t + CONV_LEAD : t + HIST_PAD, :]

    @pl.when(j >= N_GLU_STEPS)
    def _():
        chunk = jnp.minimum(j - N_GLU_STEPS, t // CONV_STEP_ROWS - 1)
        r0 = pl.multiple_of(chunk * CONV_STEP_ROWS, CONV_STEP_ROWS)
        n_row_blocks = CONV_STEP_ROWS // CONV_ROWS
        n_lane_blocks = D_B // PROJ_LANES
        rows_per_block = t // n_row_blocks
        pieces = [(rc, lb) for rc in range(n_row_blocks) for lb in range(n_lane_blocks)]
        vec_done, proj_done = [], []
        parts = []

        def conv_piece(p):
            rc, lb = pieces[p]
            wins = []
            for cl in range(lb * PROJ_LANES, (lb + 1) * PROJ_LANES, CONV_LANES):
                clanes = slice(cl, cl + CONV_LANES)
                win = ext_ref[pl.ds(r0 + rc * CONV_ROWS, CONV_ROWS + HIST_PAD), clanes]
                if p > CONV_LEAD_PIECES:
                    win = _not_before(win, proj_done[p - CONV_LEAD_PIECES - 1])
                wins.append(_conv_window(win, cw_ref, clanes, CONV_ROWS))
            parts.append(jnp.concatenate(wins, axis=-1))
            vec_done.append(parts[-1])
            if lb == n_lane_blocks - 1:
                dc = jnp.concatenate(parts, axis=-1) + cb_ref[...]
                parts.clear()
                y = jax.nn.silu(_layer_norm(dc, lg_ref[...], lb_ref[...]))
                bin_ref[pl.ds(r0 + rc * CONV_ROWS, CONV_ROWS), :] = y.astype(BF16)
                vec_done.append(y)

        n_conv = 0
        for q, (rc, lb) in enumerate(pieces):
            while n_conv < min(q + CONV_LEAD_PIECES + 1, len(pieces)):
                conv_piece(n_conv)
                n_conv += 1
            rows = slice(rc * rows_per_block, (rc + 1) * rows_per_block)
            lanes = slice(lb * PROJ_LANES, (lb + 1) * PROJ_LANES)
            zp = jnp.dot(h_ref[rows, :], w_ref[:, lanes], preferred_element_type=F32)
            if len(vec_done) >= 2:
                zp = _not_before(zp, vec_done[-2])
            z_ref[rows, lanes] = zp
            proj_done.append(zp)


def _inproj_conv(x, seq, g, w16, conv_w, conv_b, ln_g, ln_b, *, tm):
    m, k = x.shape
    tiles_per_seq = seq // tm
    n_conv_steps = N_COLS - N_GLU_STEPS
    assert tm // CONV_STEP_ROWS <= n_conv_steps, "not enough grid steps to convolve the whole tile"
    const = lambda i, j: (0, 0)
    return pl.pallas_call(
        functools.partial(_inproj_conv_kernel, tiles_per_seq=tiles_per_seq),
        out_shape=[
            jax.ShapeDtypeStruct((n_conv_steps, m, COL_W), F32),
            jax.ShapeDtypeStruct((m, D_B), BF16),
            jax.ShapeDtypeStruct((m // seq, HIST, D_B), F32),
        ],
        grid=(m // tm, N_COLS),
        in_specs=[
            pl.BlockSpec((tm, k), lambda i, j: (i, 0)),
            pl.BlockSpec((1, k), const),
            pl.BlockSpec((None, k, COL_W), lambda i, j: (_glu_first(j), 0, 0)),
            pl.BlockSpec((CONV_WIDTH, D_B), const),
            pl.BlockSpec((1, D_B), const),
            pl.BlockSpec((1, D_B), const),
            pl.BlockSpec((1, D_B), const),
        ],
        out_specs=[
            pl.BlockSpec((None, tm, COL_W), lambda i, j: (jnp.maximum(j - N_GLU_STEPS, 0), i, 0)),
            pl.BlockSpec((tm, D_B), lambda i, j: (i, 0)),
            pl.BlockSpec((1, HIST, D_B), lambda i, j: (i // tiles_per_seq, 0, 0)),
        ],
        scratch_shapes=[pltpu.VMEM((tm, k), BF16), pltpu.VMEM((tm + HIST_PAD, D_B), F32)],
        compiler_params=_cparams("arbitrary", "arbitrary", ),
        name="inproj_conv",
    )(x, g.reshape(1, k), w16, conv_w, conv_b.reshape(1, D_B), ln_g.reshape(1, D_B), ln_b.reshape(1, D_B))


def _conv_sample_kernel(*refs, seq, n_round):
    ga_ref, gb_ref, hist_ref, w_ref, cb_ref, g_ref, b_ref = refs[:7]
    f32_refs = refs[7 : 7 + n_round]
    o_ref, nh_ref = refs[7 + n_round : 9 + n_round]
    bf16_refs = refs[9 + n_round : 9 + 2 * n_round]
    cs_ref, dc_ref, os_ref = refs[9 + 2 * n_round :]
    for src, dst in zip(f32_refs, bf16_refs):
        dst[...] = src[...].astype(BF16)
    n_seq = hist_ref.shape[1]
    n_slabs = D_B // LANES
    c = ga_ref[...] * jax.nn.sigmoid(gb_ref[...])
    for l in range(n_slabs):
        cs_ref[l] = c[:, l * LANES : (l + 1) * LANES]
    new = [jnp.concatenate([cs_ref[l, pl.ds(t, n_seq, stride=seq), :] for l in range(n_slabs)], axis=-1)
           for t in range(seq)]
    nh_ref[0 : HIST - seq] = hist_ref[seq:HIST]
    for t in range(seq):
        nh_ref[HIST - seq + t] = new[t]
    for lb in range(D_B // CONV_LANES):
        lanes = slice(lb * CONV_LANES, (lb + 1) * CONV_LANES)
        ext = [hist_ref[j, :, lanes] for j in range(HIST)] + [new[t][:, lanes] for t in range(seq)]
        for t in range(seq):
            acc = w_ref[0:1, lanes] * ext[t]
            for k in range(1, CONV_WIDTH):
                acc = acc + w_ref[k : k + 1, lanes] * ext[t + k]
            dc_ref[t, :, lanes] = acc
    for t in range(seq):
        y = jax.nn.silu(_layer_norm(dc_ref[t] + cb_ref[...], g_ref[...], b_ref[...]))
        for l in range(n_slabs):
            os_ref[l, pl.ds(t, n_seq, stride=seq), :] = y[:, l * LANES : (l + 1) * LANES]
    for l in range(n_slabs):
        o_ref[:, l * LANES : (l + 1) * LANES] = os_ref[l].astype(BF16)


def _conv_sample(z, hist, seq, conv_w, conv_b, ln_g, ln_b, *, n_seq, also_round=()):
    batch = hist.shape[1]
    steps = batch // n_seq
    slab_specs = [pl.BlockSpec((w.shape[0] // steps, w.shape[1]), lambda i: (i, 0)) for w in also_round]
    slab_scratch = pltpu.VMEM((D_B // LANES, n_seq * seq, LANES), F32)
    return pl.pallas_call(
        functools.partial(_conv_sample_kernel, seq=seq, n_round=len(also_round)),
        out_shape=[
            jax.ShapeDtypeStruct((batch * seq, D_B), BF16),
            jax.ShapeDtypeStruct((HIST, batch, D_B), F32),
        ] + [jax.ShapeDtypeStruct(w.shape, BF16) for w in also_round],
        grid=(steps,),
        in_specs=[
            pl.BlockSpec((n_seq * seq, COL_W), lambda i: (i, COL_GLU_A)),
            pl.BlockSpec((n_seq * seq, COL_W), lambda i: (i, COL_GLU_B)),
            pl.BlockSpec((HIST, n_seq, D_B), lambda i: (0, i, 0)),
            pl.BlockSpec((CONV_WIDTH, D_B), lambda i: (0, 0)),
            pl.BlockSpec((1, D_B), lambda i: (0, 0)),
            pl.BlockSpec((1, D_B), lambda i: (0, 0)),
            pl.BlockSpec((1, D_B), lambda i: (0, 0)),
        ] + slab_specs,
        out_specs=[
            pl.BlockSpec((n_seq * seq, D_B), lambda i: (i, 0)),
            pl.BlockSpec((HIST, n_seq, D_B), lambda i: (0, i, 0)),
        ] + slab_specs,
        scratch_shapes=[slab_scratch, pltpu.VMEM((seq, n_seq, D_B), F32), slab_scratch],
        compiler_params=_cparams("arbitrary"),
        name="conv_sample",
    )(z, z, hist, conv_w, conv_b.reshape(1, D_B), ln_g.reshape(1, D_B), ln_b.reshape(1, D_B), *also_round)


def _head_cols(h):
    return slice(h * MEM_HEAD_DIM, (h + 1) * MEM_HEAD_DIM)


def _split_lane_tiles(kv):
    b = kv.shape[0]
    kv = kv.reshape(b, N_MEM, N_MEM_HEADS, HEAD_LANE_TILES, LANES)
    return kv.transpose(0, 1, 3, 2, 4).reshape(b, N_MEM * HEAD_ROW_PITCH, LANES)


def _merge_lane_tiles(view, batch):
    view = view.reshape(batch, N_MEM, HEAD_LANE_TILES, N_MEM_HEADS, LANES)
    return view.transpose(0, 1, 3, 2, 4).reshape(batch, N_MEM, N_MEM_HEADS, MEM_HEAD_DIM)


def _head_of(kv_ref, s, h):
    if kv_ref.shape[-1] == LANES:
        tiles = [kv_ref[s, pl.ds(j * N_MEM_HEADS + h, N_MEM, stride=HEAD_ROW_PITCH), :]
                 for j in range(HEAD_LANE_TILES)]
        return jnp.concatenate(tiles, axis=-1).astype(BF16)
    return kv_ref[s, :, _head_cols(h)].astype(BF16)


def _attn_kernel(q_ref, k_ref, v_ref, o_ref, *, rows):
    scale = 1.0 / math.sqrt(MEM_HEAD_DIM)
    pairs = [(s, h) for s in range(k_ref.shape[0]) for h in range(N_MEM_HEADS)]
    scores = []
    for s, h in pairs:
        q = (q_ref[s * rows : (s + 1) * rows, _head_cols(h)] * scale).astype(BF16)
        scores.append(lax.dot_general(q, _head_of(k_ref, s, h), (((1,), (1,)), ((), ())),
                                      preferred_element_type=F32))
    sc = jnp.concatenate(scores, axis=0)
    p = jnp.exp(sc - jnp.max(sc, axis=-1, keepdims=True))
    p = p / jnp.sum(p, axis=-1, keepdims=True)
    for n, (s, h) in enumerate(pairs):
        ph = p[n * rows : (n + 1) * rows, :].astype(BF16)
        o = jnp.dot(ph, _head_of(v_ref, s, h), preferred_element_type=F32)
        o_ref[s * rows : (s + 1) * rows, _head_cols(h)] = o.astype(BF16)


def _merge_kernel(a_ref, b_ref, c_ref, *refs):
    gate_refs = refs[: 2 * N_BRANCH]
    bg_ref, x_ref, wa_ref, wb_ref, wc_ref, wo_ref, mg_ref, x1_ref, h2_ref = refs[2 * N_BRANCH :]
    halves = []
    for half in range(2):
        cols = slice(half * COL_W, (half + 1) * COL_W)
        merged = None
        for br, (in_ref, w_ref) in enumerate(((a_ref, wa_ref), (b_ref, wb_ref), (c_ref, wc_ref))):
            y = jnp.dot(in_ref[...], w_ref[:, cols], preferred_element_type=F32)
            gcols = slice((2 * br + half) * COL_W, (2 * br + half + 1) * COL_W)
            term = jax.nn.sigmoid(gate_refs[2 * br + half][...] + bg_ref[:, gcols]) * y
            merged = term if merged is None else merged + term
        halves.append(merged.astype(BF16))
    merged16 = jnp.concatenate(halves, axis=-1)
    x1 = x_ref[...] + jnp.dot(merged16, wo_ref[...], preferred_element_type=F32)
    x1_ref[...] = x1
    h2_ref[...] = _rms_norm(x1, mg_ref[...]).astype(BF16)


def _merge(a_in, b_in, c_in, z, cols, x, b_gate, w_a, w_b, w_c, w_o, mlp_g, *, tm):
    m = x.shape[0]
    const = lambda i: (0, 0)
    resident = functools.partial(pl.BlockSpec, index_map=const, pipeline_mode=pl.Buffered(1))
    gate_specs = [_col_block_spec(z, tm, cols.gates + n) for n in range(2 * N_BRANCH)]
    return pl.pallas_call(
        _merge_kernel,
        out_shape=[jax.ShapeDtypeStruct((m, D_MODEL), F32), jax.ShapeDtypeStruct((m, D_MODEL), BF16)],
        grid=(m // tm,),
        in_specs=[pl.BlockSpec((tm, D_A), lambda i: (i, 0))] * 3
        + gate_specs
        + [
            pl.BlockSpec((1, N_BRANCH * D_MODEL), const),
            pl.BlockSpec((tm, D_MODEL), lambda i: (i, 0)),
            resident((D_A, D_MODEL)),
            resident((D_B, D_MODEL)),
            resident((D_C, D_MODEL)),
            resident((D_MODEL, D_MODEL)),
            pl.BlockSpec((1, D_MODEL), const),
        ],
        out_specs=[pl.BlockSpec((tm, D_MODEL), lambda i: (i, 0))] * 2,
        compiler_params=_cparams("arbitrary"),
        name="merge",
    )(a_in, b_in, c_in, *([z] * (2 * N_BRANCH)), b_gate.reshape(1, -1), x, w_a, w_b, w_c, w_o,
      mlp_g.reshape(1, D_MODEL))


def _mlp_kernel(h2_ref, x1_ref, wu_ref, wd_ref, fg_ref, y_ref, acc_ref):
    f = pl.program_id(1)

    def ffn():
        t = jnp.square(jnp.maximum(jnp.dot(h2_ref[...], wu_ref[...], preferred_element_type=F32), 0.0))
        return jnp.dot(t.astype(BF16), wd_ref[...], preferred_element_type=F32)

    @pl.when(f == 0)
    def _():
        acc_ref[...] = x1_ref[...] + ffn()

    @pl.when(f > 0)
    def _():
        acc_ref[...] += ffn()

    @pl.when(f == pl.num_programs(1) - 1)
    def _():
        y_ref[...] = _rms_norm(acc_ref[...], fg_ref[...])


def _mlp(h2, x1, w_up, w_down, final_g, *, tm, tf):
    m = h2.shape[0]
    return pl.pallas_call(
        _mlp_kernel,
        out_shape=jax.ShapeDtypeStruct((m, D_MODEL), F32),
        grid=(m // tm, D_FF // tf),
        in_specs=[
            pl.BlockSpec((tm, D_MODEL), lambda i, f: (i, 0)),
            pl.BlockSpec((tm, D_MODEL), lambda i, f: (i, 0)),
            pl.BlockSpec((D_MODEL, tf), lambda i, f: (0, f)),
            pl.BlockSpec((tf, D_MODEL), lambda i, f: (f, 0)),
            pl.BlockSpec((1, D_MODEL), lambda i, f: (0, 0)),
        ],
        out_specs=pl.BlockSpec((tm, D_MODEL), lambda i, f: (i, 0)),
        scratch_shapes=[pltpu.VMEM((tm, D_MODEL), F32)],
        compiler_params=_cparams("arbitrary", "arbitrary"),
        name="mlp",
    )(h2, x1, w_up, w_down, final_g.reshape(1, D_MODEL))


def _mlp_first_kernel(h2_ref, wu_ref, wd_ref, acc_ref, wu16_ref, wd16_ref):
    @pl.when(pl.program_id(0) == 0)
    def _():
        acc_ref[...] = jnp.zeros(acc_ref.shape, F32)

    h2 = h2_ref[...]
    t_pieces = []
    for c in range(0, wu_ref.shape[1], PROJ_LANES):
        wu = wu_ref[:, c : c + PROJ_LANES].astype(BF16)
        wu16_ref[:, c : c + PROJ_LANES] = wu
        t = jnp.square(jnp.maximum(jnp.dot(h2, wu, preferred_element_type=F32), 0.0))
        t_pieces.append(t.astype(BF16))
    t16 = jnp.concatenate(t_pieces, axis=-1)
    for c in range(0, wd_ref.shape[1], PROJ_LANES):
        wd = wd_ref[:, c : c + PROJ_LANES].astype(BF16)
        wd16_ref[:, c : c + PROJ_LANES] = wd
        acc_ref[:, c : c + PROJ_LANES] += jnp.dot(t16, wd, preferred_element_type=F32)


def _mlp_first(h2, w_up, w_down, *, tf):
    m = h2.shape[0]
    resident = pl.BlockSpec((m, D_MODEL), lambda f: (0, 0))
    return pl.pallas_call(
        _mlp_first_kernel,
        out_shape=[
            jax.ShapeDtypeStruct((m, D_MODEL), F32),
            jax.ShapeDtypeStruct((D_MODEL, D_FF), BF16),
            jax.ShapeDtypeStruct((D_FF, D_MODEL), BF16),
        ],
        grid=(D_FF // tf,),
        in_specs=[
            resident,
            pl.BlockSpec((D_MODEL, tf), lambda f: (0, f)),
            pl.BlockSpec((tf, D_MODEL), lambda f: (f, 0)),
        ],
        out_specs=[
            resident,
            pl.BlockSpec((D_MODEL, tf), lambda f: (0, f)),
            pl.BlockSpec((tf, D_MODEL), lambda f: (f, 0)),
        ],
        compiler_params=_cparams("arbitrary"),
        name="mlp_first",
    )(h2, w_up, w_down)


def _residual_norm_kernel(x_ref, d_ref, g_ref, y_ref):
    y_ref[...] = _rms_norm(x_ref[...] + d_ref[...], g_ref[...])


def _residual_norm(x, d, g, *, tm):
    m = x.shape[0]
    row_spec = pl.BlockSpec((tm, D_MODEL), lambda i: (i, 0))
    return pl.pallas_call(
        _residual_norm_kernel,
        out_shape=jax.ShapeDtypeStruct((m, D_MODEL), F32),
        grid=(m // tm,),
        in_specs=[row_spec, row_spec, pl.BlockSpec((1, D_MODEL), lambda i: (0, 0))],
        out_specs=row_spec,
        compiler_params=_cparams("arbitrary"),
        name="residual_norm",
    )(x, d, g.reshape(1, D_MODEL))


SAMPLE_ATTN_SEQS = 4


def _branches_kernel(u_ref, v_ref, g_ref, b_ref, sw_ref, sb_ref, qp_ref, kp_ref, vp_ref, qs_ref, ks_ref, vs_ref,
                     a_ref, op_ref, os_ref, *, chunk, rows_p, seq_s):
    _sgu_kernel(u_ref, v_ref, g_ref, b_ref, sw_ref, sb_ref, a_ref, chunk=chunk)
    _attn_kernel(qp_ref, kp_ref, vp_ref, op_ref, rows=rows_p)
    _attn_kernel(qs_ref, ks_ref, vs_ref, os_ref, rows=seq_s)


def _branches(z_p, cols_p, seq_p, kv_p, z_s, cols_s, seq_s, k_s, v_s, ln_g, ln_b, sgu_w, sgu_b):
    m_p, m_s = z_p.shape[-2], z_s.shape[-2]
    steps = m_s // (seq_s * SAMPLE_ATTN_SEQS)
    rows_p = m_p // steps
    assert rows_p % CHUNK == 0 and seq_p % rows_p == 0
    steps_per_seq = seq_p // rows_p
    rows_s = seq_s * SAMPLE_ATTN_SEQS
    const2 = lambda i: (0, 0)
    kv_p_specs = [pl.BlockSpec((1, N_MEM, D_C), functools.partial(lambda i, col: (i // steps_per_seq, 0, col), col=col))
                  for col in (0, 1)]
    kv_s_spec = pl.BlockSpec((SAMPLE_ATTN_SEQS, N_MEM * HEAD_ROW_PITCH, LANES), lambda i: (i, 0, 0))
    return pl.pallas_call(
        functools.partial(_branches_kernel, chunk=min(seq_p, CHUNK), rows_p=rows_p, seq_s=seq_s),
        out_shape=[
            jax.ShapeDtypeStruct((m_p, D_A), BF16),
            jax.ShapeDtypeStruct((m_p, D_C), BF16),
            jax.ShapeDtypeStruct((m_s, D_C), BF16),
        ],
        grid=(steps,),
        in_specs=[
            _col_block_spec(z_p, rows_p, cols_p.u),
            _col_block_spec(z_p, rows_p, cols_p.v),
            pl.BlockSpec((1, D_A), const2),
            pl.BlockSpec((1, D_A), const2),
            pl.BlockSpec((SGU_GROUPS, CHUNK, CHUNK), lambda i: (0, 0, 0)),
            pl.BlockSpec((SGU_GROUPS, CHUNK), const2),
            _col_block_spec(z_p, rows_p, cols_p.q),
            *kv_p_specs,
            _col_block_spec(z_s, rows_s, cols_s.q),
            kv_s_spec,
            kv_s_spec,
        ],
        out_specs=[
            pl.BlockSpec((rows_p, D_A), lambda i: (i, 0)),
            pl.BlockSpec((rows_p, D_C), lambda i: (i, 0)),
            pl.BlockSpec((rows_s, D_C), lambda i: (i, 0)),
        ],
        compiler_params=_cparams("arbitrary"),
        name="branches",
    )(z_p, z_p, ln_g.reshape(1, D_A), ln_b.reshape(1, D_A), sgu_w, sgu_b, z_p, kv_p, kv_p, z_s, k_s, v_s)


def kernel(x_prompt, x_sample, mem_prompt, cache_mem_k, cache_mem_v, state_conv, attn_norm_g, w_in, b_gate,
           sgu_ln_g, sgu_ln_b, sgu_w, sgu_b, w_a_out, conv_w, conv_b, conv_ln_g, conv_ln_b, w_b_out, mem_norm_g,
           w_mem_kv, w_c_out, w_o, mlp_norm_g, w_up, w_down, final_norm_g):
    assert attn_norm_g.shape[0] == 1, "single-layer trunk"
    bp, sp, _ = x_prompt.shape
    bs, ss, _ = x_sample.shape
    p = dict(
        attn_norm_g=attn_norm_g[0], w_in=w_in[0], b_gate=b_gate[0], sgu_ln_g=sgu_ln_g[0],
        sgu_ln_b=sgu_ln_b[0], sgu_w=sgu_w[0], sgu_b=sgu_b[0], w_a_out=w_a_out[0],
        conv_w=conv_w[0], conv_b=conv_b[0], conv_ln_g=conv_ln_g[0], conv_ln_b=conv_ln_b[0],
        w_b_out=w_b_out[0], w_c_out=w_c_out[0], w_o=w_o[0],
        mlp_norm_g=mlp_norm_g[0], w_up=w_up[0], w_down=w_down[0],
        final_norm_g=final_norm_g,
    )
    kv, kv_split = _norm_matmul(mem_prompt.reshape(bp * N_MEM, D_MODEL), mem_norm_g[0], w_mem_kv[0],
                                tm=1024, tn=1024, emit_head_split=True)
    kv = kv.reshape(bp, N_MEM, 2 * D_C)
    xs = x_sample.reshape(bs * ss, D_MODEL)
    xp = x_prompt.reshape(bp * sp, D_MODEL)
    conv_args = (p["conv_w"], p["conv_b"], p["conv_ln_g"], p["conv_ln_b"])
    sgu_args = (p["sgu_ln_g"], p["sgu_ln_b"], p["sgu_w"], p["sgu_b"])

    z_s, w_in16 = _norm_matmul(xs, p["attn_norm_g"], p["w_in"], tm=1024, tn=1024, emit_w16=True)
    out_names = ("w_a_out", "w_b_out", "w_c_out", "w_o")
    b_s, hist_s, *rounded = _conv_sample(z_s, jnp.transpose(state_conv[0], (1, 0, 2)), ss, *conv_args, n_seq=16,
                                         also_round=[p[name] for name in out_names])
    p = {**p, **dict(zip(out_names, rounded))}
    a_s, vn_s = _sgu(z_s, Z_COLS_ALL, *sgu_args, chunk=min(ss, CHUNK), rows=1024, want_vn=True)

    z_p, b_p, hist_p = _inproj_conv(xp, sp, p["attn_norm_g"], w_in16, *conv_args, tm=1024)
    a_p, c_p, c_s = _branches(z_p, Z_COLS_NO_GLU, sp, kv, z_s, Z_COLS_ALL, ss, _split_lane_tiles(cache_mem_k[0]),
                              _split_lane_tiles(cache_mem_v[0]), *sgu_args)

    merge_w = (p["b_gate"], p["w_a_out"], p["w_b_out"], p["w_c_out"], p["w_o"], p["mlp_norm_g"])
    x1_s, h2_s = _merge(a_s, b_s, c_s, z_s, Z_COLS_ALL, xs, *merge_w, tm=256)
    d_s, w_up16, w_down16 = _mlp_first(h2_s, p["w_up"], p["w_down"], tf=512)
    y_s = _residual_norm(x1_s, d_s, p["final_norm_g"], tm=512)
    x1_p, h2_p = _merge(a_p, b_p, c_p, z_p, Z_COLS_NO_GLU, xp, *merge_w, tm=256)
    y_p = _mlp(h2_p, x1_p, w_up16, w_down16, p["final_norm_g"], tm=512, tf=1024)
    return (
        y_p.reshape(bp, sp, D_MODEL),
        y_s.reshape(bs, ss, D_MODEL),
        _merge_lane_tiles(kv_split[0], bp)[None],
        _merge_lane_tiles(kv_split[1], bp)[None],
        hist_p[None],
        jnp.transpose(hist_s, (1, 0, 2))[None],
        vn_s.reshape(1, bs, ss, D_A),
    )
```

```python
import functools
import math
from typing import NamedTuple

import jax
import jax.numpy as jnp
from jax import lax
from jax.experimental import pallas as pl
from jax.experimental.pallas import tpu as pltpu

F32 = jnp.float32
BF16 = jnp.bfloat16

D_MODEL = 2048
D_A = 1024
D_B = 1024
D_C = 1024
CHUNK = 128
SGU_GROUPS = 8
SGU_HEAD = D_A // SGU_GROUPS
CONV_WIDTH = 31
HIST = CONV_WIDTH - 1
N_MEM = 256
N_MEM_HEADS = 4
MEM_HEAD_DIM = D_C // N_MEM_HEADS
N_BRANCH = 3
D_FF = 4 * D_MODEL
N_IN = 2 * D_A + 2 * D_B + D_C + N_BRANCH * D_MODEL
EPS = 1e-6

COL_U, COL_V, COL_GLU_A, COL_GLU_B, COL_Q, COL_GATES = 0, 1, 2, 3, 4, 5
COL_W = 1024


class ZCols(NamedTuple):
    u: int
    v: int
    q: int
    gates: int


def _col_block_spec(z, rows, col):
    if z.ndim == 3:
        return pl.BlockSpec((None, rows, COL_W), lambda i: (col, i, 0))
    return pl.BlockSpec((rows, COL_W), lambda i: (i, col))


Z_COLS_ALL = ZCols(COL_U, COL_V, COL_Q, COL_GATES)
Z_COLS_NO_GLU = ZCols(COL_U, COL_V, COL_Q - 2, COL_GATES - 2)

V7X_VMEM_BYTES = 64 * 1024 * 1024
VMEM_LIMIT_BYTES = V7X_VMEM_BYTES - 8 * 1024 * 1024
SUBLANES = 8
LANES = 128
HEAD_LANE_TILES = MEM_HEAD_DIM // LANES
HEAD_ROW_PITCH = N_MEM_HEADS * HEAD_LANE_TILES
V7X_MXU_COLS = 256
PROJ_LANES = V7X_MXU_COLS
PROJ_ROWS = 256
HIST_PAD = 32


def _cparams(*semantics, flags=None):
    return pltpu.CompilerParams(dimension_semantics=semantics, vmem_limit_bytes=VMEM_LIMIT_BYTES, flags=flags)


def _not_before(value, anchor):
    tile = anchor[-SUBLANES:, -LANES:]
    zero = pltpu.bitcast((pltpu.bitcast(tile, jnp.uint32) >> 16) >> 16, F32)
    reps = (value.shape[0] // SUBLANES, value.shape[1] // LANES)
    return value + jnp.tile(zero, reps)


def _rms_norm(x, g):
    ms = jnp.mean(x * x, axis=-1, keepdims=True)
    return x * lax.rsqrt(ms + EPS) * g


def _layer_norm(x, g, b):
    mu = jnp.mean(x, axis=-1, keepdims=True)
    xc = x - mu
    var = jnp.mean(xc * xc, axis=-1, keepdims=True)
    return xc * lax.rsqrt(var + EPS) * g + b


NORM_ROWS = 128


def _norm_matmul_kernel(x_ref, g_ref, w_ref, o_ref, *rest, emit_w16, emit_head_split):
    *extra, h_ref = rest
    w16_ref = extra.pop(0) if emit_w16 else None
    split_refs = extra if emit_head_split else None

    @pl.when(pl.program_id(1) == 0)
    def _():
        def body(r, carry):
            rows = pl.ds(pl.multiple_of(r * NORM_ROWS, NORM_ROWS), NORM_ROWS)
            h_ref[rows, :] = _rms_norm(x_ref[rows, :], g_ref[...]).astype(BF16)
            return carry

        lax.fori_loop(0, x_ref.shape[0] // NORM_ROWS, body, 0)

    tn = w_ref.shape[1]
    for c in range(0, tn, PROJ_LANES):
        w16 = w_ref[:, c : c + PROJ_LANES].astype(BF16)
        if emit_w16:
            w16_ref[:, c : c + PROJ_LANES] = w16
        piece = jnp.dot(h_ref[...], w16, preferred_element_type=F32)
        o_ref[:, c : c + PROJ_LANES] = piece
        if emit_head_split:
            col = pl.program_id(1) * tn + c
            head = (col % D_C) // MEM_HEAD_DIM
            for n, split_ref in enumerate(split_refs):
                @pl.when(col // D_C == n)
                def _(split_ref=split_ref, head=head, piece=piece):
                    for lt in range(HEAD_LANE_TILES):
                        split_ref[pl.ds(lt * N_MEM_HEADS + head, piece.shape[0], stride=HEAD_ROW_PITCH), :] = (
                            piece[:, lt * LANES : (lt + 1) * LANES])


def _norm_matmul(x, g, w, *, tm, tn, emit_w16=False, emit_head_split=False):
    m, k = x.shape
    n = w.shape[1]
    out_shape = [jax.ShapeDtypeStruct((m, n), F32)]
    out_specs = [pl.BlockSpec((tm, tn), lambda i, j: (i, j))]
    if emit_w16:
        assert m == tm, "every weight block must be visited exactly once"
        per_col = COL_W // tn
        out_shape.append(jax.ShapeDtypeStruct((n // COL_W, k, COL_W), BF16))
        out_specs.append(pl.BlockSpec((None, k, tn), lambda i, j: (j // per_col, 0, j % per_col)))
    if emit_head_split:
        assert m == tm and PROJ_LANES == MEM_HEAD_DIM and n % D_C == 0
        out_shape += [jax.ShapeDtypeStruct((m * HEAD_ROW_PITCH, LANES), F32)] * (n // D_C)
        out_specs += [pl.BlockSpec((m * HEAD_ROW_PITCH, LANES), lambda i, j: (0, 0))] * (n // D_C)
    outs = pl.pallas_call(
        functools.partial(_norm_matmul_kernel, emit_w16=emit_w16, emit_head_split=emit_head_split),
        out_shape=out_shape,
        grid=(m // tm, n // tn),
        in_specs=[
            pl.BlockSpec((tm, k), lambda i, j: (i, 0), pipeline_mode=pl.Buffered(1 if m == tm else 2)),
            pl.BlockSpec((1, k), lambda i, j: (0, 0)),
            pl.BlockSpec((k, tn), lambda i, j: (0, j)),
        ],
        out_specs=out_specs,
        scratch_shapes=[pltpu.VMEM((tm, k), BF16)],
        compiler_params=_cparams("arbitrary", "arbitrary"),
        name="norm_matmul",
    )(x, g.reshape(1, k), w)
    return outs if len(outs) > 1 else outs[0]


def _lane_periodic(x, period):
    lane = lax.broadcasted_iota(jnp.int32, x.shape, 1)
    y = jnp.where(lane < period, x, 0.0)
    while period < LANES:
        y = y + pltpu.roll(y, period, axis=1)
        period *= 2
    return y


def _sgu_kernel(u_ref, v_ref, g_ref, b_ref, w_ref, sb_ref, a_ref, *vn_refs, chunk):
    r = lax.broadcasted_iota(jnp.int32, (CHUNK, CHUNK), 0)
    c = lax.broadcasted_iota(jnp.int32, (CHUNK, CHUNK), 1)
    seg_bits = chunk.bit_length() - 1
    same_segment = (r >> seg_bits) == (c >> seg_bits)
    mask = jnp.logical_and(same_segment, (r & (chunk - 1)) >= (c & (chunk - 1)))
    sb = sb_ref[...] if chunk == CHUNK else _lane_periodic(sb_ref[...], chunk)
    w_s, bias = [], []
    for grp in range(SGU_GROUPS):
        if chunk == CHUNK:
            w = w_ref[grp]
        else:
            w = _lane_periodic(jnp.tile(w_ref[grp, 0:chunk, :], (CHUNK // chunk, 1)), chunk)
        w_s.append(jnp.where(mask, w, 0.0).astype(BF16))
        col = jnp.sum(jnp.where(r == c, sb[grp : grp + 1, :], 0.0), axis=1, keepdims=True)
        bias.append(jnp.broadcast_to(col, (CHUNK, SGU_HEAD)))
    for blk in range(u_ref.shape[0] // CHUNK):
        rows = slice(blk * CHUNK, (blk + 1) * CHUNK)
        vn = _layer_norm(jax.nn.gelu(v_ref[rows, :]), g_ref[...], b_ref[...])
        if vn_refs:
            vn_refs[0][rows, :] = vn
        vn16 = vn.astype(BF16)
        for grp in range(SGU_GROUPS):
            cols = slice(grp * SGU_HEAD, (grp + 1) * SGU_HEAD)
            mixed = jnp.dot(w_s[grp], vn16[:, cols], preferred_element_type=F32) + bias[grp]
            a_ref[rows, cols] = (jax.nn.gelu(u_ref[rows, cols]) * mixed).astype(BF16)


def _sgu(z, cols, ln_g, ln_b, sgu_w, sgu_b, *, chunk, rows, want_vn):
    m = z.shape[-2]
    out_shape = [jax.ShapeDtypeStruct((m, D_A), BF16)]
    out_specs = [pl.BlockSpec((rows, D_A), lambda i: (i, 0))]
    if want_vn:
        out_shape.append(jax.ShapeDtypeStruct((m, D_A), F32))
        out_specs.append(pl.BlockSpec((rows, D_A), lambda i: (i, 0)))
    return pl.pallas_call(
        functools.partial(_sgu_kernel, chunk=chunk),
        out_shape=out_shape,
        grid=(m // rows,),
        in_specs=[
            _col_block_spec(z, rows, cols.u),
            _col_block_spec(z, rows, cols.v),
            pl.BlockSpec((1, D_A), lambda i: (0, 0)),
            pl.BlockSpec((1, D_A), lambda i: (0, 0)),
            pl.BlockSpec((SGU_GROUPS, CHUNK, CHUNK), lambda i: (0, 0, 0)),
            pl.BlockSpec((SGU_GROUPS, CHUNK), lambda i: (0, 0)),
        ],
        out_specs=out_specs,
        compiler_params=_cparams("arbitrary"),
        name="sgu",
    )(z, z, ln_g.reshape(1, D_A), ln_b.reshape(1, D_A), sgu_w, sgu_b)


CONV_ROWS = 32
CONV_LANES = 128
CONV_LEAD = HIST_PAD - HIST


def _conv_window(win, w_ref, lanes, n_rows):
    out = None
    for b in range(SUBLANES):
        rows_b = n_rows if b == 0 else n_rows + SUBLANES
        y = None
        for a in range(HIST_PAD // SUBLANES + 1):
            k = SUBLANES * a + b - CONV_LEAD
            if 0 <= k < CONV_WIDTH:
                term = w_ref[k : k + 1, lanes] * win[SUBLANES * a : SUBLANES * a + rows_b, :]
                y = term if y is None else y + term
        shifted = y[b : b + n_rows, :]
        out = shifted if out is None else out + shifted
    return out


CONV_STEP_ROWS = 128
CONV_LEAD_PIECES = 2
STEP_PROJ_ROWS = 256
N_GLU_STEPS = 2
N_COLS = N_IN // COL_W


def _glu_first(j):
    return jnp.where(j < N_GLU_STEPS, j + COL_GLU_A, jnp.where(j < COL_GLU_A + N_GLU_STEPS, j - N_GLU_STEPS, j))


def _inproj_conv_kernel(x_ref, g_ref, w_ref, cw_ref, cb_ref, lg_ref, lb_ref, z_ref, bin_ref, hist_ref,
                        h_ref, ext_ref, *, tiles_per_seq):
    i = pl.program_id(0)
    j = pl.program_id(1)
    t = x_ref.shape[0]

    def proj():
        return jnp.dot(h_ref[...], w_ref[...], preferred_element_type=F32)

    @pl.when(j == 0)
    def _():
        @pl.when(i % tiles_per_seq == 0)
        def _():
            ext_ref[0:HIST_PAD, :] = jnp.zeros((HIST_PAD, D_B), F32)

        @pl.when(i % tiles_per_seq != 0)
        def _():
            ext_ref[0:HIST_PAD, :] = ext_ref[t : t + HIST_PAD, :]

        for rb in range(t // PROJ_ROWS):
            for r in range(rb * PROJ_ROWS, (rb + 1) * PROJ_ROWS, NORM_ROWS):
                h_ref[r : r + NORM_ROWS, :] = _rms_norm(x_ref[r : r + NORM_ROWS, :], g_ref[...]).astype(BF16)
            rows = slice(rb * PROJ_ROWS, (rb + 1) * PROJ_ROWS)
            ext_ref[HIST_PAD + rb * PROJ_ROWS : HIST_PAD + (rb + 1) * PROJ_ROWS, :] = jnp.dot(
                h_ref[rows, :], w_ref[...], preferred_element_type=F32)

    @pl.when(j == 1)
    def _():
        ext_ref[HIST_PAD : HIST_PAD + t, :] = ext_ref[HIST_PAD : HIST_PAD + t, :] * jax.nn.sigmoid(proj())
        hist_ref[0] = ext_ref[t + CONV_LEAD : t + HIST_PAD, :]

    @pl.when(j >= N_GLU_STEPS)
    def _():
        chunk = jnp.minimum(j - N_GLU_STEPS, t // CONV_STEP_ROWS - 1)
        r0 = pl.multiple_of(chunk * CONV_STEP_ROWS, CONV_STEP_ROWS)
        n_row_blocks = CONV_STEP_ROWS // CONV_ROWS
        n_lane_blocks = D_B // PROJ_LANES
        pieces =[(rc, lb) for rc in range(n_row_blocks) for lb in range(n_lane_blocks)]
        vec_done, proj_done = [], []
        parts = []

        proj_pieces = [(rb, lb) for rb in range(t // STEP_PROJ_ROWS) for lb in range(n_lane_blocks)]
        per_proj = len(pieces) // len(proj_pieces)

        def conv_piece(p):
            rc, lb = pieces[p]
            wins = []
            beside = (p - CONV_LEAD_PIECES) // per_proj
            for cl in range(lb * PROJ_LANES, (lb + 1) * PROJ_LANES, CONV_LANES):
                clanes = slice(cl, cl + CONV_LANES)
                win = ext_ref[pl.ds(r0 + rc * CONV_ROWS, CONV_ROWS + HIST_PAD), clanes]
                if beside >= 1:
                    win = _not_before(win, proj_done[beside - 1])
                wins.append(_conv_window(win, cw_ref, clanes, CONV_ROWS))
            parts.append(jnp.concatenate(wins, axis=-1))
            vec_done.append(parts[-1])
            if lb == n_lane_blocks - 1:
                dc = jnp.concatenate(parts, axis=-1) + cb_ref[...]
                parts.clear()
                y = jax.nn.silu(_layer_norm(dc, lg_ref[...], lb_ref[...]))
                bin_ref[pl.ds(r0 + rc * CONV_ROWS, CONV_ROWS), :] = y.astype(BF16)
                vec_done.append(y)

        n_conv = 0
        for q, (rb, lb) in enumerate(proj_pieces):
            while n_conv < min((q + 1) * per_proj + CONV_LEAD_PIECES, len(pieces)):
                conv_piece(n_conv)
                n_conv += 1
            rows = slice(rb * STEP_PROJ_ROWS, (rb + 1) * STEP_PROJ_ROWS)
            lanes = slice(lb * PROJ_LANES, (lb + 1) * PROJ_LANES)
            zp = jnp.dot(h_ref[rows, :], w_ref[:, lanes], preferred_element_type=F32)
            if len(vec_done) >= 2:
                zp = _not_before(zp, vec_done[-2])
            z_ref[rows, lanes] = zp
            proj_done.append(zp)


def _inproj_conv(x, seq, g, w16, conv_w, conv_b, ln_g, ln_b, *, tm):
    m, k = x.shape
    tiles_per_seq = seq // tm
    n_conv_steps = N_COLS - N_GLU_STEPS
    assert tm // CONV_STEP_ROWS <= n_conv_steps, "not enough grid steps to convolve the whole tile"
    const = lambda i, j: (0, 0)
    return pl.pallas_call(
        functools.partial(_inproj_conv_kernel, tiles_per_seq=tiles_per_seq),
        out_shape=[
            jax.ShapeDtypeStruct((n_conv_steps, m, COL_W), F32),
            jax.ShapeDtypeStruct((m, D_B), BF16),
            jax.ShapeDtypeStruct((m // seq, HIST, D_B), F32),
        ],
        grid=(m // tm, N_COLS),
        in_specs=[
            pl.BlockSpec((tm, k), lambda i, j: (i, 0)),
            pl.BlockSpec((1, k), const),
            pl.BlockSpec((None, k, COL_W), lambda i, j: (_glu_first(j), 0, 0)),
            pl.BlockSpec((CONV_WIDTH, D_B), const),
            pl.BlockSpec((1, D_B), const),
            pl.BlockSpec((1, D_B), const),
            pl.BlockSpec((1, D_B), const),
        ],
        out_specs=[
            pl.BlockSpec((None, tm, COL_W), lambda i, j: (jnp.maximum(j - N_GLU_STEPS, 0), i, 0)),
            pl.BlockSpec((tm, D_B), lambda i, j: (i, 0)),
            pl.BlockSpec((1, HIST, D_B), lambda i, j: (i // tiles_per_seq, 0, 0)),
        ],
        scratch_shapes=[pltpu.VMEM((tm, k), BF16), pltpu.VMEM((tm + HIST_PAD, D_B), F32)],
        compiler_params=_cparams("arbitrary", "arbitrary", ),
        name="inproj_conv",
    )(x, g.reshape(1, k), w16, conv_w, conv_b.reshape(1, D_B), ln_g.reshape(1, D_B), ln_b.reshape(1, D_B))


def _conv_sample_kernel(*refs, seq, n_round):
    ga_ref, gb_ref, hist_ref, w_ref, cb_ref, g_ref, b_ref = refs[:7]
    f32_refs = refs[7 : 7 + n_round]
    o_ref, nh_ref = refs[7 + n_round : 9 + n_round]
    bf16_refs = refs[9 + n_round : 9 + 2 * n_round]
    cs_ref, dc_ref, os_ref = refs[9 + 2 * n_round :]
    for src, dst in zip(f32_refs, bf16_refs):
        dst[...] = src[...].astype(BF16)
    n_seq = hist_ref.shape[1]
    n_slabs = D_B // LANES
    c = ga_ref[...] * jax.nn.sigmoid(gb_ref[...])
    for l in range(n_slabs):
        cs_ref[l] = c[:, l * LANES : (l + 1) * LANES]
    new = [jnp.concatenate([cs_ref[l, pl.ds(t, n_seq, stride=seq), :] for l in range(n_slabs)], axis=-1)
           for t in range(seq)]
    nh_ref[0 : HIST - seq] = hist_ref[seq:HIST]
    for t in range(seq):
        nh_ref[HIST - seq + t] = new[t]
    for lb in range(D_B // CONV_LANES):
        lanes = slice(lb * CONV_LANES, (lb + 1) * CONV_LANES)
        ext = [hist_ref[j, :, lanes] for j in range(HIST)] + [new[t][:, lanes] for t in range(seq)]
        for t in range(seq):
            acc = w_ref[0:1, lanes] * ext[t]
            for k in range(1, CONV_WIDTH):
                acc = acc + w_ref[k : k + 1, lanes] * ext[t + k]
            dc_ref[t, :, lanes] = acc
    for t in range(seq):
        y = jax.nn.silu(_layer_norm(dc_ref[t] + cb_ref[...], g_ref[...], b_ref[...]))
        for l in range(n_slabs):
            os_ref[l, pl.ds(t, n_seq, stride=seq), :] = y[:, l * LANES : (l + 1) * LANES]
    for l in range(n_slabs):
        o_ref[:, l * LANES : (l + 1) * LANES] = os_ref[l].astype(BF16)


def _conv_sample(z, hist, seq, conv_w, conv_b, ln_g, ln_b, *, n_seq, also_round=()):
    batch = hist.shape[1]
    steps = batch // n_seq
    slab_specs = [pl.BlockSpec((w.shape[0] // steps, w.shape[1]), lambda i: (i, 0)) for w in also_round]
    slab_scratch = pltpu.VMEM((D_B // LANES, n_seq * seq, LANES), F32)
    return pl.pallas_call(
        functools.partial(_conv_sample_kernel, seq=seq, n_round=len(also_round)),
        out_shape=[
            jax.ShapeDtypeStruct((batch * seq, D_B), BF16),
            jax.ShapeDtypeStruct((HIST, batch, D_B), F32),
        ] + [jax.ShapeDtypeStruct(w.shape, BF16) for w in also_round],
        grid=(steps,),
        in_specs=[
            pl.BlockSpec((n_seq * seq, COL_W), lambda i: (i, COL_GLU_A)),
            pl.BlockSpec((n_seq * seq, COL_W), lambda i: (i, COL_GLU_B)),
            pl.BlockSpec((HIST, n_seq, D_B), lambda i: (0, i, 0)),
            pl.BlockSpec((CONV_WIDTH, D_B), lambda i: (0, 0)),
            pl.BlockSpec((1, D_B), lambda i: (0, 0)),
            pl.BlockSpec((1, D_B), lambda i: (0, 0)),
            pl.BlockSpec((1, D_B), lambda i: (0, 0)),
        ] + slab_specs,
        out_specs=[
            pl.BlockSpec((n_seq * seq, D_B), lambda i: (i, 0)),
            pl.BlockSpec((HIST, n_seq, D_B), lambda i: (0, i, 0)),
        ] + slab_specs,
        scratch_shapes=[slab_scratch, pltpu.VMEM((seq, n_seq, D_B), F32), slab_scratch],
        compiler_params=_cparams("arbitrary"),
        name="conv_sample",
    )(z, z, hist, conv_w, conv_b.reshape(1, D_B), ln_g.reshape(1, D_B), ln_b.reshape(1, D_B), *also_round)


def _head_cols(h):
    return slice(h * MEM_HEAD_DIM, (h + 1) * MEM_HEAD_DIM)


def _split_lane_tiles(kv):
    b = kv.shape[0]
    kv = kv.reshape(b, N_MEM, N_MEM_HEADS, HEAD_LANE_TILES, LANES)
    return kv.transpose(0, 1, 3, 2, 4).reshape(b, N_MEM * HEAD_ROW_PITCH, LANES)


def _merge_lane_tiles(view, batch):
    view = view.reshape(batch, N_MEM, HEAD_LANE_TILES, N_MEM_HEADS, LANES)
    return view.transpose(0, 1, 3, 2, 4).reshape(batch, N_MEM, N_MEM_HEADS, MEM_HEAD_DIM)


def _head_of(kv_ref, s, h):
    if kv_ref.shape[-1] == LANES:
        tiles = [kv_ref[s, pl.ds(j * N_MEM_HEADS + h, N_MEM, stride=HEAD_ROW_PITCH), :]
                 for j in range(HEAD_LANE_TILES)]
        return jnp.concatenate(tiles, axis=-1).astype(BF16)
    return kv_ref[s, :, _head_cols(h)].astype(BF16)


def _attn_kernel(q_ref, k_ref, v_ref, o_ref, *, rows):
    scale = 1.0 / math.sqrt(MEM_HEAD_DIM)
    pairs = [(s, h) for s in range(k_ref.shape[0]) for h in range(N_MEM_HEADS)]
    scores = []
    for s, h in pairs:
        q = (q_ref[s * rows : (s + 1) * rows, _head_cols(h)] * scale).astype(BF16)
        scores.append(lax.dot_general(q, _head_of(k_ref, s, h), (((1,), (1,)), ((), ())),
                                      preferred_element_type=F32))
    sc = jnp.concatenate(scores, axis=0)
    p = jnp.exp(sc - jnp.max(sc, axis=-1, keepdims=True))
    p = p / jnp.sum(p, axis=-1, keepdims=True)
    for n, (s, h) in enumerate(pairs):
        ph = p[n * rows : (n + 1) * rows, :].astype(BF16)
        o = jnp.dot(ph, _head_of(v_ref, s, h), preferred_element_type=F32)
        o_ref[s * rows : (s + 1) * rows, _head_cols(h)] = o.astype(BF16)


def _merge_kernel(a_ref, b_ref, c_ref, *refs):
    gate_refs = refs[: 2 * N_BRANCH]
    bg_ref, x_ref, wa_ref, wb_ref, wc_ref, wo_ref, mg_ref, x1_ref, h2_ref = refs[2 * N_BRANCH :]
    halves = []
    for half in range(2):
        cols = slice(half * COL_W, (half + 1) * COL_W)
        merged = None
        for br, (in_ref, w_ref) in enumerate(((a_ref, wa_ref), (b_ref, wb_ref), (c_ref, wc_ref))):
            y = jnp.dot(in_ref[...], w_ref[:, cols], preferred_element_type=F32)
            gcols = slice((2 * br + half) * COL_W, (2 * br + half + 1) * COL_W)
            term = jax.nn.sigmoid(gate_refs[2 * br + half][...] + bg_ref[:, gcols]) * y
            merged = term if merged is None else merged + term
        halves.append(merged.astype(BF16))
    merged16 = jnp.concatenate(halves, axis=-1)
    x1 = x_ref[...] + jnp.dot(merged16, wo_ref[...], preferred_element_type=F32)
    x1_ref[...] = x1
    h2_ref[...] = _rms_norm(x1, mg_ref[...]).astype(BF16)


def _merge(a_in, b_in, c_in, z, cols, x, b_gate, w_a, w_b, w_c, w_o, mlp_g, *, tm):
    m = x.shape[0]
    const = lambda i: (0, 0)
    resident = functools.partial(pl.BlockSpec, index_map=const, pipeline_mode=pl.Buffered(1))
    gate_specs = [_col_block_spec(z, tm, cols.gates + n) for n in range(2 * N_BRANCH)]
    return pl.pallas_call(
        _merge_kernel,
        out_shape=[jax.ShapeDtypeStruct((m, D_MODEL), F32), jax.ShapeDtypeStruct((m, D_MODEL), BF16)],
        grid=(m // tm,),
        in_specs=[pl.BlockSpec((tm, D_A), lambda i: (i, 0))] * 3
        + gate_specs
        + [
            pl.BlockSpec((1, N_BRANCH * D_MODEL), const),
            pl.BlockSpec((tm, D_MODEL), lambda i: (i, 0)),
            resident((D_A, D_MODEL)),
            resident((D_B, D_MODEL)),
            resident((D_C, D_MODEL)),
            resident((D_MODEL, D_MODEL)),
            pl.BlockSpec((1, D_MODEL), const),
        ],
        out_specs=[pl.BlockSpec((tm, D_MODEL), lambda i: (i, 0))] * 2,
        compiler_params=_cparams("arbitrary"),
        name="merge",
    )(a_in, b_in, c_in, *([z] * (2 * N_BRANCH)), b_gate.reshape(1, -1), x, w_a, w_b, w_c, w_o,
      mlp_g.reshape(1, D_MODEL))


def _mlp_kernel(h2_ref, x1_ref, wu_ref, wd_ref, fg_ref, y_ref, acc_ref):
    f = pl.program_id(1)

    def ffn():
        t = jnp.square(jnp.maximum(jnp.dot(h2_ref[...], wu_ref[...], preferred_element_type=F32), 0.0))
        return jnp.dot(t.astype(BF16), wd_ref[...], preferred_element_type=F32)

    @pl.when(f == 0)
    def _():
        acc_ref[...] = x1_ref[...] + ffn()

    @pl.when(f > 0)
    def _():
        acc_ref[...] += ffn()

    @pl.when(f == pl.num_programs(1) - 1)
    def _():
        y_ref[...] = _rms_norm(acc_ref[...], fg_ref[...])


def _mlp(h2, x1, w_up, w_down, final_g, *, tm, tf):
    m = h2.shape[0]
    return pl.pallas_call(
        _mlp_kernel,
        out_shape=jax.ShapeDtypeStruct((m, D_MODEL), F32),
        grid=(m // tm, D_FF // tf),
        in_specs=[
            pl.BlockSpec((tm, D_MODEL), lambda i, f: (i, 0)),
            pl.BlockSpec((tm, D_MODEL), lambda i, f: (i, 0)),
            pl.BlockSpec((D_MODEL, tf), lambda i, f: (0, f)),
            pl.BlockSpec((tf, D_MODEL), lambda i, f: (f, 0)),
            pl.BlockSpec((1, D_MODEL), lambda i, f: (0, 0)),
        ],
        out_specs=pl.BlockSpec((tm, D_MODEL), lambda i, f: (i, 0)),
        scratch_shapes=[pltpu.VMEM((tm, D_MODEL), F32)],
        compiler_params=_cparams("arbitrary", "arbitrary"),
        name="mlp",
    )(h2, x1, w_up, w_down, final_g.reshape(1, D_MODEL))


def _mlp_first_kernel(h2_ref, wu_ref, wd_ref, acc_ref, wu16_ref, wd16_ref):
    @pl.when(pl.program_id(0) == 0)
    def _():
        acc_ref[...] = jnp.zeros(acc_ref.shape, F32)

    h2 = h2_ref[...]
    t_pieces = []
    for c in range(0, wu_ref.shape[1], PROJ_LANES):
        wu = wu_ref[:, c : c + PROJ_LANES].astype(BF16)
        wu16_ref[:, c : c + PROJ_LANES] = wu
        t = jnp.square(jnp.maximum(jnp.dot(h2, wu, preferred_element_type=F32), 0.0))
        t_pieces.append(t.astype(BF16))
    t16 = jnp.concatenate(t_pieces, axis=-1)
    for c in range(0, wd_ref.shape[1], PROJ_LANES):
        wd = wd_ref[:, c : c + PROJ_LANES].astype(BF16)
        wd16_ref[:, c : c + PROJ_LANES] = wd
        acc_ref[:, c : c + PROJ_LANES] += jnp.dot(t16, wd, preferred_element_type=F32)


def _mlp_first(h2, w_up, w_down, *, tf):
    m = h2.shape[0]
    resident = pl.BlockSpec((m, D_MODEL), lambda f: (0, 0))
    return pl.pallas_call(
        _mlp_first_kernel,
        out_shape=[
            jax.ShapeDtypeStruct((m, D_MODEL), F32),
            jax.ShapeDtypeStruct((D_MODEL, D_FF), BF16),
            jax.ShapeDtypeStruct((D_FF, D_MODEL), BF16),
        ],
        grid=(D_FF // tf,),
        in_specs=[
            resident,
            pl.BlockSpec((D_MODEL, tf), lambda f: (0, f)),
            pl.BlockSpec((tf, D_MODEL), lambda f: (f, 0)),
        ],
        out_specs=[
            resident,
            pl.BlockSpec((D_MODEL, tf), lambda f: (0, f)),
            pl.BlockSpec((tf, D_MODEL), lambda f: (f, 0)),
        ],
        compiler_params=_cparams("arbitrary"),
        name="mlp_first",
    )(h2, w_up, w_down)


def _residual_norm_kernel(x_ref, d_ref, g_ref, y_ref):
    y_ref[...] = _rms_norm(x_ref[...] + d_ref[...], g_ref[...])


def _residual_norm(x, d, g, *, tm):
    m = x.shape[0]
    row_spec = pl.BlockSpec((tm, D_MODEL), lambda i: (i, 0))
    return pl.pallas_call(
        _residual_norm_kernel,
        out_shape=jax.ShapeDtypeStruct((m, D_MODEL), F32),
        grid=(m // tm,),
        in_specs=[row_spec, row_spec, pl.BlockSpec((1, D_MODEL), lambda i: (0, 0))],
        out_specs=row_spec,
        compiler_params=_cparams("arbitrary"),
        name="residual_norm",
    )(x, d, g.reshape(1, D_MODEL))


SAMPLE_ATTN_SEQS = 4


def _branches_kernel(u_ref, v_ref, g_ref, b_ref, sw_ref, sb_ref, qp_ref, kp_ref, vp_ref, qs_ref, ks_ref, vs_ref,
                     a_ref, op_ref, os_ref, *, chunk, rows_p, seq_s):
    _sgu_kernel(u_ref, v_ref, g_ref, b_ref, sw_ref, sb_ref, a_ref, chunk=chunk)
    _attn_kernel(qp_ref, kp_ref, vp_ref, op_ref, rows=rows_p)
    _attn_kernel(qs_ref, ks_ref, vs_ref, os_ref, rows=seq_s)


def _branches(z_p, cols_p, seq_p, kv_p, z_s, cols_s, seq_s, k_s, v_s, ln_g, ln_b, sgu_w, sgu_b):
    m_p, m_s = z_p.shape[-2], z_s.shape[-2]
    steps = m_s // (seq_s * SAMPLE_ATTN_SEQS)
    rows_p = m_p // steps
    assert rows_p % CHUNK == 0 and seq_p % rows_p == 0
    steps_per_seq = seq_p // rows_p
    rows_s = seq_s * SAMPLE_ATTN_SEQS
    const2 = lambda i: (0, 0)
    kv_p_specs = [pl.BlockSpec((1, N_MEM, D_C), functools.partial(lambda i, col: (i // steps_per_seq, 0, col), col=col))
                  for col in (0, 1)]
    kv_s_spec = pl.BlockSpec((SAMPLE_ATTN_SEQS, N_MEM * HEAD_ROW_PITCH, LANES), lambda i: (i, 0, 0))
    return pl.pallas_call(
        functools.partial(_branches_kernel, chunk=min(seq_p, CHUNK), rows_p=rows_p, seq_s=seq_s),
        out_shape=[
            jax.ShapeDtypeStruct((m_p, D_A), BF16),
            jax.ShapeDtypeStruct((m_p, D_C), BF16),
            jax.ShapeDtypeStruct((m_s, D_C), BF16),
        ],
        grid=(steps,),
        in_specs=[
            _col_block_spec(z_p, rows_p, cols_p.u),
            _col_block_spec(z_p, rows_p, cols_p.v),
            pl.BlockSpec((1, D_A), const2),
            pl.BlockSpec((1, D_A), const2),
            pl.BlockSpec((SGU_GROUPS, CHUNK, CHUNK), lambda i: (0, 0, 0)),
            pl.BlockSpec((SGU_GROUPS, CHUNK), const2),
            _col_block_spec(z_p, rows_p, cols_p.q),
            *kv_p_specs,
            _col_block_spec(z_s, rows_s, cols_s.q),
            kv_s_spec,
            kv_s_spec,
        ],
        out_specs=[
            pl.BlockSpec((rows_p, D_A), lambda i: (i, 0)),
            pl.BlockSpec((rows_p, D_C), lambda i: (i, 0)),
            pl.BlockSpec((rows_s, D_C), lambda i: (i, 0)),
        ],
        compiler_params=_cparams("arbitrary"),
        name="branches",
    )(z_p, z_p, ln_g.reshape(1, D_A), ln_b.reshape(1, D_A), sgu_w, sgu_b, z_p, kv_p, kv_p, z_s, k_s, v_s)


def kernel(x_prompt, x_sample, mem_prompt, cache_mem_k, cache_mem_v, state_conv, attn_norm_g, w_in, b_gate,
           sgu_ln_g, sgu_ln_b, sgu_w, sgu_b, w_a_out, conv_w, conv_b, conv_ln_g, conv_ln_b, w_b_out, mem_norm_g,
           w_mem_kv, w_c_out, w_o, mlp_norm_g, w_up, w_down, final_norm_g):
    assert attn_norm_g.shape[0] == 1, "single-layer trunk"
    bp, sp, _ = x_prompt.shape
    bs, ss, _ = x_sample.shape
    p = dict(
        attn_norm_g=attn_norm_g[0], w_in=w_in[0], b_gate=b_gate[0], sgu_ln_g=sgu_ln_g[0],
        sgu_ln_b=sgu_ln_b[0], sgu_w=sgu_w[0], sgu_b=sgu_b[0], w_a_out=w_a_out[0],
        conv_w=conv_w[0], conv_b=conv_b[0], conv_ln_g=conv_ln_g[0], conv_ln_b=conv_ln_b[0],
        w_b_out=w_b_out[0], w_c_out=w_c_out[0], w_o=w_o[0],
        mlp_norm_g=mlp_norm_g[0], w_up=w_up[0], w_down=w_down[0],
        final_norm_g=final_norm_g,
    )
    kv, *kv_split = _norm_matmul(mem_prompt.reshape(bp * N_MEM, D_MODEL), mem_norm_g[0], w_mem_kv[0],
                                 tm=1024, tn=512, emit_head_split=True)
    kv = kv.reshape(bp, N_MEM, 2 * D_C)
    xs = x_sample.reshape(bs * ss, D_MODEL)
    xp = x_prompt.reshape(bp * sp, D_MODEL)
    conv_args = (p["conv_w"], p["conv_b"], p["conv_ln_g"], p["conv_ln_b"])
    sgu_args = (p["sgu_ln_g"], p["sgu_ln_b"], p["sgu_w"], p["sgu_b"])

    z_s, w_in16 = _norm_matmul(xs, p["attn_norm_g"], p["w_in"], tm=1024, tn=1024, emit_w16=True)
    out_names = ("w_a_out", "w_b_out", "w_c_out", "w_o")
    b_s, hist_s, *rounded = _conv_sample(z_s, jnp.transpose(state_conv[0], (1, 0, 2)), ss, *conv_args, n_seq=16,
                                         also_round=[p[name] for name in out_names])
    p = {**p, **dict(zip(out_names, rounded))}
    a_s, vn_s = _sgu(z_s, Z_COLS_ALL, *sgu_args, chunk=min(ss, CHUNK), rows=1024, want_vn=True)

    z_p, b_p, hist_p = _inproj_conv(xp, sp, p["attn_norm_g"], w_in16, *conv_args, tm=1024)
    a_p, c_p, c_s = _branches(z_p, Z_COLS_NO_GLU, sp, kv, z_s, Z_COLS_ALL, ss, _split_lane_tiles(cache_mem_k[0]),
                              _split_lane_tiles(cache_mem_v[0]), *sgu_args)

    merge_w = (p["b_gate"], p["w_a_out"], p["w_b_out"], p["w_c_out"], p["w_o"], p["mlp_norm_g"])
    x1_s, h2_s = _merge(a_s, b_s, c_s, z_s, Z_COLS_ALL, xs, *merge_w, tm=256)
    d_s, w_up16, w_down16 = _mlp_first(h2_s, p["w_up"], p["w_down"], tf=512)
    y_s = _residual_norm(x1_s, d_s, p["final_norm_g"], tm=512)
    x1_p, h2_p = _merge(a_p, b_p, c_p, z_p, Z_COLS_NO_GLU, xp, *merge_w, tm=256)
    y_p = _mlp(h2_p, x1_p, w_up16, w_down16, p["final_norm_g"], tm=512, tf=1024)
    return (
        y_p.reshape(bp, sp, D_MODEL),
        y_s.reshape(bs, ss, D_MODEL),
        _merge_lane_tiles(kv_split[0], bp)[None],
        _merge_lane_tiles(kv_split[1], bp)[None],
        hist_p[None],
        jnp.transpose(hist_s, (1, 0, 2))[None],
        vn_s.reshape(1, bs, ss, D_A),
    )
```

```python
import functools
import math
from typing import NamedTuple

import jax
import jax.numpy as jnp
from jax import lax
from jax.experimental import pallas as pl
from jax.experimental.pallas import tpu as pltpu

F32 = jnp.float32
BF16 = jnp.bfloat16

D_MODEL = 2048
D_A = 1024
D_B = 1024
D_C = 1024
CHUNK = 128
SGU_GROUPS = 8
SGU_HEAD = D_A // SGU_GROUPS
CONV_WIDTH = 31
HIST = CONV_WIDTH - 1
N_MEM = 256
N_MEM_HEADS = 4
MEM_HEAD_DIM = D_C // N_MEM_HEADS
N_BRANCH = 3
D_FF = 4 * D_MODEL
N_IN = 2 * D_A + 2 * D_B + D_C + N_BRANCH * D_MODEL
EPS = 1e-6

COL_U, COL_V, COL_GLU_A, COL_GLU_B, COL_Q, COL_GATES = 0, 1, 2, 3, 4, 5
COL_W = 1024


class ZCols(NamedTuple):
    u: int
    v: int
    q: int
    gates: int


def _col_block_spec(z, rows, col):
    if z.ndim == 3:
        return pl.BlockSpec((None, rows, COL_W), lambda i: (col, i, 0))
    return pl.BlockSpec((rows, COL_W), lambda i: (i, col))


Z_COLS_ALL = ZCols(COL_U, COL_V, COL_Q, COL_GATES)
Z_COLS_NO_GLU = ZCols(COL_U, COL_V, COL_Q - 2, COL_GATES - 2)

V7X_VMEM_BYTES = 64 * 1024 * 1024
VMEM_LIMIT_BYTES = V7X_VMEM_BYTES - 8 * 1024 * 1024
SUBLANES = 8
LANES = 128
HEAD_LANE_TILES = MEM_HEAD_DIM // LANES
HEAD_ROW_PITCH = N_MEM_HEADS * HEAD_LANE_TILES
V7X_MXU_COLS = 256
PROJ_LANES = V7X_MXU_COLS
PROJ_ROWS = 256
HIST_PAD = 32


def _cparams(*semantics, flags=None):
    return pltpu.CompilerParams(dimension_semantics=semantics, vmem_limit_bytes=VMEM_LIMIT_BYTES, flags=flags)


def _not_before(value, anchor):
    tile = anchor[-SUBLANES:, -LANES:]
    zero = pltpu.bitcast((pltpu.bitcast(tile, jnp.uint32) >> 16) >> 16, F32)
    reps = (value.shape[0] // SUBLANES, value.shape[1] // LANES)
    return value + jnp.tile(zero, reps)


def _rms_norm(x, g):
    ms = jnp.mean(x * x, axis=-1, keepdims=True)
    return x * lax.rsqrt(ms + EPS) * g


def _layer_norm(x, g, b):
    mu = jnp.mean(x, axis=-1, keepdims=True)
    xc = x - mu
    var = jnp.mean(xc * xc, axis=-1, keepdims=True)
    return xc * lax.rsqrt(var + EPS) * g + b


NORM_ROWS = 128


def _norm_matmul_kernel(x_ref, g_ref, w_ref, o_ref, *rest, emit_w16, emit_head_split):
    *extra, h_ref = rest
    w16_ref = extra.pop(0) if emit_w16 else None
    split_ref = extra.pop(0) if emit_head_split else None

    @pl.when(pl.program_id(1) == 0)
    def _():
        def body(r, carry):
            rows = pl.ds(pl.multiple_of(r * NORM_ROWS, NORM_ROWS), NORM_ROWS)
            h_ref[rows, :] = _rms_norm(x_ref[rows, :], g_ref[...]).astype(BF16)
            return carry

        lax.fori_loop(0, x_ref.shape[0] // NORM_ROWS, body, 0)

    tn = w_ref.shape[1]
    for c in range(0, tn, PROJ_LANES):
        w16 = w_ref[:, c : c + PROJ_LANES].astype(BF16)
        if emit_w16:
            w16_ref[:, c : c + PROJ_LANES] = w16
        piece = jnp.dot(h_ref[...], w16, preferred_element_type=F32)
        o_ref[:, c : c + PROJ_LANES] = piece
        if emit_head_split:
            head = ((pl.program_id(1) * tn + c) % D_C) // MEM_HEAD_DIM
            for lt in range(HEAD_LANE_TILES):
                split_ref[pl.ds(lt * N_MEM_HEADS + head, piece.shape[0], stride=HEAD_ROW_PITCH), :] = (
                    piece[:, lt * LANES : (lt + 1) * LANES])


def _norm_matmul(x, g, w, *, tm, tn, emit_w16=False, emit_head_split=False):
    m, k = x.shape
    n = w.shape[1]
    out_shape = [jax.ShapeDtypeStruct((m, n), F32)]
    out_specs = [pl.BlockSpec((tm, tn), lambda i, j: (i, j))]
    if emit_w16:
        assert m == tm, "every weight block must be visited exactly once"
        per_col = COL_W // tn
        out_shape.append(jax.ShapeDtypeStruct((n // COL_W, k, COL_W), BF16))
        out_specs.append(pl.BlockSpec((None, k, tn), lambda i, j: (j // per_col, 0, j % per_col)))
    if emit_head_split:
        assert m == tm and PROJ_LANES == MEM_HEAD_DIM and n == 2 * D_C
        out_shape.append(jax.ShapeDtypeStruct((n // D_C, m * HEAD_ROW_PITCH, LANES), F32))
        out_specs.append(pl.BlockSpec((None, m * HEAD_ROW_PITCH, LANES), lambda i, j: (j // (D_C // tn), 0, 0)))
    outs = pl.pallas_call(
        functools.partial(_norm_matmul_kernel, emit_w16=emit_w16, emit_head_split=emit_head_split),
        out_shape=out_shape,
        grid=(m // tm, n // tn),
        in_specs=[
            pl.BlockSpec((tm, k), lambda i, j: (i, 0), pipeline_mode=pl.Buffered(1 if m == tm else 2)),
            pl.BlockSpec((1, k), lambda i, j: (0, 0)),
            pl.BlockSpec((k, tn), lambda i, j: (0, j)),
        ],
        out_specs=out_specs,
        scratch_shapes=[pltpu.VMEM((tm, k), BF16)],
        compiler_params=_cparams("arbitrary", "arbitrary"),
        name="norm_matmul",
    )(x, g.reshape(1, k), w)
    return outs if len(outs) > 1 else outs[0]


def _lane_periodic(x, period):
    lane = lax.broadcasted_iota(jnp.int32, x.shape, 1)
    y = jnp.where(lane < period, x, 0.0)
    while period < LANES:
        y = y + pltpu.roll(y, period, axis=1)
        period *= 2
    return y


def _sgu_kernel(u_ref, v_ref, g_ref, b_ref, w_ref, sb_ref, a_ref, *vn_refs, chunk):
    r = lax.broadcasted_iota(jnp.int32, (CHUNK, CHUNK), 0)
    c = lax.broadcasted_iota(jnp.int32, (CHUNK, CHUNK), 1)
    seg_bits = chunk.bit_length() - 1
    same_segment = (r >> seg_bits) == (c >> seg_bits)
    mask = jnp.logical_and(same_segment, (r & (chunk - 1)) >= (c & (chunk - 1)))
    sb = sb_ref[...] if chunk == CHUNK else _lane_periodic(sb_ref[...], chunk)
    w_s, bias = [], []
    for grp in range(SGU_GROUPS):
        if chunk == CHUNK:
            w = w_ref[grp]
        else:
            w = _lane_periodic(jnp.tile(w_ref[grp, 0:chunk, :], (CHUNK // chunk, 1)), chunk)
        w_s.append(jnp.where(mask, w, 0.0).astype(BF16))
        col = jnp.sum(jnp.where(r == c, sb[grp : grp + 1, :], 0.0), axis=1, keepdims=True)
        bias.append(jnp.broadcast_to(col, (CHUNK, SGU_HEAD)))
    for blk in range(u_ref.shape[0] // CHUNK):
        rows = slice(blk * CHUNK, (blk + 1) * CHUNK)
        vn = _layer_norm(jax.nn.gelu(v_ref[rows, :]), g_ref[...], b_ref[...])
        if vn_refs:
            vn_refs[0][rows, :] = vn
        vn16 = vn.astype(BF16)
        for grp in range(SGU_GROUPS):
            cols = slice(grp * SGU_HEAD, (grp + 1) * SGU_HEAD)
            mixed = jnp.dot(w_s[grp], vn16[:, cols], preferred_element_type=F32) + bias[grp]
            a_ref[rows, cols] = (jax.nn.gelu(u_ref[rows, cols]) * mixed).astype(BF16)


def _sgu(z, cols, ln_g, ln_b, sgu_w, sgu_b, *, chunk, rows, want_vn):
    m = z.shape[-2]
    out_shape = [jax.ShapeDtypeStruct((m, D_A), BF16)]
    out_specs = [pl.BlockSpec((rows, D_A), lambda i: (i, 0))]
    if want_vn:
        out_shape.append(jax.ShapeDtypeStruct((m, D_A), F32))
        out_specs.append(pl.BlockSpec((rows, D_A), lambda i: (i, 0)))
    return pl.pallas_call(
        functools.partial(_sgu_kernel, chunk=chunk),
        out_shape=out_shape,
        grid=(m // rows,),
        in_specs=[
            _col_block_spec(z, rows, cols.u),
            _col_block_spec(z, rows, cols.v),
            pl.BlockSpec((1, D_A), lambda i: (0, 0)),
            pl.BlockSpec((1, D_A), lambda i: (0, 0)),
            pl.BlockSpec((SGU_GROUPS, CHUNK, CHUNK), lambda i: (0, 0, 0)),
            pl.BlockSpec((SGU_GROUPS, CHUNK), lambda i: (0, 0)),
        ],
        out_specs=out_specs,
        compiler_params=_cparams("arbitrary"),
        name="sgu",
    )(z, z, ln_g.reshape(1, D_A), ln_b.reshape(1, D_A), sgu_w, sgu_b)


CONV_ROWS = 32
CONV_LANES = 128
CONV_LEAD = HIST_PAD - HIST


def _conv_window(win, w_ref, lanes, n_rows):
    out = None
    for b in range(SUBLANES):
        rows_b = n_rows if b == 0 else n_rows + SUBLANES
        y = None
        for a in range(HIST_PAD // SUBLANES + 1):
            k = SUBLANES * a + b - CONV_LEAD
            if 0 <= k < CONV_WIDTH:
                term = w_ref[k : k + 1, lanes] * win[SUBLANES * a : SUBLANES * a + rows_b, :]
                y = term if y is None else y + term
        shifted = y[b : b + n_rows, :]
        out = shifted if out is None else out + shifted
    return out


CONV_STEP_ROWS = 128
CONV_LEAD_PIECES = 2
N_GLU_STEPS = 2
N_COLS = N_IN // COL_W


def _glu_first(j):
    return jnp.where(j < N_GLU_STEPS, j + COL_GLU_A, jnp.where(j < COL_GLU_A + N_GLU_STEPS, j - N_GLU_STEPS, j))


def _inproj_conv_kernel(x_ref, g_ref, w_ref, cw_ref, cb_ref, lg_ref, lb_ref, z_ref, bin_ref, hist_ref,
                        h_ref, ext_ref, *, tiles_per_seq):
    i = pl.program_id(0)
    j = pl.program_id(1)
    t = x_ref.shape[0]

    def proj():
        return jnp.dot(h_ref[...], w_ref[...], preferred_element_type=F32)

    @pl.when(j == 0)
    def _():
        @pl.when(i % tiles_per_seq == 0)
        def _():
            ext_ref[0:HIST_PAD, :] = jnp.zeros((HIST_PAD, D_B), F32)

        @pl.when(i % tiles_per_seq != 0)
        def _():
            ext_ref[0:HIST_PAD, :] = ext_ref[t : t + HIST_PAD, :]

        for rb in range(t // PROJ_ROWS):
            for r in range(rb * PROJ_ROWS, (rb + 1) * PROJ_ROWS, NORM_ROWS):
                h_ref[r : r + NORM_ROWS, :] = _rms_norm(x_ref[r : r + NORM_ROWS, :], g_ref[...]).astype(BF16)
            rows = slice(rb * PROJ_ROWS, (rb + 1) * PROJ_ROWS)
            ext_ref[HIST_PAD + rb * PROJ_ROWS : HIST_PAD + (rb + 1) * PROJ_ROWS, :] = jnp.dot(
                h_ref[rows, :], w_ref[...], preferred_element_type=F32)

    @pl.when(j == 1)
    def _():
        ext_ref[HIST_PAD : HIST_PAD + t, :] = ext_ref[HIST_PAD : HIST_PAD + t, :] * jax.nn.sigmoid(proj())
        hist_ref[0] = ext_ref[t + CONV_LEAD : t + HIST_PAD, :]

    @pl.when(j >= N_GLU_STEPS)
    def _():
        chunk = jnp.minimum(j - N_GLU_STEPS, t // CONV_STEP_ROWS - 1)
        r0 = pl.multiple_of(chunk * CONV_STEP_ROWS, CONV_STEP_ROWS)
        n_row_blocks = CONV_STEP_ROWS // CONV_ROWS
        n_lane_blocks = D_B // PROJ_LANES
        rows_per_block = t // n_row_blocks
        pieces = [(rc, lb) for rc in range(n_row_blocks) for lb in range(n_lane_blocks)]
        vec_done, proj_done = [], []
        parts = []

        def conv_piece(p):
            rc, lb = pieces[p]
            wins = []
            for cl in range(lb * PROJ_LANES, (lb + 1) * PROJ_LANES, CONV_LANES):
                clanes = slice(cl, cl + CONV_LANES)
                win = ext_ref[pl.ds(r0 + rc * CONV_ROWS, CONV_ROWS + HIST_PAD), clanes]
                if p > CONV_LEAD_PIECES:
                    win = _not_before(win, proj_done[p - CONV_LEAD_PIECES - 1])
                wins.append(_conv_window(win, cw_ref, clanes, CONV_ROWS))
            parts.append(jnp.concatenate(wins, axis=-1))
            vec_done.append(parts[-1])
            if lb == n_lane_blocks - 1:
                dc = jnp.concatenate(parts, axis=-1) + cb_ref[...]
                parts.clear()
                y = jax.nn.silu(_layer_norm(dc, lg_ref[...], lb_ref[...]))
                bin_ref[pl.ds(r0 + rc * CONV_ROWS, CONV_ROWS), :] = y.astype(BF16)
                vec_done.append(y)

        n_conv = 0
        for q, (rc, lb) in enumerate(pieces):
            while n_conv < min(q + CONV_LEAD_PIECES + 1, len(pieces)):
                conv_piece(n_conv)
                n_conv += 1
            rows = slice(rc * rows_per_block, (rc + 1) * rows_per_block)
            lanes = slice(lb * PROJ_LANES, (lb + 1) * PROJ_LANES)
            zp = jnp.dot(h_ref[rows, :], w_ref[:, lanes], preferred_element_type=F32)
            if len(vec_done) >= 2:
                zp = _not_before(zp, vec_done[-2])
            z_ref[rows, lanes] = zp
            proj_done.append(zp)


def _inproj_conv(x, seq, g, w16, conv_w, conv_b, ln_g, ln_b, *, tm):
    m, k = x.shape
    tiles_per_seq = seq // tm
    n_conv_steps = N_COLS - N_GLU_STEPS
    assert tm // CONV_STEP_ROWS <= n_conv_steps, "not enough grid steps to convolve the whole tile"
    const = lambda i, j: (0, 0)
    return pl.pallas_call(
        functools.partial(_inproj_conv_kernel, tiles_per_seq=tiles_per_seq),
        out_shape=[
            jax.ShapeDtypeStruct((n_conv_steps, m, COL_W), F32),
            jax.ShapeDtypeStruct((m, D_B), BF16),
            jax.ShapeDtypeStruct((m // seq, HIST, D_B), F32),
        ],
        grid=(m // tm, N_COLS),
        in_specs=[
            pl.BlockSpec((tm, k), lambda i, j: (i, 0)),
            pl.BlockSpec((1, k), const),
            pl.BlockSpec((None, k, COL_W), lambda i, j: (_glu_first(j), 0, 0)),
            pl.BlockSpec((CONV_WIDTH, D_B), const),
            pl.BlockSpec((1, D_B), const),
            pl.BlockSpec((1, D_B), const),
            pl.BlockSpec((1, D_B), const),
        ],
        out_specs=[
            pl.BlockSpec((None, tm, COL_W), lambda i, j: (jnp.maximum(j - N_GLU_STEPS, 0), i, 0)),
            pl.BlockSpec((tm, D_B), lambda i, j: (i, 0)),
            pl.BlockSpec((1, HIST, D_B), lambda i, j: (i // tiles_per_seq, 0, 0)),
        ],
        scratch_shapes=[pltpu.VMEM((tm, k), BF16), pltpu.VMEM((tm + HIST_PAD, D_B), F32)],
        compiler_params=_cparams("arbitrary", "arbitrary", ),
        name="inproj_conv",
    )(x, g.reshape(1, k), w16, conv_w, conv_b.reshape(1, D_B), ln_g.reshape(1, D_B), ln_b.reshape(1, D_B))


def _conv_sample_kernel(*refs, seq, n_round):
    ga_ref, gb_ref, hist_ref, w_ref, cb_ref, g_ref, b_ref = refs[:7]
    f32_refs = refs[7 : 7 + n_round]
    o_ref, nh_ref = refs[7 + n_round : 9 + n_round]
    bf16_refs = refs[9 + n_round : 9 + 2 * n_round]
    cs_ref, dc_ref, os_ref = refs[9 + 2 * n_round :]
    for src, dst in zip(f32_refs, bf16_refs):
        dst[...] = src[...].astype(BF16)
    n_seq = hist_ref.shape[1]
    n_slabs = D_B // LANES
    c = ga_ref[...] * jax.nn.sigmoid(gb_ref[...])
    for l in range(n_slabs):
        cs_ref[l] = c[:, l * LANES : (l + 1) * LANES]
    new = [jnp.concatenate([cs_ref[l, pl.ds(t, n_seq, stride=seq), :] for l in range(n_slabs)], axis=-1)
           for t in range(seq)]
    nh_ref[0 : HIST - seq] = hist_ref[seq:HIST]
    for t in range(seq):
        nh_ref[HIST - seq + t] = new[t]
    for lb in range(D_B // CONV_LANES):
        lanes = slice(lb * CONV_LANES, (lb + 1) * CONV_LANES)
        ext = [hist_ref[j, :, lanes] for j in range(HIST)] + [new[t][:, lanes] for t in range(seq)]
        for t in range(seq):
            acc = w_ref[0:1, lanes] * ext[t]
            for k in range(1, CONV_WIDTH):
                acc = acc + w_ref[k : k + 1, lanes] * ext[t + k]
            dc_ref[t, :, lanes] = acc
    for t in range(seq):
        y = jax.nn.silu(_layer_norm(dc_ref[t] + cb_ref[...], g_ref[...], b_ref[...]))
        for l in range(n_slabs):
            os_ref[l, pl.ds(t, n_seq, stride=seq), :] = y[:, l * LANES : (l + 1) * LANES]
    for l in range(n_slabs):
        o_ref[:, l * LANES : (l + 1) * LANES] = os_ref[l].astype(BF16)


def _conv_sample(z, hist, seq, conv_w, conv_b, ln_g, ln_b, *, n_seq, also_round=()):
    batch = hist.shape[1]
    steps = batch // n_seq
    slab_specs = [pl.BlockSpec((w.shape[0] // steps, w.shape[1]), lambda i: (i, 0)) for w in also_round]
    slab_scratch = pltpu.VMEM((D_B // LANES, n_seq * seq, LANES), F32)
    return pl.pallas_call(
        functools.partial(_conv_sample_kernel, seq=seq, n_round=len(also_round)),
        out_shape=[
            jax.ShapeDtypeStruct((batch * seq, D_B), BF16),
            jax.ShapeDtypeStruct((HIST, batch, D_B), F32),
        ] + [jax.ShapeDtypeStruct(w.shape, BF16) for w in also_round],
        grid=(steps,),
        in_specs=[
            pl.BlockSpec((n_seq * seq, COL_W), lambda i: (i, COL_GLU_A)),
            pl.BlockSpec((n_seq * seq, COL_W), lambda i: (i, COL_GLU_B)),
            pl.BlockSpec((HIST, n_seq, D_B), lambda i: (0, i, 0)),
            pl.BlockSpec((CONV_WIDTH, D_B), lambda i: (0, 0)),
            pl.BlockSpec((1, D_B), lambda i: (0, 0)),
            pl.BlockSpec((1, D_B), lambda i: (0, 0)),
            pl.BlockSpec((1, D_B), lambda i: (0, 0)),
        ] + slab_specs,
        out_specs=[
            pl.BlockSpec((n_seq * seq, D_B), lambda i: (i, 0)),
            pl.BlockSpec((HIST, n_seq, D_B), lambda i: (0, i, 0)),
        ] + slab_specs,
        scratch_shapes=[slab_scratch, pltpu.VMEM((seq, n_seq, D_B), F32), slab_scratch],
        compiler_params=_cparams("arbitrary"),
        name="conv_sample",
    )(z, z, hist, conv_w, conv_b.reshape(1, D_B), ln_g.reshape(1, D_B), ln_b.reshape(1, D_B), *also_round)


def _head_cols(h):
    return slice(h * MEM_HEAD_DIM, (h + 1) * MEM_HEAD_DIM)


def _split_lane_tiles(kv):
    b = kv.shape[0]
    kv = kv.reshape(b, N_MEM, N_MEM_HEADS, HEAD_LANE_TILES, LANES)
    return kv.transpose(0, 1, 3, 2, 4).reshape(b, N_MEM * HEAD_ROW_PITCH, LANES)


def _merge_lane_tiles(view, batch):
    view = view.reshape(batch, N_MEM, HEAD_LANE_TILES, N_MEM_HEADS, LANES)
    return view.transpose(0, 1, 3, 2, 4).reshape(batch, N_MEM, N_MEM_HEADS, MEM_HEAD_DIM)


def _head_of(kv_ref, s, h):
    if kv_ref.shape[-1] == LANES:
        tiles = [kv_ref[s, pl.ds(j * N_MEM_HEADS + h, N_MEM, stride=HEAD_ROW_PITCH), :]
                 for j in range(HEAD_LANE_TILES)]
        return jnp.concatenate(tiles, axis=-1).astype(BF16)
    return kv_ref[s, :, _head_cols(h)].astype(BF16)


def _attn_kernel(q_ref, k_ref, v_ref, o_ref, *, rows):
    scale = 1.0 / math.sqrt(MEM_HEAD_DIM)
    pairs = [(s, h) for s in range(k_ref.shape[0]) for h in range(N_MEM_HEADS)]
    scores = []
    for s, h in pairs:
        q = (q_ref[s * rows : (s + 1) * rows, _head_cols(h)] * scale).astype(BF16)
        scores.append(lax.dot_general(q, _head_of(k_ref, s, h), (((1,), (1,)), ((), ())),
                                      preferred_element_type=F32))
    sc = jnp.concatenate(scores, axis=0)
    p = jnp.exp(sc - jnp.max(sc, axis=-1, keepdims=True))
    p = p / jnp.sum(p, axis=-1, keepdims=True)
    for n, (s, h) in enumerate(pairs):
        ph = p[n * rows : (n + 1) * rows, :].astype(BF16)
        o = jnp.dot(ph, _head_of(v_ref, s, h), preferred_element_type=F32)
        o_ref[s * rows : (s + 1) * rows, _head_cols(h)] = o.astype(BF16)


def _merge_kernel(a_ref, b_ref, c_ref, *refs):
    gate_refs = refs[: 2 * N_BRANCH]
    bg_ref, x_ref, wa_ref, wb_ref, wc_ref, wo_ref, mg_ref, x1_ref, *h2_refs = refs[2 * N_BRANCH :]
    halves = []
    for half in range(2):
        cols = slice(half * COL_W, (half + 1) * COL_W)
        merged = None
        for br, (in_ref, w_ref) in enumerate(((a_ref, wa_ref), (b_ref, wb_ref), (c_ref, wc_ref))):
            y = jnp.dot(in_ref[...], w_ref[:, cols], preferred_element_type=F32)
            gcols = slice((2 * br + half) * COL_W, (2 * br + half + 1) * COL_W)
            term = jax.nn.sigmoid(gate_refs[2 * br + half][...] + bg_ref[:, gcols]) * y
            merged = term if merged is None else merged + term
        halves.append(merged.astype(BF16))
    merged16 = jnp.concatenate(halves, axis=-1)
    x1 = x_ref[...] + jnp.dot(merged16, wo_ref[...], preferred_element_type=F32)
    x1_ref[...] = x1
    if h2_refs:
        h2_refs[0][...] = _rms_norm(x1, mg_ref[...]).astype(BF16)


def _merge(a_in, b_in, c_in, z, cols, x, b_gate, w_a, w_b, w_c, w_o, mlp_g, *, tm, emit_h2):
    m = x.shape[0]
    const = lambda i: (0, 0)
    resident = functools.partial(pl.BlockSpec, index_map=const, pipeline_mode=pl.Buffered(1))
    gate_specs = [_col_block_spec(z, tm, cols.gates + n) for n in range(2 * N_BRANCH)]
    out_dtypes = [F32, BF16] if emit_h2 else [F32]
    return pl.pallas_call(
        _merge_kernel,
        out_shape=[jax.ShapeDtypeStruct((m, D_MODEL), dt) for dt in out_dtypes],
        grid=(m // tm,),
        in_specs=[pl.BlockSpec((tm, D_A), lambda i: (i, 0))] * 3
        + gate_specs
        + [
            pl.BlockSpec((1, N_BRANCH * D_MODEL), const),
            pl.BlockSpec((tm, D_MODEL), lambda i: (i, 0)),
            resident((D_A, D_MODEL)),
            resident((D_B, D_MODEL)),
            resident((D_C, D_MODEL)),
            resident((D_MODEL, D_MODEL)),
            pl.BlockSpec((1, D_MODEL), const),
        ],
        out_specs=[pl.BlockSpec((tm, D_MODEL), lambda i: (i, 0))] * len(out_dtypes),
        compiler_params=_cparams("arbitrary"),
        name="merge",
    )(a_in, b_in, c_in, *([z] * (2 * N_BRANCH)), b_gate.reshape(1, -1), x, w_a, w_b, w_c, w_o,
      mlp_g.reshape(1, D_MODEL))


def _mlp_kernel(x1_ref, mg_ref, wu_ref, wd_ref, fg_ref, y_ref, acc_ref, h2_ref):
    f = pl.program_id(1)
    tm = x1_ref.shape[0]

    def up(h2):
        return jnp.square(jnp.maximum(jnp.dot(h2, wu_ref[...], preferred_element_type=F32), 0.0)).astype(BF16)

    @pl.when(f == 0)
    def _():
        t_blocks = []
        for rb in range(tm // PROJ_ROWS):
            for r in range(rb * PROJ_ROWS, (rb + 1) * PROJ_ROWS, NORM_ROWS):
                h2_ref[r : r + NORM_ROWS, :] = _rms_norm(x1_ref[r : r + NORM_ROWS, :], mg_ref[...]).astype(BF16)
            t_blocks.append(up(h2_ref[rb * PROJ_ROWS : (rb + 1) * PROJ_ROWS, :]))
        t = jnp.concatenate(t_blocks, axis=0)
        acc_ref[...] = x1_ref[...] + jnp.dot(t, wd_ref[...], preferred_element_type=F32)

    @pl.when(f > 0)
    def _():
        acc_ref[...] += jnp.dot(up(h2_ref[...]), wd_ref[...], preferred_element_type=F32)

    @pl.when(f == pl.num_programs(1) - 1)
    def _():
        y_ref[...] = _rms_norm(acc_ref[...], fg_ref[...])


def _mlp(x1, mlp_g, w_up, w_down, final_g, *, tm, tf):
    m = x1.shape[0]
    return pl.pallas_call(
        _mlp_kernel,
        out_shape=jax.ShapeDtypeStruct((m, D_MODEL), F32),
        grid=(m // tm, D_FF // tf),
        in_specs=[
            pl.BlockSpec((tm, D_MODEL), lambda i, f: (i, 0)),
            pl.BlockSpec((1, D_MODEL), lambda i, f: (0, 0)),
            pl.BlockSpec((D_MODEL, tf), lambda i, f: (0, f)),
            pl.BlockSpec((tf, D_MODEL), lambda i, f: (f, 0)),
            pl.BlockSpec((1, D_MODEL), lambda i, f: (0, 0)),
        ],
        out_specs=pl.BlockSpec((tm, D_MODEL), lambda i, f: (i, 0)),
        scratch_shapes=[pltpu.VMEM((tm, D_MODEL), F32), pltpu.VMEM((tm, D_MODEL), BF16)],
        compiler_params=_cparams("arbitrary", "arbitrary"),
        name="mlp",
    )(x1, mlp_g.reshape(1, D_MODEL), w_up, w_down, final_g.reshape(1, D_MODEL))


def _mlp_first_kernel(h2_ref, wu_ref, wd_ref, acc_ref, wu16_ref, wd16_ref):
    @pl.when(pl.program_id(0) == 0)
    def _():
        acc_ref[...] = jnp.zeros(acc_ref.shape, F32)

    h2 = h2_ref[...]
    t_pieces = []
    for c in range(0, wu_ref.shape[1], PROJ_LANES):
        wu = wu_ref[:, c : c + PROJ_LANES].astype(BF16)
        wu16_ref[:, c : c + PROJ_LANES] = wu
        t = jnp.square(jnp.maximum(jnp.dot(h2, wu, preferred_element_type=F32), 0.0))
        t_pieces.append(t.astype(BF16))
    t16 = jnp.concatenate(t_pieces, axis=-1)
    for c in range(0, wd_ref.shape[1], PROJ_LANES):
        wd = wd_ref[:, c : c + PROJ_LANES].astype(BF16)
        wd16_ref[:, c : c + PROJ_LANES] = wd
        acc_ref[:, c : c + PROJ_LANES] += jnp.dot(t16, wd, preferred_element_type=F32)


def _mlp_first(h2, w_up, w_down, *, tf):
    m = h2.shape[0]
    resident = pl.BlockSpec((m, D_MODEL), lambda f: (0, 0))
    return pl.pallas_call(
        _mlp_first_kernel,
        out_shape=[
            jax.ShapeDtypeStruct((m, D_MODEL), F32),
            jax.ShapeDtypeStruct((D_MODEL, D_FF), BF16),
            jax.ShapeDtypeStruct((D_FF, D_MODEL), BF16),
        ],
        grid=(D_FF // tf,),
        in_specs=[
            resident,
            pl.BlockSpec((D_MODEL, tf), lambda f: (0, f)),
            pl.BlockSpec((tf, D_MODEL), lambda f: (f, 0)),
        ],
        out_specs=[
            resident,
            pl.BlockSpec((D_MODEL, tf), lambda f: (0, f)),
            pl.BlockSpec((tf, D_MODEL), lambda f: (f, 0)),
        ],
        compiler_params=_cparams("arbitrary"),
        name="mlp_first",
    )(h2, w_up, w_down)


def _residual_norm_kernel(x_ref, d_ref, g_ref, y_ref):
    y_ref[...] = _rms_norm(x_ref[...] + d_ref[...], g_ref[...])


def _residual_norm(x, d, g, *, tm):
    m = x.shape[0]
    row_spec = pl.BlockSpec((tm, D_MODEL), lambda i: (i, 0))
    return pl.pallas_call(
        _residual_norm_kernel,
        out_shape=jax.ShapeDtypeStruct((m, D_MODEL), F32),
        grid=(m // tm,),
        in_specs=[row_spec, row_spec, pl.BlockSpec((1, D_MODEL), lambda i: (0, 0))],
        out_specs=row_spec,
        compiler_params=_cparams("arbitrary"),
        name="residual_norm",
    )(x, d, g.reshape(1, D_MODEL))


SAMPLE_ATTN_SEQS = 4


def _branches_kernel(u_ref, v_ref, g_ref, b_ref, sw_ref, sb_ref, qp_ref, kp_ref, vp_ref, qs_ref, ks_ref, vs_ref,
                     a_ref, op_ref, os_ref, *, chunk, rows_p, seq_s):
    _sgu_kernel(u_ref, v_ref, g_ref, b_ref, sw_ref, sb_ref, a_ref, chunk=chunk)
    _attn_kernel(qp_ref, kp_ref, vp_ref, op_ref, rows=rows_p)
    _attn_kernel(qs_ref, ks_ref, vs_ref, os_ref, rows=seq_s)


def _branches(z_p, cols_p, seq_p, kv_p, z_s, cols_s, seq_s, k_s, v_s, ln_g, ln_b, sgu_w, sgu_b):
    m_p, m_s = z_p.shape[-2], z_s.shape[-2]
    steps = m_s // (seq_s * SAMPLE_ATTN_SEQS)
    rows_p = m_p // steps
    assert rows_p % CHUNK == 0 and seq_p % rows_p == 0
    steps_per_seq = seq_p // rows_p
    rows_s = seq_s * SAMPLE_ATTN_SEQS
    const2 = lambda i: (0, 0)
    kv_p_specs = [pl.BlockSpec((1, N_MEM, D_C), functools.partial(lambda i, col: (i // steps_per_seq, 0, col), col=col))
                  for col in (0, 1)]
    kv_s_spec = pl.BlockSpec((SAMPLE_ATTN_SEQS, N_MEM * HEAD_ROW_PITCH, LANES), lambda i: (i, 0, 0))
    return pl.pallas_call(
        functools.partial(_branches_kernel, chunk=min(seq_p, CHUNK), rows_p=rows_p, seq_s=seq_s),
        out_shape=[
            jax.ShapeDtypeStruct((m_p, D_A), BF16),
            jax.ShapeDtypeStruct((m_p, D_C), BF16),
            jax.ShapeDtypeStruct((m_s, D_C), BF16),
        ],
        grid=(steps,),
        in_specs=[
            _col_block_spec(z_p, rows_p, cols_p.u),
            _col_block_spec(z_p, rows_p, cols_p.v),
            pl.BlockSpec((1, D_A), const2),
            pl.BlockSpec((1, D_A), const2),
            pl.BlockSpec((SGU_GROUPS, CHUNK, CHUNK), lambda i: (0, 0, 0)),
            pl.BlockSpec((SGU_GROUPS, CHUNK), const2),
            _col_block_spec(z_p, rows_p, cols_p.q),
            *kv_p_specs,
            _col_block_spec(z_s, rows_s, cols_s.q),
            kv_s_spec,
            kv_s_spec,
        ],
        out_specs=[
            pl.BlockSpec((rows_p, D_A), lambda i: (i, 0)),
            pl.BlockSpec((rows_p, D_C), lambda i: (i, 0)),
            pl.BlockSpec((rows_s, D_C), lambda i: (i, 0)),
        ],
        compiler_params=_cparams("arbitrary"),
        name="branches",
    )(z_p, z_p, ln_g.reshape(1, D_A), ln_b.reshape(1, D_A), sgu_w, sgu_b, z_p, kv_p, kv_p, z_s, k_s, v_s)


def kernel(x_prompt, x_sample, mem_prompt, cache_mem_k, cache_mem_v, state_conv, attn_norm_g, w_in, b_gate,
           sgu_ln_g, sgu_ln_b, sgu_w, sgu_b, w_a_out, conv_w, conv_b, conv_ln_g, conv_ln_b, w_b_out, mem_norm_g,
           w_mem_kv, w_c_out, w_o, mlp_norm_g, w_up, w_down, final_norm_g):
    assert attn_norm_g.shape[0] == 1, "single-layer trunk"
    bp, sp, _ = x_prompt.shape
    bs, ss, _ = x_sample.shape
    p = dict(
        attn_norm_g=attn_norm_g[0], w_in=w_in[0], b_gate=b_gate[0], sgu_ln_g=sgu_ln_g[0],
        sgu_ln_b=sgu_ln_b[0], sgu_w=sgu_w[0], sgu_b=sgu_b[0], w_a_out=w_a_out[0],
        conv_w=conv_w[0], conv_b=conv_b[0], conv_ln_g=conv_ln_g[0], conv_ln_b=conv_ln_b[0],
        w_b_out=w_b_out[0], w_c_out=w_c_out[0], w_o=w_o[0],
        mlp_norm_g=mlp_norm_g[0], w_up=w_up[0], w_down=w_down[0],
        final_norm_g=final_norm_g,
    )
    kv, kv_split = _norm_matmul(mem_prompt.reshape(bp * N_MEM, D_MODEL), mem_norm_g[0], w_mem_kv[0],
                                tm=1024, tn=1024, emit_head_split=True)
    kv = kv.reshape(bp, N_MEM, 2 * D_C)
    xs = x_sample.reshape(bs * ss, D_MODEL)
    xp = x_prompt.reshape(bp * sp, D_MODEL)
    conv_args = (p["conv_w"], p["conv_b"], p["conv_ln_g"], p["conv_ln_b"])
    sgu_args = (p["sgu_ln_g"], p["sgu_ln_b"], p["sgu_w"], p["sgu_b"])

    z_s, w_in16 = _norm_matmul(xs, p["attn_norm_g"], p["w_in"], tm=1024, tn=1024, emit_w16=True)
    out_names = ("w_a_out", "w_b_out", "w_c_out", "w_o")
    b_s, hist_s, *rounded = _conv_sample(z_s, jnp.transpose(state_conv[0], (1, 0, 2)), ss, *conv_args, n_seq=16,
                                         also_round=[p[name] for name in out_names])
    p = {**p, **dict(zip(out_names, rounded))}
    a_s, vn_s = _sgu(z_s, Z_COLS_ALL, *sgu_args, chunk=min(ss, CHUNK), rows=1024, want_vn=True)

    z_p, b_p, hist_p = _inproj_conv(xp, sp, p["attn_norm_g"], w_in16, *conv_args, tm=1024)
    a_p, c_p, c_s = _branches(z_p, Z_COLS_NO_GLU, sp, kv, z_s, Z_COLS_ALL, ss, _split_lane_tiles(cache_mem_k[0]),
                              _split_lane_tiles(cache_mem_v[0]), *sgu_args)

    merge_w = (p["b_gate"], p["w_a_out"], p["w_b_out"], p["w_c_out"], p["w_o"], p["mlp_norm_g"])
    x1_s, h2_s = _merge(a_s, b_s, c_s, z_s, Z_COLS_ALL, xs, *merge_w, tm=256, emit_h2=True)
    d_s, w_up16, w_down16 = _mlp_first(h2_s, p["w_up"], p["w_down"], tf=512)
    y_s = _residual_norm(x1_s, d_s, p["final_norm_g"], tm=512)
    (x1_p,) = _merge(a_p, b_p, c_p, z_p, Z_COLS_NO_GLU, xp, *merge_w, tm=256, emit_h2=False)
    y_p = _mlp(x1_p, p["mlp_norm_g"], w_up16, w_down16, p["final_norm_g"], tm=512, tf=1024)
    return (
        y_p.reshape(bp, sp, D_MODEL),
        y_s.reshape(bs, ss, D_MODEL),
        _merge_lane_tiles(kv_split[0], bp)[None],
        _merge_lane_tiles(kv_split[1], bp)[None],
        hist_p[None],
        jnp.transpose(hist_s, (1, 0, 2))[None],
        vn_s.reshape(1, bs, ss, D_A),
    )
```

```python
import functools
import math
from typing import NamedTuple

import jax
import jax.numpy as jnp
from jax import lax
from jax.experimental import pallas as pl
from jax.experimental.pallas import tpu as pltpu

F32 = jnp.float32
BF16 = jnp.bfloat16

D_MODEL = 2048
D_A = 1024
D_B = 1024
D_C = 1024
CHUNK = 128
SGU_GROUPS = 8
SGU_HEAD = D_A // SGU_GROUPS
CONV_WIDTH = 31
HIST = CONV_WIDTH - 1
N_MEM = 256
N_MEM_HEADS = 4
MEM_HEAD_DIM = D_C // N_MEM_HEADS
N_BRANCH = 3
D_FF = 4 * D_MODEL
N_IN = 2 * D_A + 2 * D_B + D_C + N_BRANCH * D_MODEL
EPS = 1e-6

COL_U, COL_V, COL_GLU_A, COL_GLU_B, COL_Q, COL_GATES = 0, 1, 2, 3, 4, 5
COL_W = 1024


class ZCols(NamedTuple):
    u: int
    v: int
    q: int
    gates: int


def _col_block_spec(z, rows, col):
    if z.ndim == 3:
        return pl.BlockSpec((None, rows, COL_W), lambda i: (col, i, 0))
    return pl.BlockSpec((rows, COL_W), lambda i: (i, col))


Z_COLS_ALL = ZCols(COL_U, COL_V, COL_Q, COL_GATES)
Z_COLS_NO_GLU = ZCols(COL_U, COL_V, COL_Q - 2, COL_GATES - 2)

V7X_VMEM_BYTES = 64 * 1024 * 1024
VMEM_LIMIT_BYTES = V7X_VMEM_BYTES - 8 * 1024 * 1024
SUBLANES = 8
LANES = 128
HEAD_LANE_TILES = MEM_HEAD_DIM // LANES
HEAD_ROW_PITCH = N_MEM_HEADS * HEAD_LANE_TILES
V7X_MXU_COLS = 256
PROJ_LANES = V7X_MXU_COLS
PROJ_ROWS = 256
HIST_PAD = 32


def _cparams(*semantics, flags=None):
    return pltpu.CompilerParams(dimension_semantics=semantics, vmem_limit_bytes=VMEM_LIMIT_BYTES, flags=flags)


def _not_before(value, anchor):
    tile = anchor[-SUBLANES:, -LANES:]
    zero = pltpu.bitcast((pltpu.bitcast(tile, jnp.uint32) >> 16) >> 16, F32)
    reps = (value.shape[0] // SUBLANES, value.shape[1] // LANES)
    return value + jnp.tile(zero, reps)


def _rms_norm(x, g):
    ms = jnp.mean(x * x, axis=-1, keepdims=True)
    return x * lax.rsqrt(ms + EPS) * g


def _layer_norm(x, g, b):
    mu = jnp.mean(x, axis=-1, keepdims=True)
    xc = x - mu
    var = jnp.mean(xc * xc, axis=-1, keepdims=True)
    return xc * lax.rsqrt(var + EPS) * g + b


NORM_ROWS = 128


def _norm_matmul_kernel(x_ref, g_ref, w_ref, o_ref, *rest, emit_w16, emit_head_split):
    *extra, h_ref = rest
    w16_ref = extra.pop(0) if emit_w16 else None
    split_ref = extra.pop(0) if emit_head_split else None

    @pl.when(pl.program_id(1) == 0)
    def _():
        def body(r, carry):
            rows = pl.ds(pl.multiple_of(r * NORM_ROWS, NORM_ROWS), NORM_ROWS)
            h_ref[rows, :] = _rms_norm(x_ref[rows, :], g_ref[...]).astype(BF16)
            return carry

        lax.fori_loop(0, x_ref.shape[0] // NORM_ROWS, body, 0)

    tn = w_ref.shape[1]
    for c in range(0, tn, PROJ_LANES):
        w16 = w_ref[:, c : c + PROJ_LANES].astype(BF16)
        if emit_w16:
            w16_ref[:, c : c + PROJ_LANES] = w16
        piece = jnp.dot(h_ref[...], w16, preferred_element_type=F32)
        o_ref[:, c : c + PROJ_LANES] = piece
        if emit_head_split:
            head = ((pl.program_id(1) * tn + c) % D_C) // MEM_HEAD_DIM
            for lt in range(HEAD_LANE_TILES):
                split_ref[pl.ds(lt * N_MEM_HEADS + head, piece.shape[0], stride=HEAD_ROW_PITCH), :] = (
                    piece[:, lt * LANES : (lt + 1) * LANES])


def _norm_matmul(x, g, w, *, tm, tn, emit_w16=False, emit_head_split=False):
    m, k = x.shape
    n = w.shape[1]
    out_shape = [jax.ShapeDtypeStruct((m, n), F32)]
    out_specs = [pl.BlockSpec((tm, tn), lambda i, j: (i, j))]
    if emit_w16:
        assert m == tm, "every weight block must be visited exactly once"
        per_col = COL_W // tn
        out_shape.append(jax.ShapeDtypeStruct((n // COL_W, k, COL_W), BF16))
        out_specs.append(pl.BlockSpec((None, k, tn), lambda i, j: (j // per_col, 0, j % per_col)))
    if emit_head_split:
        assert m == tm and PROJ_LANES == MEM_HEAD_DIM and n == 2 * D_C
        out_shape.append(jax.ShapeDtypeStruct((n // D_C, m * HEAD_ROW_PITCH, LANES), F32))
        out_specs.append(pl.BlockSpec((None, m * HEAD_ROW_PITCH, LANES), lambda i, j: (j // (D_C // tn), 0, 0)))
    outs = pl.pallas_call(
        functools.partial(_norm_matmul_kernel, emit_w16=emit_w16, emit_head_split=emit_head_split),
        out_shape=out_shape,
        grid=(m // tm, n // tn),
        in_specs=[
            pl.BlockSpec((tm, k), lambda i, j: (i, 0), pipeline_mode=pl.Buffered(1 if m == tm else 2)),
            pl.BlockSpec((1, k), lambda i, j: (0, 0)),
            pl.BlockSpec((k, tn), lambda i, j: (0, j)),
        ],
        out_specs=out_specs,
        scratch_shapes=[pltpu.VMEM((tm, k), BF16)],
        compiler_params=_cparams("arbitrary", "arbitrary"),
        name="norm_matmul",
    )(x, g.reshape(1, k), w)
    return outs if len(outs) > 1 else outs[0]


def _lane_periodic(x, period):
    lane = lax.broadcasted_iota(jnp.int32, x.shape, 1)
    y = jnp.where(lane < period, x, 0.0)
    while period < LANES:
        y = y + pltpu.roll(y, period, axis=1)
        period *= 2
    return y


def _sgu_kernel(u_ref, v_ref, g_ref, b_ref, w_ref, sb_ref, a_ref, *vn_refs, chunk):
    r = lax.broadcasted_iota(jnp.int32, (CHUNK, CHUNK), 0)
    c = lax.broadcasted_iota(jnp.int32, (CHUNK, CHUNK), 1)
    seg_bits = chunk.bit_length() - 1
    same_segment = (r >> seg_bits) == (c >> seg_bits)
    mask = jnp.logical_and(same_segment, (r & (chunk - 1)) >= (c & (chunk - 1)))
    sb = sb_ref[...] if chunk == CHUNK else _lane_periodic(sb_ref[...], chunk)
    w_s, bias = [], []
    for grp in range(SGU_GROUPS):
        if chunk == CHUNK:
            w = w_ref[grp]
        else:
            w = _lane_periodic(jnp.tile(w_ref[grp, 0:chunk, :], (CHUNK // chunk, 1)), chunk)
        w_s.append(jnp.where(mask, w, 0.0).astype(BF16))
        col = jnp.sum(jnp.where(r == c, sb[grp : grp + 1, :], 0.0), axis=1, keepdims=True)
        bias.append(jnp.broadcast_to(col, (CHUNK, SGU_HEAD)))
    for blk in range(u_ref.shape[0] // CHUNK):
        rows = slice(blk * CHUNK, (blk + 1) * CHUNK)
        vn = _layer_norm(jax.nn.gelu(v_ref[rows, :]), g_ref[...], b_ref[...])
        if vn_refs:
            vn_refs[0][rows, :] = vn
        vn16 = vn.astype(BF16)
        for grp in range(SGU_GROUPS):
            cols = slice(grp * SGU_HEAD, (grp + 1) * SGU_HEAD)
            mixed = jnp.dot(w_s[grp], vn16[:, cols], preferred_element_type=F32) + bias[grp]
            a_ref[rows, cols] = (jax.nn.gelu(u_ref[rows, cols]) * mixed).astype(BF16)


def _sgu(z, cols, ln_g, ln_b, sgu_w, sgu_b, *, chunk, rows, want_vn):
    m = z.shape[-2]
    out_shape = [jax.ShapeDtypeStruct((m, D_A), BF16)]
    out_specs = [pl.BlockSpec((rows, D_A), lambda i: (i, 0))]
    if want_vn:
        out_shape.append(jax.ShapeDtypeStruct((m, D_A), F32))
        out_specs.append(pl.BlockSpec((rows, D_A), lambda i: (i, 0)))
    return pl.pallas_call(
        functools.partial(_sgu_kernel, chunk=chunk),
        out_shape=out_shape,
        grid=(m // rows,),
        in_specs=[
            _col_block_spec(z, rows, cols.u),
            _col_block_spec(z, rows, cols.v),
            pl.BlockSpec((1, D_A), lambda i: (0, 0)),
            pl.BlockSpec((1, D_A), lambda i: (0, 0)),
            pl.BlockSpec((SGU_GROUPS, CHUNK, CHUNK), lambda i: (0, 0, 0)),
            pl.BlockSpec((SGU_GROUPS, CHUNK), lambda i: (0, 0)),
        ],
        out_specs=out_specs,
        compiler_params=_cparams("arbitrary"),
        name="sgu",
    )(z, z, ln_g.reshape(1, D_A), ln_b.reshape(1, D_A), sgu_w, sgu_b)


CONV_ROWS = 32
CONV_LANES = 128
CONV_LEAD = HIST_PAD - HIST


def _conv_window(win, w_ref, lanes, n_rows):
    out = None
    for b in range(SUBLANES):
        rows_b = n_rows if b == 0 else n_rows + SUBLANES
        y = None
        for a in range(HIST_PAD // SUBLANES + 1):
            k = SUBLANES * a + b - CONV_LEAD
            if 0 <= k < CONV_WIDTH:
                term = w_ref[k : k + 1, lanes] * win[SUBLANES * a : SUBLANES * a + rows_b, :]
                y = term if y is None else y + term
        shifted = y[b : b + n_rows, :]
        out = shifted if out is None else out + shifted
    return out


CONV_STEP_ROWS = 128
CONV_LEAD_PIECES = 2
N_GLU_STEPS = 2
N_COLS = N_IN // COL_W


def _glu_first(j):
    return jnp.where(j < N_GLU_STEPS, j + COL_GLU_A, jnp.where(j < COL_GLU_A + N_GLU_STEPS, j - N_GLU_STEPS, j))


def _inproj_conv_kernel(x_ref, g_ref, w_ref, cw_ref, cb_ref, lg_ref, lb_ref, z_ref, bin_ref, hist_ref,
                        h_ref, ext_ref, *, tiles_per_seq):
    i = pl.program_id(0)
    j = pl.program_id(1)
    t = x_ref.shape[0]

    def proj():
        return jnp.dot(h_ref[...], w_ref[...], preferred_element_type=F32)

    @pl.when(j == 0)
    def _():
        @pl.when(i % tiles_per_seq == 0)
        def _():
            ext_ref[0:HIST_PAD, :] = jnp.zeros((HIST_PAD, D_B), F32)

        @pl.when(i % tiles_per_seq != 0)
        def _():
            ext_ref[0:HIST_PAD, :] = ext_ref[t : t + HIST_PAD, :]

        for rb in range(t // PROJ_ROWS):
            for r in range(rb * PROJ_ROWS, (rb + 1) * PROJ_ROWS, NORM_ROWS):
                h_ref[r : r + NORM_ROWS, :] = _rms_norm(x_ref[r : r + NORM_ROWS, :], g_ref[...]).astype(BF16)
            rows = slice(rb * PROJ_ROWS, (rb + 1) * PROJ_ROWS)
            ext_ref[HIST_PAD + rb * PROJ_ROWS : HIST_PAD + (rb + 1) * PROJ_ROWS, :] = jnp.dot(
                h_ref[rows, :], w_ref[...], preferred_element_type=F32)

    @pl.when(j == 1)
    def _():
        ext_ref[HIST_PAD : HIST_PAD + t, :] = ext_ref[HIST_PAD : HIST_PAD + t, :] * jax.nn.sigmoid(proj())
        hist_ref[0] = ext_ref[t + CONV_LEAD : t + HIST_PAD, :]

    @pl.when(j >= N_GLU_STEPS)
    def _():
        chunk = jnp.minimum(j - N_GLU_STEPS, t // CONV_STEP_ROWS - 1)
        r0 = pl.multiple_of(chunk * CONV_STEP_ROWS, CONV_STEP_ROWS)
        n_row_blocks = CONV_STEP_ROWS // CONV_ROWS
        n_lane_blocks = D_B // PROJ_LANES
        rows_per_block = t // n_row_blocks
        pieces = [(rc, lb) for rc in range(n_row_blocks) for lb in range(n_lane_blocks)]
        vec_done, proj_done = [], []
        parts = []

        def conv_piece(p):
            rc, lb = pieces[p]
            wins = []
            for cl in range(lb * PROJ_LANES, (lb + 1) * PROJ_LANES, CONV_LANES):
                clanes = slice(cl, cl + CONV_LANES)
                win = ext_ref[pl.ds(r0 + rc * CONV_ROWS, CONV_ROWS + HIST_PAD), clanes]
                if p > CONV_LEAD_PIECES:
                    win = _not_before(win, proj_done[p - CONV_LEAD_PIECES - 1])
                wins.append(_conv_window(win, cw_ref, clanes, CONV_ROWS))
            parts.append(jnp.concatenate(wins, axis=-1))
            vec_done.append(parts[-1])
            if lb == n_lane_blocks - 1:
                dc = jnp.concatenate(parts, axis=-1) + cb_ref[...]
                parts.clear()
                y = jax.nn.silu(_layer_norm(dc, lg_ref[...], lb_ref[...]))
                bin_ref[pl.ds(r0 + rc * CONV_ROWS, CONV_ROWS), :] = y.astype(BF16)
                vec_done.append(y)

        n_conv = 0
        for q, (rc, lb) in enumerate(pieces):
            while n_conv < min(q + CONV_LEAD_PIECES + 1, len(pieces)):
                conv_piece(n_conv)
                n_conv += 1
            rows = slice(rc * rows_per_block, (rc + 1) * rows_per_block)
            lanes = slice(lb * PROJ_LANES, (lb + 1) * PROJ_LANES)
            zp = jnp.dot(h_ref[rows, :], w_ref[:, lanes], preferred_element_type=F32)
            if len(vec_done) >= 2:
                zp = _not_before(zp, vec_done[-2])
            z_ref[rows, lanes] = zp
            proj_done.append(zp)


def _inproj_conv(x, seq, g, w16, conv_w, conv_b, ln_g, ln_b, *, tm):
    m, k = x.shape
    tiles_per_seq = seq // tm
    n_conv_steps = N_COLS - N_GLU_STEPS
    assert tm // CONV_STEP_ROWS <= n_conv_steps, "not enough grid steps to convolve the whole tile"
    const = lambda i, j: (0, 0)
    return pl.pallas_call(
        functools.partial(_inproj_conv_kernel, tiles_per_seq=tiles_per_seq),
        out_shape=[
            jax.ShapeDtypeStruct((n_conv_steps, m, COL_W), F32),
            jax.ShapeDtypeStruct((m, D_B), BF16),
            jax.ShapeDtypeStruct((m // seq, HIST, D_B), F32),
        ],
        grid=(m // tm, N_COLS),
        in_specs=[
            pl.BlockSpec((tm, k), lambda i, j: (i, 0)),
            pl.BlockSpec((1, k), const),
            pl.BlockSpec((None, k, COL_W), lambda i, j: (_glu_first(j), 0, 0)),
            pl.BlockSpec((CONV_WIDTH, D_B), const),
            pl.BlockSpec((1, D_B), const),
            pl.BlockSpec((1, D_B), const),
            pl.BlockSpec((1, D_B), const),
        ],
        out_specs=[
            pl.BlockSpec((None, tm, COL_W), lambda i, j: (jnp.maximum(j - N_GLU_STEPS, 0), i, 0)),
            pl.BlockSpec((tm, D_B), lambda i, j: (i, 0)),
            pl.BlockSpec((1, HIST, D_B), lambda i, j: (i // tiles_per_seq, 0, 0)),
        ],
        scratch_shapes=[pltpu.VMEM((tm, k), BF16), pltpu.VMEM((tm + HIST_PAD, D_B), F32)],
        compiler_params=_cparams("arbitrary", "arbitrary", ),
        name="inproj_conv",
    )(x, g.reshape(1, k), w16, conv_w, conv_b.reshape(1, D_B), ln_g.reshape(1, D_B), ln_b.reshape(1, D_B))


def _conv_sample_kernel(*refs, seq, n_round):
    ga_ref, gb_ref, hist_ref, w_ref, cb_ref, g_ref, b_ref = refs[:7]
    f32_refs = refs[7 : 7 + n_round]
    o_ref, nh_ref = refs[7 + n_round : 9 + n_round]
    bf16_refs = refs[9 + n_round : 9 + 2 * n_round]
    cs_ref, dc_ref, os_ref = refs[9 + 2 * n_round :]
    for src, dst in zip(f32_refs, bf16_refs):
        dst[...] = src[...].astype(BF16)
    n_seq = hist_ref.shape[1]
    n_slabs = D_B // LANES
    c = ga_ref[...] * jax.nn.sigmoid(gb_ref[...])
    for l in range(n_slabs):
        cs_ref[l] = c[:, l * LANES : (l + 1) * LANES]
    new = [jnp.concatenate([cs_ref[l, pl.ds(t, n_seq, stride=seq), :] for l in range(n_slabs)], axis=-1)
           for t in range(seq)]
    nh_ref[0 : HIST - seq] = hist_ref[seq:HIST]
    for t in range(seq):
        nh_ref[HIST - seq + t] = new[t]
    for lb in range(D_B // CONV_LANES):
        lanes = slice(lb * CONV_LANES, (lb + 1) * CONV_LANES)
        ext = [hist_ref[j, :, lanes] for j in range(HIST)] + [new[t][:, lanes] for t in range(seq)]
        for t in range(seq):
            acc = w_ref[0:1, lanes] * ext[t]
            for k in range(1, CONV_WIDTH):
                acc = acc + w_ref[k : k + 1, lanes] * ext[t + k]
            dc_ref[t, :, lanes] = acc
    for t in range(seq):
        y = jax.nn.silu(_layer_norm(dc_ref[t] + cb_ref[...], g_ref[...], b_ref[...]))
        for l in range(n_slabs):
            os_ref[l, pl.ds(t, n_seq, stride=seq), :] = y[:, l * LANES : (l + 1) * LANES]
    for l in range(n_slabs):
        o_ref[:, l * LANES : (l + 1) * LANES] = os_ref[l].astype(BF16)


def _conv_sample(z, hist, seq, conv_w, conv_b, ln_g, ln_b, *, n_seq, also_round=()):
    batch = hist.shape[1]
    steps = batch // n_seq
    slab_specs = [pl.BlockSpec((w.shape[0] // steps, w.shape[1]), lambda i: (i, 0)) for w in also_round]
    slab_scratch = pltpu.VMEM((D_B // LANES, n_seq * seq, LANES), F32)
    return pl.pallas_call(
        functools.partial(_conv_sample_kernel, seq=seq, n_round=len(also_round)),
        out_shape=[
            jax.ShapeDtypeStruct((batch * seq, D_B), BF16),
            jax.ShapeDtypeStruct((HIST, batch, D_B), F32),
        ] + [jax.ShapeDtypeStruct(w.shape, BF16) for w in also_round],
        grid=(steps,),
        in_specs=[
            pl.BlockSpec((n_seq * seq, COL_W), lambda i: (i, COL_GLU_A)),
            pl.BlockSpec((n_seq * seq, COL_W), lambda i: (i, COL_GLU_B)),
            pl.BlockSpec((HIST, n_seq, D_B), lambda i: (0, i, 0)),
            pl.BlockSpec((CONV_WIDTH, D_B), lambda i: (0, 0)),
            pl.BlockSpec((1, D_B), lambda i: (0, 0)),
            pl.BlockSpec((1, D_B), lambda i: (0, 0)),
            pl.BlockSpec((1, D_B), lambda i: (0, 0)),
        ] + slab_specs,
        out_specs=[
            pl.BlockSpec((n_seq * seq, D_B), lambda i: (i, 0)),
            pl.BlockSpec((HIST, n_seq, D_B), lambda i: (0, i, 0)),
        ] + slab_specs,
        scratch_shapes=[slab_scratch, pltpu.VMEM((seq, n_seq, D_B), F32), slab_scratch],
        compiler_params=_cparams("arbitrary"),
        name="conv_sample",
    )(z, z, hist, conv_w, conv_b.reshape(1, D_B), ln_g.reshape(1, D_B), ln_b.reshape(1, D_B), *also_round)


def _head_cols(h):
    return slice(h * MEM_HEAD_DIM, (h + 1) * MEM_HEAD_DIM)


def _split_lane_tiles(kv):
    b = kv.shape[0]
    kv = kv.reshape(b, N_MEM, N_MEM_HEADS, HEAD_LANE_TILES, LANES)
    return kv.transpose(0, 1, 3, 2, 4).reshape(b, N_MEM * HEAD_ROW_PITCH, LANES)


def _merge_lane_tiles(view, batch):
    view = view.reshape(batch, N_MEM, HEAD_LANE_TILES, N_MEM_HEADS, LANES)
    return view.transpose(0, 1, 3, 2, 4).reshape(batch, N_MEM, N_MEM_HEADS, MEM_HEAD_DIM)


def _head_of(kv_ref, s, h):
    if kv_ref.shape[-1] == LANES:
        tiles = [kv_ref[s, pl.ds(j * N_MEM_HEADS + h, N_MEM, stride=HEAD_ROW_PITCH), :]
                 for j in range(HEAD_LANE_TILES)]
        return jnp.concatenate(tiles, axis=-1).astype(BF16)
    return kv_ref[s, :, _head_cols(h)].astype(BF16)


def _attn_kernel(q_ref, k_ref, v_ref, o_ref, *, rows):
    scale = 1.0 / math.sqrt(MEM_HEAD_DIM)
    pairs = [(s, h) for s in range(k_ref.shape[0]) for h in range(N_MEM_HEADS)]
    scores = []
    for s, h in pairs:
        q = (q_ref[s * rows : (s + 1) * rows, _head_cols(h)] * scale).astype(BF16)
        scores.append(lax.dot_general(q, _head_of(k_ref, s, h), (((1,), (1,)), ((), ())),
                                      preferred_element_type=F32))
    sc = jnp.concatenate(scores, axis=0)
    p = jnp.exp(sc - jnp.max(sc, axis=-1, keepdims=True))
    p = p / jnp.sum(p, axis=-1, keepdims=True)
    for n, (s, h) in enumerate(pairs):
        ph = p[n * rows : (n + 1) * rows, :].astype(BF16)
        o = jnp.dot(ph, _head_of(v_ref, s, h), preferred_element_type=F32)
        o_ref[s * rows : (s + 1) * rows, _head_cols(h)] = o.astype(BF16)


def _merge_kernel(a_ref, b_ref, c_ref, *refs):
    gate_refs = refs[: 2 * N_BRANCH]
    bg_ref, x_ref, wa_ref, wb_ref, wc_ref, wo_ref, mg_ref, x1_ref, *h2_refs = refs[2 * N_BRANCH :]
    halves = []
    for half in range(2):
        cols = slice(half * COL_W, (half + 1) * COL_W)
        merged = None
        for br, (in_ref, w_ref) in enumerate(((a_ref, wa_ref), (b_ref, wb_ref), (c_ref, wc_ref))):
            y = jnp.dot(in_ref[...], w_ref[:, cols], preferred_element_type=F32)
            gcols = slice((2 * br + half) * COL_W, (2 * br + half + 1) * COL_W)
            term = jax.nn.sigmoid(gate_refs[2 * br + half][...] + bg_ref[:, gcols]) * y
            merged = term if merged is None else merged + term
        halves.append(merged.astype(BF16))
    merged16 = jnp.concatenate(halves, axis=-1)
    x1 = x_ref[...] + jnp.dot(merged16, wo_ref[...], preferred_element_type=F32)
    x1_ref[...] = x1
    if h2_refs:
        h2_refs[0][...] = _rms_norm(x1, mg_ref[...]).astype(BF16)


def _merge(a_in, b_in, c_in, z, cols, x, b_gate, w_a, w_b, w_c, w_o, mlp_g, *, tm, emit_h2):
    m = x.shape[0]
    const = lambda i: (0, 0)
    resident = functools.partial(pl.BlockSpec, index_map=const, pipeline_mode=pl.Buffered(1))
    gate_specs = [_col_block_spec(z, tm, cols.gates + n) for n in range(2 * N_BRANCH)]
    out_dtypes = [F32, BF16] if emit_h2 else [F32]
    return pl.pallas_call(
        _merge_kernel,
        out_shape=[jax.ShapeDtypeStruct((m, D_MODEL), dt) for dt in out_dtypes],
        grid=(m // tm,),
        in_specs=[pl.BlockSpec((tm, D_A), lambda i: (i, 0))] * 3
        + gate_specs
        + [
            pl.BlockSpec((1, N_BRANCH * D_MODEL), const),
            pl.BlockSpec((tm, D_MODEL), lambda i: (i, 0)),
            resident((D_A, D_MODEL)),
            resident((D_B, D_MODEL)),
            resident((D_C, D_MODEL)),
            resident((D_MODEL, D_MODEL)),
            pl.BlockSpec((1, D_MODEL), const),
        ],
        out_specs=[pl.BlockSpec((tm, D_MODEL), lambda i: (i, 0))] * len(out_dtypes),
        compiler_params=_cparams("arbitrary"),
        name="merge",
    )(a_in, b_in, c_in, *([z] * (2 * N_BRANCH)), b_gate.reshape(1, -1), x, w_a, w_b, w_c, w_o,
      mlp_g.reshape(1, D_MODEL))


def _mlp_kernel(x1_ref, mg_ref, wu_ref, wd_ref, fg_ref, y_ref, acc_ref, h2_ref):
    f = pl.program_id(1)
    tm = x1_ref.shape[0]

    def up(h2):
        return jnp.square(jnp.maximum(jnp.dot(h2, wu_ref[...], preferred_element_type=F32), 0.0)).astype(BF16)

    @pl.when(f == 0)
    def _():
        t_blocks = []
        for rb in range(tm // PROJ_ROWS):
            for r in range(rb * PROJ_ROWS, (rb + 1) * PROJ_ROWS, NORM_ROWS):
                h2_ref[r : r + NORM_ROWS, :] = _rms_norm(x1_ref[r : r + NORM_ROWS, :], mg_ref[...]).astype(BF16)
            t_blocks.append(up(h2_ref[rb * PROJ_ROWS : (rb + 1) * PROJ_ROWS, :]))
        t = jnp.concatenate(t_blocks, axis=0)
        acc_ref[...] = x1_ref[...] + jnp.dot(t, wd_ref[...], preferred_element_type=F32)

    last = pl.num_programs(1) - 1

    @pl.when(jnp.logical_and(f > 0, f < last))
    def _():
        acc_ref[...] += jnp.dot(up(h2_ref[...]), wd_ref[...], preferred_element_type=F32)

    @pl.when(f == last)
    def _():
        t = up(h2_ref[...])
        for rb in range(tm // PROJ_ROWS):
            rows = slice(rb * PROJ_ROWS, (rb + 1) * PROJ_ROWS)
            total = acc_ref[rows, :] + jnp.dot(t[rows, :], wd_ref[...], preferred_element_type=F32)
            y_ref[rows, :] = _rms_norm(total, fg_ref[...])


def _mlp(x1, mlp_g, w_up, w_down, final_g, *, tm, tf):
    m = x1.shape[0]
    assert D_FF // tf >= 2, "the first and the last d_ff step are distinct kernel branches"
    return pl.pallas_call(
        _mlp_kernel,
        out_shape=jax.ShapeDtypeStruct((m, D_MODEL), F32),
        grid=(m // tm, D_FF // tf),
        in_specs=[
            pl.BlockSpec((tm, D_MODEL), lambda i, f: (i, 0)),
            pl.BlockSpec((1, D_MODEL), lambda i, f: (0, 0)),
            pl.BlockSpec((D_MODEL, tf), lambda i, f: (0, f)),
            pl.BlockSpec((tf, D_MODEL), lambda i, f: (f, 0)),
            pl.BlockSpec((1, D_MODEL), lambda i, f: (0, 0)),
        ],
        out_specs=pl.BlockSpec((tm, D_MODEL), lambda i, f: (i, 0)),
        scratch_shapes=[pltpu.VMEM((tm, D_MODEL), F32), pltpu.VMEM((tm, D_MODEL), BF16)],
        compiler_params=_cparams("arbitrary", "arbitrary"),
        name="mlp",
    )(x1, mlp_g.reshape(1, D_MODEL), w_up, w_down, final_g.reshape(1, D_MODEL))


def _mlp_first_kernel(h2_ref, wu_ref, wd_ref, acc_ref, wu16_ref, wd16_ref):
    def step(first):
        h2 = h2_ref[...]
        t_pieces = []
        for c in range(0, wu_ref.shape[1], PROJ_LANES):
            wu = wu_ref[:, c : c + PROJ_LANES].astype(BF16)
            wu16_ref[:, c : c + PROJ_LANES] = wu
            t = jnp.square(jnp.maximum(jnp.dot(h2, wu, preferred_element_type=F32), 0.0))
            t_pieces.append(t.astype(BF16))
        t16 = jnp.concatenate(t_pieces, axis=-1)
        for c in range(0, wd_ref.shape[1], PROJ_LANES):
            wd = wd_ref[:, c : c + PROJ_LANES].astype(BF16)
            wd16_ref[:, c : c + PROJ_LANES] = wd
            part = jnp.dot(t16, wd, preferred_element_type=F32)
            if first:
                acc_ref[:, c : c + PROJ_LANES] = part
            else:
                acc_ref[:, c : c + PROJ_LANES] += part

    pl.when(pl.program_id(0) == 0)(lambda: step(True))
    pl.when(pl.program_id(0) > 0)(lambda: step(False))


def _mlp_first(h2, w_up, w_down, *, tf):
    m = h2.shape[0]
    resident = pl.BlockSpec((m, D_MODEL), lambda f: (0, 0))
    return pl.pallas_call(
        _mlp_first_kernel,
        out_shape=[
            jax.ShapeDtypeStruct((m, D_MODEL), F32),
            jax.ShapeDtypeStruct((D_MODEL, D_FF), BF16),
            jax.ShapeDtypeStruct((D_FF, D_MODEL), BF16),
        ],
        grid=(D_FF // tf,),
        in_specs=[
            resident,
            pl.BlockSpec((D_MODEL, tf), lambda f: (0, f)),
            pl.BlockSpec((tf, D_MODEL), lambda f: (f, 0)),
        ],
        out_specs=[
            resident,
            pl.BlockSpec((D_MODEL, tf), lambda f: (0, f)),
            pl.BlockSpec((tf, D_MODEL), lambda f: (f, 0)),
        ],
        compiler_params=_cparams("arbitrary"),
        name="mlp_first",
    )(h2, w_up, w_down)


def _residual_norm_kernel(x_ref, d_ref, g_ref, y_ref):
    y_ref[...] = _rms_norm(x_ref[...] + d_ref[...], g_ref[...])


def _residual_norm(x, d, g, *, tm):
    m = x.shape[0]
    row_spec = pl.BlockSpec((tm, D_MODEL), lambda i: (i, 0))
    return pl.pallas_call(
        _residual_norm_kernel,
        out_shape=jax.ShapeDtypeStruct((m, D_MODEL), F32),
        grid=(m // tm,),
        in_specs=[row_spec, row_spec, pl.BlockSpec((1, D_MODEL), lambda i: (0, 0))],
        out_specs=row_spec,
        compiler_params=_cparams("arbitrary"),
        name="residual_norm",
    )(x, d, g.reshape(1, D_MODEL))


SAMPLE_ATTN_SEQS = 4


def _branches_kernel(u_ref, v_ref, g_ref, b_ref, sw_ref, sb_ref, qp_ref, kp_ref, vp_ref, qs_ref, ks_ref, vs_ref,
                     a_ref, op_ref, os_ref, *, chunk, rows_p, seq_s):
    _sgu_kernel(u_ref, v_ref, g_ref, b_ref, sw_ref, sb_ref, a_ref, chunk=chunk)
    _attn_kernel(qp_ref, kp_ref, vp_ref, op_ref, rows=rows_p)
    _attn_kernel(qs_ref, ks_ref, vs_ref, os_ref, rows=seq_s)


def _branches(z_p, cols_p, seq_p, kv_p, z_s, cols_s, seq_s, k_s, v_s, ln_g, ln_b, sgu_w, sgu_b):
    m_p, m_s = z_p.shape[-2], z_s.shape[-2]
    steps = m_s // (seq_s * SAMPLE_ATTN_SEQS)
    rows_p = m_p // steps
    assert rows_p % CHUNK == 0 and seq_p % rows_p == 0
    steps_per_seq = seq_p // rows_p
    rows_s = seq_s * SAMPLE_ATTN_SEQS
    const2 = lambda i: (0, 0)
    kv_p_specs = [pl.BlockSpec((1, N_MEM, D_C), functools.partial(lambda i, col: (i // steps_per_seq, 0, col), col=col))
                  for col in (0, 1)]
    kv_s_spec = pl.BlockSpec((SAMPLE_ATTN_SEQS, N_MEM * HEAD_ROW_PITCH, LANES), lambda i: (i, 0, 0))
    return pl.pallas_call(
        functools.partial(_branches_kernel, chunk=min(seq_p, CHUNK), rows_p=rows_p, seq_s=seq_s),
        out_shape=[
            jax.ShapeDtypeStruct((m_p, D_A), BF16),
            jax.ShapeDtypeStruct((m_p, D_C), BF16),
            jax.ShapeDtypeStruct((m_s, D_C), BF16),
        ],
        grid=(steps,),
        in_specs=[
            _col_block_spec(z_p, rows_p, cols_p.u),
            _col_block_spec(z_p, rows_p, cols_p.v),
            pl.BlockSpec((1, D_A), const2),
            pl.BlockSpec((1, D_A), const2),
            pl.BlockSpec((SGU_GROUPS, CHUNK, CHUNK), lambda i: (0, 0, 0)),
            pl.BlockSpec((SGU_GROUPS, CHUNK), const2),
            _col_block_spec(z_p, rows_p, cols_p.q),
            *kv_p_specs,
            _col_block_spec(z_s, rows_s, cols_s.q),
            kv_s_spec,
            kv_s_spec,
        ],
        out_specs=[
            pl.BlockSpec((rows_p, D_A), lambda i: (i, 0)),
            pl.BlockSpec((rows_p, D_C), lambda i: (i, 0)),
            pl.BlockSpec((rows_s, D_C), lambda i: (i, 0)),
        ],
        compiler_params=_cparams("arbitrary"),
        name="branches",
    )(z_p, z_p, ln_g.reshape(1, D_A), ln_b.reshape(1, D_A), sgu_w, sgu_b, z_p, kv_p, kv_p, z_s, k_s, v_s)


def kernel(x_prompt, x_sample, mem_prompt, cache_mem_k, cache_mem_v, state_conv, attn_norm_g, w_in, b_gate,
           sgu_ln_g, sgu_ln_b, sgu_w, sgu_b, w_a_out, conv_w, conv_b, conv_ln_g, conv_ln_b, w_b_out, mem_norm_g,
           w_mem_kv, w_c_out, w_o, mlp_norm_g, w_up, w_down, final_norm_g):
    assert attn_norm_g.shape[0] == 1, "single-layer trunk"
    bp, sp, _ = x_prompt.shape
    bs, ss, _ = x_sample.shape
    p = dict(
        attn_norm_g=attn_norm_g[0], w_in=w_in[0], b_gate=b_gate[0], sgu_ln_g=sgu_ln_g[0],
        sgu_ln_b=sgu_ln_b[0], sgu_w=sgu_w[0], sgu_b=sgu_b[0], w_a_out=w_a_out[0],
        conv_w=conv_w[0], conv_b=conv_b[0], conv_ln_g=conv_ln_g[0], conv_ln_b=conv_ln_b[0],
        w_b_out=w_b_out[0], w_c_out=w_c_out[0], w_o=w_o[0],
        mlp_norm_g=mlp_norm_g[0], w_up=w_up[0], w_down=w_down[0],
        final_norm_g=final_norm_g,
    )
    kv, kv_split = _norm_matmul(mem_prompt.reshape(bp * N_MEM, D_MODEL), mem_norm_g[0], w_mem_kv[0],
                                tm=1024, tn=1024, emit_head_split=True)
    kv = kv.reshape(bp, N_MEM, 2 * D_C)
    xs = x_sample.reshape(bs * ss, D_MODEL)
    xp = x_prompt.reshape(bp * sp, D_MODEL)
    conv_args = (p["conv_w"], p["conv_b"], p["conv_ln_g"], p["conv_ln_b"])
    sgu_args = (p["sgu_ln_g"], p["sgu_ln_b"], p["sgu_w"], p["sgu_b"])

    z_s, w_in16 = _norm_matmul(xs, p["attn_norm_g"], p["w_in"], tm=1024, tn=1024, emit_w16=True)
    out_names = ("w_a_out", "w_b_out", "w_c_out", "w_o")
    b_s, hist_s, *rounded = _conv_sample(z_s, jnp.transpose(state_conv[0], (1, 0, 2)), ss, *conv_args, n_seq=16,
                                         also_round=[p[name] for name in out_names])
    p = {**p, **dict(zip(out_names, rounded))}
    a_s, vn_s = _sgu(z_s, Z_COLS_ALL, *sgu_args, chunk=min(ss, CHUNK), rows=1024, want_vn=True)

    z_p, b_p, hist_p = _inproj_conv(xp, sp, p["attn_norm_g"], w_in16, *conv_args, tm=1024)
    a_p, c_p, c_s = _branches(z_p, Z_COLS_NO_GLU, sp, kv, z_s, Z_COLS_ALL, ss, _split_lane_tiles(cache_mem_k[0]),
                              _split_lane_tiles(cache_mem_v[0]), *sgu_args)

    merge_w = (p["b_gate"], p["w_a_out"], p["w_b_out"], p["w_c_out"], p["w_o"], p["mlp_norm_g"])
    x1_s, h2_s = _merge(a_s, b_s, c_s, z_s, Z_COLS_ALL, xs, *merge_w, tm=256, emit_h2=True)
    d_s, w_up16, w_down16 = _mlp_first(h2_s, p["w_up"], p["w_down"], tf=512)
    y_s = _residual_norm(x1_s, d_s, p["final_norm_g"], tm=512)
    (x1_p,) = _merge(a_p, b_p, c_p, z_p, Z_COLS_NO_GLU, xp, *merge_w, tm=256, emit_h2=False)
    y_p = _mlp(x1_p, p["mlp_norm_g"], w_up16, w_down16, p["final_norm_g"], tm=512, tf=1024)
    return (
        y_p.reshape(bp, sp, D_MODEL),
        y_s.reshape(bs, ss, D_MODEL),
        _merge_lane_tiles(kv_split[0], bp)[None],
        _merge_lane_tiles(kv_split[1], bp)[None],
        hist_p[None],
        jnp.transpose(hist_s, (1, 0, 2))[None],
        vn_s.reshape(1, bs, ss, D_A),
    )
```
